```python
import jax
import jax.numpy as jnp
from jax import lax
import numpy as np

D_MODEL = 1024
BATCH = 16
SEQ = 2048
DEPTH = 2

GRID_W = 64
CTX_LEN = 256
HEAD_DIM = 64
N_MIXERS = 4
GROUP_W = D_MODEL // N_MIXERS
MIX_W = N_MIXERS * GROUP_W
A_HEADS = GROUP_W // HEAD_DIM
A_KV_HEADS = A_HEADS // 2
WINDOW = 128
WIN_BLOCK = 128
B_HEADS = GROUP_W // HEAD_DIM
NB_ROWS = 8
NB_COLS = 16
NB_KEY_COLS = 2 * NB_COLS
CONV_W = 3
D_HEADS = GROUP_W // HEAD_DIM
SCAN_CHUNK = 64
LB_FLOOR = 1e-20
ROPE_THETA = 10000.0
AXIS_DIM = HEAD_DIM // 2
PEER_HEADS = 8
PEER_NKEYS = 128
PEER_EXPERTS = PEER_NKEYS * PEER_NKEYS
PEER_DQ = D_MODEL // 4
PEER_TOPK = 16
PEER_TOKEN_BLOCK = 128
N_MOD = 6
EPS = 1e-6
MASK_VALUE = -1e30

SPLIT_SIZES = (A_HEADS * HEAD_DIM, A_KV_HEADS * HEAD_DIM, A_KV_HEADS * HEAD_DIM,
               B_HEADS * HEAD_DIM, B_HEADS * HEAD_DIM, B_HEADS * HEAD_DIM,
               GROUP_W, GROUP_W, GROUP_W,
               GROUP_W, GROUP_W, GROUP_W, GROUP_W, GROUP_W)
IN_W = sum(SPLIT_SIZES)

kernel_name = 'hybrid_parallel_mixer_dit_peer'


def _rmsnorm(x, g):
    xf = x.astype(jnp.float32)
    y = xf * lax.rsqrt(jnp.mean(xf * xf, axis=-1, keepdims=True) + EPS)
    return (y * g.astype(jnp.float32)).astype(x.dtype)


def _modulation(cvec, w, b):
    m = jax.nn.silu(cvec) @ w + b
    if m.ndim == 2:
        m = m[:, None, :]
    return jnp.split(m, N_MOD, axis=-1)


def _split_proj(p):
    points = np.cumsum(SPLIT_SIZES)[:-1].tolist()
    return jnp.split(p, points, axis=-1)


def _heads(t):
    return t.reshape(t.shape[0], t.shape[1], -1, HEAD_DIM)


def _axial_angles(L):
    t = jnp.arange(L)
    inv = ROPE_THETA ** (-jnp.arange(0, AXIS_DIM, 2, dtype=jnp.float32) / AXIS_DIM)
    row = (t // GRID_W).astype(jnp.float32)[:, None] * inv
    col = (t % GRID_W).astype(jnp.float32)[:, None] * inv
    return row, col


def _rot_half(x, ang):
    x1, x2 = jnp.split(x, 2, axis=-1)
    c = jnp.cos(ang)[None, :, None, :]
    s = jnp.sin(ang)[None, :, None, :]
    return jnp.concatenate([x1 * c - x2 * s, x1 * s + x2 * c], axis=-1)


def _axial_rope(x, row_ang, col_ang):
    xf = x.astype(jnp.float32)
    xr, xc = jnp.split(xf, 2, axis=-1)
    return jnp.concatenate([_rot_half(xr, row_ang), _rot_half(xc, col_ang)], axis=-1).astype(x.dtype)


def _ctx_attention(q, k, v, sink):
    B, C, H, dh = q.shape
    Hk = k.shape[2]
    G = H // Hk
    qg = q.reshape(B, C, Hk, G, dh)
    s = jnp.einsum('bqhgd,bshd->bhgqs', qg, k).astype(jnp.float32) * dh ** -0.5
    if sink is not None:
        s_sink = jnp.broadcast_to(sink.astype(jnp.float32).reshape(Hk, G)[None, :, :, None, None], (B, Hk, G, C, 1))
        s = jnp.concatenate([s, s_sink], axis=-1)
    p = jax.nn.softmax(s, axis=-1)[..., :C].astype(v.dtype)
    o = jnp.einsum('bhgqs,bshd->bqhgd', p, v)
    return o.reshape(B, C, H * dh)


def _window_attention(q, k, v, k_ctx, v_ctx, sink, row_ang, col_ang):
    B, L, H, dh = q.shape
    G = H // A_KV_HEADS
    C = k_ctx.shape[1]
    nb = L // WIN_BLOCK
    span = WIN_BLOCK + 2 * WINDOW
    scale = dh ** -0.5
    q = _axial_rope(q, row_ang, col_ang)
    k = _axial_rope(k, row_ang, col_ang)
    pad = ((0, 0), (WINDOW, WINDOW), (0, 0), (0, 0))
    idx = jnp.arange(nb)[:, None] * WIN_BLOCK + jnp.arange(span)[None, :]
    kb = jnp.pad(k, pad)[:, idx]
    vb = jnp.pad(v, pad)[:, idx]
    qpos = jnp.arange(nb)[:, None] * WIN_BLOCK + jnp.arange(WIN_BLOCK)[None, :]
    kpos = idx - WINDOW
    mask = ((jnp.abs(qpos[:, :, None] - kpos[:, None, :]) <= WINDOW)
            & (kpos[:, None, :] >= 0) & (kpos[:, None, :] < L))
    qb = q.reshape(B, nb, WIN_BLOCK, A_KV_HEADS, G, dh)
    s_loc = jnp.einsum('bnqhgd,bnshd->bnhgqs', qb, kb).astype(jnp.float32) * scale
    s_loc = jnp.where(mask[None, :, None, None], s_loc, MASK_VALUE)
    s_ctx = jnp.einsum('bnqhgd,bchd->bnhgqc', qb, k_ctx).astype(jnp.float32) * scale
    s_sink = jnp.broadcast_to(sink.astype(jnp.float32).reshape(A_KV_HEADS, G)[None, None, :, :, None, None],
                              s_loc.shape[:-1] + (1,))
    p = jax.nn.softmax(jnp.concatenate([s_loc, s_ctx, s_sink], axis=-1), axis=-1)
    p_loc = p[..., :span].astype(v.dtype)
    p_ctx = p[..., span:span + C].astype(v.dtype)
    o = (jnp.einsum('bnhgqs,bnshd->bnqhgd', p_loc, vb)
         + jnp.einsum('bnhgqc,bchd->bnqhgd', p_ctx, v_ctx))
    return o.reshape(B, L, H * dh)


def _neighbourhood_attention(q, k, v, k_ctx, v_ctx, rpb):
    B, L, H, dh = q.shape
    C = k_ctx.shape[1]
    rows = L // GRID_W
    kr = min(NB_ROWS, rows)
    ncb = GRID_W // NB_COLS
    S = kr * NB_KEY_COLS
    scale = dh ** -0.5
    r = jnp.arange(rows)
    row_idx = jnp.clip(r - kr // 2, 0, rows - kr)[:, None] + jnp.arange(kr)[None, :]
    cblk = jnp.arange(ncb)
    col_idx = (jnp.clip(cblk * NB_COLS - NB_COLS // 2, 0, GRID_W - NB_KEY_COLS)[:, None]
               + jnp.arange(NB_KEY_COLS)[None, :])
    qcol = cblk[:, None] * NB_COLS + jnp.arange(NB_COLS)[None, :]
    win0 = jnp.clip(qcol - NB_COLS // 2, 0, GRID_W - NB_COLS)
    kc = col_idx[:, None, :]
    col_ok = (kc >= win0[:, :, None]) & (kc < win0[:, :, None] + NB_COLS)
    mask = jnp.broadcast_to(col_ok[:, :, None, :], (ncb, NB_COLS, kr, NB_KEY_COLS)).reshape(ncb, NB_COLS, S)
    d_row = row_idx - r[:, None]
    d_col = jnp.clip(kc - qcol[:, :, None], -(NB_COLS - 1), NB_COLS - 1)
    bias = rpb[:, d_row[:, None, None, :, None] + (NB_ROWS - 1), d_col[None, :, :, None, :] + (NB_COLS - 1)]
    bias = bias.reshape(H, rows, ncb, NB_COLS, S).transpose(1, 2, 0, 3, 4).astype(jnp.float32)
    kg = k.reshape(B, rows, GRID_W, H, dh)
    vg = v.reshape(B, rows, GRID_W, H, dh)
    ri = row_idx[:, None, :, None]
    ci = col_idx[None, :, None, :]
    kb = kg[:, ri, ci].reshape(B, rows, ncb, S, H, dh)
    vb = vg[:, ri, ci].reshape(B, rows, ncb, S, H, dh)
    qb = q.reshape(B, rows, ncb, NB_COLS, H, dh)
    s_loc = jnp.einsum('brnqhd,brnshd->brnhqs', qb, kb).astype(jnp.float32) * scale + bias[None]
    s_loc = jnp.where(mask[None, None, :, None], s_loc, MASK_VALUE)
    s_ctx = jnp.einsum('brnqhd,bchd->brnhqc', qb, k_ctx).astype(jnp.float32) * scale
    p = jax.nn.softmax(jnp.concatenate([s_loc, s_ctx], axis=-1), axis=-1)
    p_loc = p[..., :S].astype(v.dtype)
    p_ctx = p[..., S:].astype(v.dtype)
    o = (jnp.einsum('brnhqs,brnshd->brnqhd', p_loc, vb)
         + jnp.einsum('brnhqc,bchd->brnqhd', p_ctx, v_ctx))
    return o.reshape(B, L, H * dh)


def _short_conv(u, w):
    ch = u.shape[-1]
    return lax.conv_general_dilated(u, w[:, None, :].astype(u.dtype), window_strides=(1,),
                                    padding=[(CONV_W // 2, CONV_W // 2)],
                                    dimension_numbers=('NWC', 'WIO', 'NWC'),
                                    feature_group_count=ch)


def _hgrn2_gates(z, lb):
    log_lb = jnp.log(jnp.maximum(lb, LB_FLOOR))
    logf = jnp.logaddexp(log_lb, jnp.log1p(-lb) + jax.nn.log_sigmoid(z.astype(jnp.float32)))
    return -jnp.expm1(logf), logf


def _hgrn2_scan(q, k, v, logf, s0):
    B, L, H, _ = q.shape
    dv = v.shape[-1]
    n = L // SCAN_CHUNK

    def chunks(a):
        return a.astype(jnp.float32).reshape(B, n, SCAN_CHUNK, H, a.shape[-1]).transpose(1, 0, 3, 2, 4)

    tril = jnp.tril(jnp.ones((SCAN_CHUNK, SCAN_CHUNK), dtype=bool))[:, :, None]

    def step(S, inp):
        qc, kc, vc, lc = inp
        G = jnp.cumsum(lc, axis=2)
        diff = G[:, :, :, None, :] - G[:, :, None, :, :]
        decay = jnp.where(tril, jnp.exp(jnp.where(tril, diff, 0.0)), 0.0)
        attn = jnp.einsum('bhid,bhjd,bhijd->bhij', qc, kc, decay)
        o = jnp.einsum('bhij,bhjv->bhiv', attn, vc) + jnp.einsum('bhid,bhdv->bhiv', qc * jnp.exp(G), S)
        G_end = G[:, :, -1:, :]
        S = (jnp.exp(G_end[:, :, 0, :, None]) * S
             + jnp.einsum('bhjd,bhjv->bhdv', kc * jnp.exp(G_end - G), vc))
        return S, o

    S, o = lax.scan(step, s0, (chunks(q), chunks(k), chunks(v), chunks(logf)))
    o = o.transpose(1, 0, 3, 2, 4).reshape(B, L, H, dv)
    return o.astype(v.dtype), S


def _hgrn2_bidir(q, v, zf, zb, q_c, v_c, zf_c, zb_c, lb):
    B = q.shape[0]
    lb = lb.reshape(D_HEADS, HEAD_DIM)
    s0 = jnp.zeros((B, D_HEADS, HEAD_DIM, HEAD_DIM), jnp.float32)
    rev = lambda a: a[:, ::-1]
    kf, lf = _hgrn2_gates(zf, lb)
    kf_c, lf_c = _hgrn2_gates(zf_c, lb)
    kb, lbw = _hgrn2_gates(zb, lb)
    kb_c, lbw_c = _hgrn2_gates(zb_c, lb)
    oc_f, s_f = _hgrn2_scan(q_c, kf_c, v_c, lf_c, s0)
    o_f, _ = _hgrn2_scan(q, kf, v, lf, s_f)
    oc_b, s_b = _hgrn2_scan(rev(q_c), rev(kb_c), rev(v_c), rev(lbw_c), s0)
    o_b, _ = _hgrn2_scan(rev(q), rev(kb), rev(v), rev(lbw), s_b)
    return o_f + rev(o_b), oc_f + rev(oc_b)


def _merge_groups(parts, gnorm):
    gains = jnp.split(gnorm, N_MIXERS)
    return jnp.concatenate([_rmsnorm(p, g) for p, g in zip(parts, gains)], axis=-1)


def _mixers(hx, hc, w_in, conv_w, sink, rpb, lb, gnorm, w_out, row_ang, col_ang, with_ctx):
    B, L, _ = hx.shape
    C = hc.shape[1]
    (aq, ak, av, bq, bk, bv, cx, cb, cc, dq, dzf, dzb, di, dg) = _split_proj(hx @ w_in)
    (aq_c, ak_c, av_c, bq_c, bk_c, bv_c, cx_c, cb_c, cc_c, dq_c, dzf_c, dzb_c, di_c, dg_c) = _split_proj(hc @ w_in)
    ak_c, av_c, bk_c, bv_c = _heads(ak_c), _heads(av_c), _heads(bk_c), _heads(bv_c)
    y_a = _window_attention(_heads(aq), _heads(ak), _heads(av), ak_c, av_c, sink, row_ang, col_ang)
    y_b = _neighbourhood_attention(_heads(bq), _heads(bk), _heads(bv), bk_c, bv_c, rpb)
    y_c = cb * _short_conv(cc * cx, conv_w)
    o_d, o_dc = _hgrn2_bidir(_heads(dq), _heads(di), _heads(dzf), _heads(dzb),
                             _heads(dq_c), _heads(di_c), _heads(dzf_c), _heads(dzb_c), lb)
    y_d = o_d.reshape(B, L, GROUP_W) * jax.nn.silu(dg)
    y = _merge_groups((y_a, y_b, y_c, y_d), gnorm) @ w_out
    if not with_ctx:
        return y, None
    yc_a = _ctx_attention(_heads(aq_c), ak_c, av_c, sink)
    yc_b = _ctx_attention(_heads(bq_c), bk_c, bv_c, None)
    yc_c = cb_c * _short_conv(cc_c * cx_c, conv_w)
    yc_d = o_dc.reshape(B, C, GROUP_W) * jax.nn.silu(dg_c)
    yc = _merge_groups((yc_a, yc_b, yc_c, yc_d), gnorm) @ w_out
    return y, yc


def _peer(h, w_q, sub_k1, sub_k2, u, v):
    T, D = h.shape
    q = (h @ w_q).reshape(T, PEER_HEADS, 2, PEER_DQ // 2)
    s1 = jnp.einsum('thd,kd->thk', q[:, :, 0], sub_k1)
    s2 = jnp.einsum('thd,kd->thk', q[:, :, 1], sub_k2)
    v1, i1 = lax.top_k(s1, PEER_TOPK)
    v2, i2 = lax.top_k(s2, PEER_TOPK)
    cand_s = (v1[..., :, None] + v2[..., None, :]).reshape(T, PEER_HEADS, PEER_TOPK * PEER_TOPK)
    cand_i = (i1[..., :, None] * PEER_NKEYS + i2[..., None, :]).reshape(T, PEER_HEADS, PEER_TOPK * PEER_TOPK)
    top_s, pos = lax.top_k(cand_s, PEER_TOPK)
    idx = jnp.take_along_axis(cand_i, pos, axis=-1)
    gate = jax.nn.softmax(top_s.astype(jnp.float32), axis=-1).astype(h.dtype)
    E = PEER_HEADS * PEER_TOPK
    nblk = T // PEER_TOKEN_BLOCK

    def block(args):
        hb, ib, gb = args
        act = jax.nn.gelu(jnp.einsum('td,ted->te', hb, u[ib])) * gb
        return jnp.einsum('te,ted->td', act, v[ib])

    out = lax.map(block, (h.reshape(nblk, PEER_TOKEN_BLOCK, D),
                          idx.reshape(nblk, PEER_TOKEN_BLOCK, E),
                          gate.reshape(nblk, PEER_TOKEN_BLOCK, E)))
    return out.reshape(T, D)


def setup_inputs(seed: int = 0) -> dict:
    key = jax.random.key(seed)
    ks = jax.random.split(key, 21)
    nrm = lambda k, shape, s: jax.random.normal(k, shape, jnp.float32) * s
    return {
        'x': nrm(ks[0], (BATCH, SEQ, D_MODEL), 1.0),
        'c': nrm(ks[1], (BATCH, D_MODEL), 1.0),
        'ctx': nrm(ks[2], (BATCH, CTX_LEN, D_MODEL), 1.0),
        'c_ctx': nrm(ks[3], (D_MODEL,), 1.0),
        'w_mod': nrm(ks[4], (DEPTH, D_MODEL, N_MOD * D_MODEL), 0.5 * D_MODEL ** -0.5),
        'b_mod': nrm(ks[5], (DEPTH, N_MOD * D_MODEL), 0.02),
        'norm_mix': 1.0 + nrm(ks[6], (DEPTH, D_MODEL), 0.02),
        'norm_ffn': 1.0 + nrm(ks[7], (DEPTH, D_MODEL), 0.02),
        'w_in': nrm(ks[8], (DEPTH, D_MODEL, IN_W), D_MODEL ** -0.5),
        'conv_w': nrm(ks[9], (DEPTH, CONV_W, GROUP_W), CONV_W ** -0.5),
        'attn_sink': nrm(ks[10], (DEPTH, A_HEADS), 0.5),
        'na_rpb': nrm(ks[11], (DEPTH, B_HEADS, 2 * NB_ROWS - 1, 2 * NB_COLS - 1), 0.1),
        'lb_logits': nrm(ks[12], (DEPTH, GROUP_W), 0.5),
        'group_norm': 1.0 + nrm(ks[13], (DEPTH, MIX_W), 0.02),
        'w_out': nrm(ks[14], (DEPTH, MIX_W, D_MODEL), MIX_W ** -0.5),
        'peer_wq': nrm(ks[15], (DEPTH, D_MODEL, PEER_HEADS * PEER_DQ), D_MODEL ** -0.5),
        'peer_k1': nrm(ks[16], (DEPTH, PEER_NKEYS, PEER_DQ // 2), (PEER_DQ // 2) ** -0.5),
        'peer_k2': nrm(ks[17], (DEPTH, PEER_NKEYS, PEER_DQ // 2), (PEER_DQ // 2) ** -0.5),
        'peer_u': nrm(ks[18], (DEPTH, PEER_EXPERTS, D_MODEL), D_MODEL ** -0.5),
        'peer_v': nrm(ks[19], (DEPTH, PEER_EXPERTS, D_MODEL), PEER_HEADS ** -0.5),
        'final_norm': 1.0 + nrm(ks[20], (D_MODEL,), 0.02),
    }


def reference(x, c, ctx, c_ctx, w_mod, b_mod, norm_mix, norm_ffn, w_in, conv_w, attn_sink, na_rpb,
              lb_logits, group_norm, w_out, peer_wq, peer_k1, peer_k2, peer_u, peer_v, final_norm):
    B, L, D = x.shape
    C = ctx.shape[1]
    row_ang, col_ang = _axial_angles(L)
    lb_soft = jax.nn.softmax(lb_logits.astype(jnp.float32), axis=0)
    lower_bounds = jnp.cumsum(lb_soft, axis=0) - lb_soft[0:1]
    for l in range(DEPTH):
        last = l == DEPTH - 1
        sh1, sc1, g1, sh2, sc2, g2 = _modulation(c, w_mod[l], b_mod[l])
        sh1c, sc1c, g1c, sh2c, sc2c, g2c = _modulation(c_ctx, w_mod[l], b_mod[l])
        hx = _rmsnorm(x, norm_mix[l]) * (1 + sc1) + sh1
        hc = _rmsnorm(ctx, norm_mix[l]) * (1 + sc1c) + sh1c
        y, yc = _mixers(hx, hc, w_in[l], conv_w[l], attn_sink[l], na_rpb[l], lower_bounds[l],
                        group_norm[l], w_out[l], row_ang, col_ang, not last)
        x = x + g1 * y
        hx2 = _rmsnorm(x, norm_ffn[l]) * (1 + sc2) + sh2
        peer_args = (peer_wq[l], peer_k1[l], peer_k2[l], peer_u[l], peer_v[l])
        if last:
            x = x + g2 * _peer(hx2.reshape(B * L, D), *peer_args).reshape(B, L, D)
        else:
            ctx = ctx + g1c * yc
            hc2 = _rmsnorm(ctx, norm_ffn[l]) * (1 + sc2c) + sh2c
            f = _peer(jnp.concatenate([hc2, hx2], axis=1).reshape(B * (C + L), D), *peer_args)
            f = f.reshape(B, C + L, D)
            ctx = ctx + g2c * f[:, :C]
            x = x + g2 * f[:, C:]
    return _rmsnorm(x, final_norm)
```

```python
import functools
import math

import jax
import jax.numpy as jnp
import numpy as np
from jax import lax
from jax.experimental import pallas as pl
from jax.experimental.pallas import tpu as pltpu

F32 = jnp.float32
BF16 = jnp.bfloat16

GRID_W = 64
HEAD_DIM = 64
GROUP_W = 256
A_HEADS = 4
A_KV_HEADS = 2
WINDOW = 128
WIN_BLOCK = 128
B_HEADS = 4
NB_ROWS = 8
NB_COLS = 16
CONV_W = 3
D_HEADS = 4
SCAN_CHUNK = 64
LB_FLOOR = 1e-20
ROPE_THETA = 10000.0
AXIS_DIM = HEAD_DIM // 2
PEER_HEADS = 8
PEER_NKEYS = 128
PEER_TOPK = 16
N_MOD = 6
EPS = 1e-6
MASK_VALUE = -1e30
IN_W = 13 * GROUP_W

ROW_TILE = 256
MOD_COL_TILE = 1536
TOPK_TOKENS = 128
PEER_TOKENS = 8
VMEM_LIMIT = 48 * 1024 * 1024


def _cparams(n_axes):
    return pltpu.CompilerParams(dimension_semantics=("arbitrary",) * n_axes, vmem_limit_bytes=VMEM_LIMIT)


def _dot(a, b):
    return jnp.dot(a, b, preferred_element_type=F32)


def _dot_nt(a, b):
    return lax.dot_general(a, b, (((1,), (1,)), ((), ())), preferred_element_type=F32)


def _dot_tn(a, b):
    return lax.dot_general(a, b, (((0,), (0,)), ((), ())), preferred_element_type=F32)


def _rms(x):
    return x * lax.rsqrt(jnp.mean(x * x, axis=-1, keepdims=True) + EPS)


def _silu(x):
    return x / (1.0 + jnp.exp(-x))


def _mod_kernel(c_ref, w_ref, b_ref, o_ref):
    s = _silu(c_ref[...])
    o_ref[0] = _dot(s.astype(BF16), w_ref[0].astype(BF16)) + b_ref[0]


def _modulation(cvec, w_mod, b_mod):
    depth, d, n = w_mod.shape
    r = cvec.shape[0]
    tn = MOD_COL_TILE
    return pl.pallas_call(
        _mod_kernel,
        grid=(depth, n // tn),
        in_specs=[
            pl.BlockSpec((r, d), lambda l, j: (0, 0)),
            pl.BlockSpec((1, d, tn), lambda l, j: (l, 0, j)),
            pl.BlockSpec((1, 1, tn), lambda l, j: (l, 0, j)),
        ],
        out_specs=pl.BlockSpec((1, r, tn), lambda l, j: (l, 0, j)),
        out_shape=jax.ShapeDtypeStruct((depth, r, n), F32),
        compiler_params=_cparams(2),
        name="adaln_modulation",
    )(cvec, w_mod, b_mod.reshape(depth, 1, n))


def _inproj_kernel(x_ref, mod_ref, g_ref, w_ref, o_ref):
    y = _rms(x_ref[0]) * g_ref[...]
    h = y * (1.0 + mod_ref[0, 1:2, :]) + mod_ref[0, 0:1, :]
    o_ref[0] = _dot(h.astype(BF16), w_ref[...])


def _in_projection(xc, mods, gain, w_bf16, n_ctx_tiles):
    b, s, d = xc.shape
    n = w_bf16.shape[1]
    tm = ROW_TILE
    ctx_cond = mods.shape[0] - 1
    return pl.pallas_call(
        _inproj_kernel,
        grid=(b, s // tm),
        in_specs=[
            pl.BlockSpec((1, tm, d), lambda i, j: (i, j, 0)),
            pl.BlockSpec((1, N_MOD, d), lambda i, j: (jnp.where(j < n_ctx_tiles, ctx_cond, i), 0, 0)),
            pl.BlockSpec((1, d), lambda i, j: (0, 0)),
            pl.BlockSpec((d, n), lambda i, j: (0, 0)),
        ],
        out_specs=pl.BlockSpec((1, tm, n), lambda i, j: (i, j, 0)),
        out_shape=jax.ShapeDtypeStruct((b, s, n), F32),
        compiler_params=_cparams(2),
        name="in_projection",
    )(xc, mods, gain.reshape(1, d), w_bf16)


def _rope(x, cos, sin_signed):
    n = x.shape[1]
    lane = lax.broadcasted_iota(jnp.int32, x.shape, 1)
    fwd = pltpu.roll(x, n - AXIS_DIM // 2, 1)
    bwd = pltpu.roll(x, AXIS_DIM // 2, 1)
    swapped = jnp.where((lane % AXIS_DIM) < AXIS_DIM // 2, fwd, bwd)
    return x * cos + swapped * sin_signed


def _attn_a_kernel(sink_ref, q_ref, k_ref, v_ref, cq_ref, sq_ref, ck_ref, sk_ref, o_ref, kh_ref, vh_ref,
                   *, n_ctx, n_lat, ctx_steps):
    t = pl.program_id(1)
    span = WIN_BLOCK + 2 * WINDOW
    g = A_HEADS // A_KV_HEADS

    @pl.when(t == 0)
    def _():
        kr = _rope(k_ref[0], ck_ref[...], sk_ref[...])
        v = v_ref[0]
        for h in range(A_KV_HEADS):
            kh_ref[h] = kr[:, h * HEAD_DIM:(h + 1) * HEAD_DIM].astype(BF16)
            vh_ref[h] = v[:, h * HEAD_DIM:(h + 1) * HEAD_DIM].astype(BF16)

    q = _rope(q_ref[0], cq_ref[...], sq_ref[...]) * (HEAD_DIM ** -0.5)
    row = lax.broadcasted_iota(jnp.int32, (g * WIN_BLOCK, 1), 0)

    def q_pair(hk):
        parts = [q[:, (hk * g + j) * HEAD_DIM:(hk * g + j + 1) * HEAD_DIM] for j in range(g)]
        return jnp.concatenate(parts, axis=0).astype(BF16)

    def sink_col(hk):
        col = jnp.full((g * WIN_BLOCK, 1), sink_ref[hk * g], F32)
        for j in range(1, g):
            col = jnp.where(row >= j * WIN_BLOCK, sink_ref[hk * g + j], col)
        return col

    def assemble(outs):
        pieces = []
        for hk in range(A_KV_HEADS):
            for j in range(g):
                pieces.append(outs[hk][j * WIN_BLOCK:(j + 1) * WIN_BLOCK])
        return jnp.concatenate(pieces, axis=1)

    def latent_block():
        p0 = (t - ctx_steps) * WIN_BLOCK
        ws = jnp.clip(p0 - WINDOW, 0, n_lat - span)
        kstart = pl.multiple_of(n_ctx + ws, WIN_BLOCK)
        qpos = p0 + row % WIN_BLOCK
        kpos = ws + lax.broadcasted_iota(jnp.int32, (1, span), 1)
        mask = jnp.abs(qpos - kpos) <= WINDOW
        outs = []
        for hk in range(A_KV_HEADS):
            q2 = q_pair(hk)
            s_loc = _dot_nt(q2, kh_ref[hk, pl.ds(kstart, span), :])
            s_loc = jnp.where(mask, s_loc, MASK_VALUE)
            s_ctx = _dot_nt(q2, kh_ref[hk, 0:n_ctx, :])
            sk = sink_col(hk)
            m = jnp.maximum(jnp.maximum(jnp.max(s_loc, axis=1, keepdims=True),
                                        jnp.max(s_ctx, axis=1, keepdims=True)), sk)
            p_loc = jnp.exp(s_loc - m)
            p_ctx = jnp.exp(s_ctx - m)
            den = (jnp.sum(p_loc, axis=1, keepdims=True) + jnp.sum(p_ctx, axis=1, keepdims=True)
                   + jnp.exp(sk - m))
            o = (_dot(p_loc.astype(BF16), vh_ref[hk, pl.ds(kstart, span), :])
                 + _dot(p_ctx.astype(BF16), vh_ref[hk, 0:n_ctx, :]))
            outs.append(o / den)
        o_ref[0] = assemble(outs)

    def ctx_block():
        outs = []
        for hk in range(A_KV_HEADS):
            q2 = q_pair(hk)
            s_ctx = _dot_nt(q2, kh_ref[hk, 0:n_ctx, :])
            sk = sink_col(hk)
            m = jnp.maximum(jnp.max(s_ctx, axis=1, keepdims=True), sk)
            p_ctx = jnp.exp(s_ctx - m)
            den = jnp.sum(p_ctx, axis=1, keepdims=True) + jnp.exp(sk - m)
            outs.append(_dot(p_ctx.astype(BF16), vh_ref[hk, 0:n_ctx, :]) / den)
        o_ref[0] = assemble(outs)

    if ctx_steps:
        pl.when(t < ctx_steps)(ctx_block)
        pl.when(t >= ctx_steps)(latent_block)
    else:
        latent_block()


def _attention_a(proj, cos_t, sin_t, sink, n_ctx, with_ctx):
    b, s, _ = proj.shape
    n_lat = s - n_ctx
    ctx_blocks = n_ctx // WIN_BLOCK
    ctx_steps = ctx_blocks if with_ctx else 0
    off = ctx_blocks - ctx_steps
    steps = ctx_steps + n_lat // WIN_BLOCK
    kw = A_KV_HEADS * HEAD_DIM
    kernel = functools.partial(_attn_a_kernel, n_ctx=n_ctx, n_lat=n_lat, ctx_steps=ctx_steps)
    return pl.pallas_call(
        kernel,
        grid=(b, steps),
        in_specs=[
            pl.BlockSpec(memory_space=pltpu.SMEM),
            pl.BlockSpec((1, WIN_BLOCK, GROUP_W), lambda i, t: (i, t + off, 0)),
            pl.BlockSpec((1, s, kw), lambda i, t: (i, 0, 2)),
            pl.BlockSpec((1, s, kw), lambda i, t: (i, 0, 3)),
            pl.BlockSpec((WIN_BLOCK, GROUP_W), lambda i, t: (t + off, 0)),
            pl.BlockSpec((WIN_BLOCK, GROUP_W), lambda i, t: (t + off, 0)),
            pl.BlockSpec((s, kw), lambda i, t: (0, 0)),
            pl.BlockSpec((s, kw), lambda i, t: (0, 0)),
        ],
        out_specs=pl.BlockSpec((1, WIN_BLOCK, GROUP_W), lambda i, t: (i, t, 0)),
        out_shape=jax.ShapeDtypeStruct((b, steps * WIN_BLOCK, GROUP_W), F32),
        scratch_shapes=[pltpu.VMEM((A_KV_HEADS, s, HEAD_DIM), BF16), pltpu.VMEM((A_KV_HEADS, s, HEAD_DIM), BF16)],
        compiler_params=_cparams(2),
        name="window_attention",
    )(sink, proj, proj, proj, cos_t, sin_t, cos_t, sin_t)


def _attn_b_kernel(q_ref, k_ref, v_ref, bias_ref, o_ref, kh_ref, vh_ref, *, n_ctx, n_lat, ctx_steps):
    t = pl.program_id(1)
    rows = n_lat // GRID_W
    nkeys = NB_ROWS * GRID_W

    @pl.when(t == 0)
    def _():
        k = k_ref[0]
        v = v_ref[0]
        for h in range(B_HEADS):
            kh_ref[h] = k[:, h * HEAD_DIM:(h + 1) * HEAD_DIM].astype(BF16)
            vh_ref[h] = v[:, h * HEAD_DIM:(h + 1) * HEAD_DIM].astype(BF16)

    q = q_ref[0] * (HEAD_DIM ** -0.5)

    def latent_block():
        r = t - ctx_steps
        r0 = jnp.clip(r - NB_ROWS // 2, 0, rows - NB_ROWS)
        kstart = pl.multiple_of(n_ctx + r0 * GRID_W, GRID_W)
        outs = []
        for h in range(B_HEADS):
            qh = q[:, h * HEAD_DIM:(h + 1) * HEAD_DIM].astype(BF16)
            s_loc = _dot_nt(qh, kh_ref[h, pl.ds(kstart, nkeys), :]) + bias_ref[0, h]
            s_ctx = _dot_nt(qh, kh_ref[h, 0:n_ctx, :])
            m = jnp.maximum(jnp.max(s_loc, axis=1, keepdims=True), jnp.max(s_ctx, axis=1, keepdims=True))
            p_loc = jnp.exp(s_loc - m)
            p_ctx = jnp.exp(s_ctx - m)
            den = jnp.sum(p_loc, axis=1, keepdims=True) + jnp.sum(p_ctx, axis=1, keepdims=True)
            o = (_dot(p_loc.astype(BF16), vh_ref[h, pl.ds(kstart, nkeys), :])
                 + _dot(p_ctx.astype(BF16), vh_ref[h, 0:n_ctx, :]))
            outs.append(o / den)
        o_ref[0] = jnp.concatenate(outs, axis=1)

    def ctx_block():
        outs = []
        for h in range(B_HEADS):
            qh = q[:, h * HEAD_DIM:(h + 1) * HEAD_DIM].astype(BF16)
            s_ctx = _dot_nt(qh, kh_ref[h, 0:n_ctx, :])
            m = jnp.max(s_ctx, axis=1, keepdims=True)
            p_ctx = jnp.exp(s_ctx - m)
            den = jnp.sum(p_ctx, axis=1, keepdims=True)
            outs.append(_dot(p_ctx.astype(BF16), vh_ref[h, 0:n_ctx, :]) / den)
        o_ref[0] = jnp.concatenate(outs, axis=1)

    if ctx_steps:
        pl.when(t < ctx_steps)(ctx_block)
        pl.when(t >= ctx_steps)(latent_block)
    else:
        latent_block()


def _attention_b(proj, bias, n_ctx, with_ctx):
    b, s, _ = proj.shape
    n_lat = s - n_ctx
    rows = n_lat // GRID_W
    ctx_blocks = n_ctx // GRID_W
    ctx_steps = ctx_blocks if with_ctx else 0
    off = ctx_blocks - ctx_steps
    steps = ctx_steps + rows

    def bias_index(i, t):
        r = jnp.maximum(t - ctx_steps, 0)
        r0 = jnp.clip(r - NB_ROWS // 2, 0, rows - NB_ROWS)
        return (r - r0, 0, 0, 0)

    kernel = functools.partial(_attn_b_kernel, n_ctx=n_ctx, n_lat=n_lat, ctx_steps=ctx_steps)
    return pl.pallas_call(
        kernel,
        grid=(b, steps),
        in_specs=[
            pl.BlockSpec((1, GRID_W, GROUP_W), lambda i, t: (i, t + off, 2)),
            pl.BlockSpec((1, s, GROUP_W), lambda i, t: (i, 0, 3)),
            pl.BlockSpec((1, s, GROUP_W), lambda i, t: (i, 0, 4)),
            pl.BlockSpec((1, B_HEADS, GRID_W, NB_ROWS * GRID_W), bias_index),
        ],
        out_specs=pl.BlockSpec((1, GRID_W, GROUP_W), lambda i, t: (i, t, 0)),
        out_shape=jax.ShapeDtypeStruct((b, steps * GRID_W, GROUP_W), F32),
        scratch_shapes=[pltpu.VMEM((B_HEADS, s, HEAD_DIM), BF16), pltpu.VMEM((B_HEADS, s, HEAD_DIM), BF16)],
        compiler_params=_cparams(2),
        name="neighbourhood_attention",
    )(proj, proj, proj, bias)


def _neighbourhood_bias(rpb):
    qc = np.arange(GRID_W)[:, None]
    kc = np.arange(GRID_W)[None, :]
    win0 = np.clip(qc - NB_COLS // 2, 0, GRID_W - NB_COLS)
    ok = (kc >= win0) & (kc < win0 + NB_COLS)
    d_col = np.clip(kc - qc, -(NB_COLS - 1), NB_COLS - 1) + (NB_COLS - 1)
    rel = np.arange(NB_ROWS)[:, None]
    kr = np.arange(NB_ROWS)[None, :]
    d_row = kr - rel + (NB_ROWS - 1)
    tab = rpb[:, d_row[:, None, :, None], d_col[None, :, None, :]]
    tab = jnp.where(ok[None, None, :, None, :], tab.astype(F32), MASK_VALUE)
    tab = tab.transpose(1, 0, 2, 3, 4)
    return tab.reshape(NB_ROWS, B_HEADS, GRID_W, NB_ROWS * GRID_W)


def _conv_kernel(cx_ref, cb_ref, cc_ref, w_ref, o_ref, pad_ref, *, n_ctx, seq):
    edge = 8
    chunk = ROW_TILE
    pad_ref[0:edge, :] = jnp.zeros((edge, GROUP_W), F32)
    pad_ref[edge + seq:2 * edge + seq, :] = jnp.zeros((edge, GROUP_W), F32)
    for c in range(seq // chunk):
        sl = slice(c * chunk, (c + 1) * chunk)
        pad_ref[edge + c * chunk:edge + (c + 1) * chunk, :] = cc_ref[0, sl, :] * cx_ref[0, sl, :]
    w = w_ref[...]
    for c in range(seq // chunk):
        lo = edge + c * chunk
        row = c * chunk + lax.broadcasted_iota(jnp.int32, (chunk, 1), 0)
        prev = pad_ref[lo - 1:lo - 1 + chunk, :]
        cur = pad_ref[lo:lo + chunk, :]
        nxt = pad_ref[lo + 1:lo + 1 + chunk, :]
        prev = jnp.where((row == 0) | (row == n_ctx), 0.0, prev)
        nxt = jnp.where((row == n_ctx - 1) | (row == seq - 1), 0.0, nxt)
        conv = w[0:1] * prev + w[1:2] * cur + w[2:3] * nxt
        o_ref[0, c * chunk:(c + 1) * chunk, :] = cb_ref[0, c * chunk:(c + 1) * chunk, :] * conv


def _short_conv(proj, conv_w, n_ctx):
    b, s, _ = proj.shape
    kernel = functools.partial(_conv_kernel, n_ctx=n_ctx, seq=s)
    return pl.pallas_call(
        kernel,
        grid=(b,),
        in_specs=[
            pl.BlockSpec((1, s, GROUP_W), lambda i: (i, 0, 5)),
            pl.BlockSpec((1, s, GROUP_W), lambda i: (i, 0, 6)),
            pl.BlockSpec((1, s, GROUP_W), lambda i: (i, 0, 7)),
            pl.BlockSpec((CONV_W, GROUP_W), lambda i: (0, 0)),
        ],
        out_specs=pl.BlockSpec((1, s, GROUP_W), lambda i: (i, 0, 0)),
        out_shape=jax.ShapeDtypeStruct((b, s, GROUP_W), F32),
        scratch_shapes=[pltpu.VMEM((s + 16, GROUP_W), F32)],
        compiler_params=_cparams(1),
        name="gated_short_conv",
    )(proj, proj, proj, conv_w)


OFFSET_GROUP = 16


def _hgrn_chunk(d, base, q_ref, z_ref, i_ref, gate_consts, o_ref, st_ref, kst, fst, vst, sst, pbuf, ones_bd, bd_mask):
    ch = SCAN_CHUNK
    sgn = 1 if d == 0 else -1
    lb_floor, one_minus_lb, floor_excess = gate_consts
    rows = pl.ds(base, ch)
    z = z_ref[0, rows, :]
    q = q_ref[0, rows, :]
    v = i_ref[0, rows, :]
    sig = 1.0 / (1.0 + jnp.exp(-z))
    f = lb_floor + one_minus_lb * sig
    k = one_minus_lb * (1.0 - sig) - floor_excess
    kst[d, ch:2 * ch, :] = k
    fst[d, ch:2 * ch, :] = f
    vst[d, ch:2 * ch, :] = v

    def shifted(ref, o):
        return ref[d, ch - sgn * o:2 * ch - sgn * o, :]

    inc = f
    exc = f
    step = 1
    while step < ch:
        sst[d, ch:2 * ch, :] = inc
        inc = inc * shifted(sst, step)
        step *= 2
    step = 1
    while step < ch:
        sst[d, ch:2 * ch, :] = exc
        exc = exc * shifted(sst, -step)
        step *= 2
    sst[d, ch:2 * ch, :] = exc
    exc = shifted(sst, -1)
    total = inc[ch - 1:ch, :] if d == 0 else inc[0:1, :]

    decay = jnp.ones((ch, GROUP_W), F32)
    acc = jnp.zeros((ch, GROUP_W), F32)
    for g0 in range(0, ch, OFFSET_GROUP):
        for oo in range(OFFSET_GROUP):
            o = g0 + oo
            if o > 0:
                decay = decay * shifted(fst, o - 1)
            pbuf[d, oo * ch:(oo + 1) * ch, :] = (q * shifted(kst, o) * decay).astype(BF16)
        head_sums = _dot(pbuf[d], ones_bd)
        for oo in range(OFFSET_GROUP):
            acc = acc + head_sums[oo * ch:(oo + 1) * ch, :] * shifted(vst, g0 + oo)

    st = st_ref[d]
    o_inter = _dot_nt((q * inc).astype(BF16), st.astype(BF16))
    upd = _dot_tn(v.astype(BF16), (k * exc).astype(BF16))
    st_ref[d] = st * total + upd * bd_mask
    o_ref[0, rows, :] = o_ref[0, rows, :] + acc + o_inter


def _hgrn_kernel(q_ref, zf_ref, zb_ref, i_ref, lb_ref, o_ref, st_ref, kst, fst, vst, sst, pbuf, *, n_ctx, seq):
    ch = SCAN_CHUNK
    n_chunks = seq // ch
    ctx_chunks = n_ctx // ch
    lb = lb_ref[...]
    lb_floor = jnp.maximum(lb, LB_FLOOR)
    gate_consts = (lb_floor, 1.0 - lb, lb_floor - lb)
    hi = lax.broadcasted_iota(jnp.int32, (GROUP_W, GROUP_W), 0) // HEAD_DIM
    hj = lax.broadcasted_iota(jnp.int32, (GROUP_W, GROUP_W), 1) // HEAD_DIM
    bd_mask = jnp.where(hi == hj, 1.0, 0.0).astype(F32)
    ones_bd = bd_mask.astype(BF16)

    o_ref[...] = jnp.zeros(o_ref.shape, F32)
    st_ref[...] = jnp.zeros(st_ref.shape, F32)
    kst[...] = jnp.zeros(kst.shape, F32)
    fst[...] = jnp.zeros(fst.shape, F32)
    vst[...] = jnp.zeros(vst.shape, F32)
    sst[...] = jnp.ones(sst.shape, F32)

    def body(s, carry):
        cf = s
        cb = jnp.where(s < ctx_chunks, ctx_chunks - 1 - s, n_chunks - 1 - (s - ctx_chunks))
        for d, c, z_ref in ((0, cf, zf_ref), (1, cb, zb_ref)):
            base = pl.multiple_of(c * ch, ch)
            _hgrn_chunk(d, base, q_ref, z_ref, i_ref, gate_consts, o_ref, st_ref, kst, fst, vst, sst, pbuf,
                        ones_bd, bd_mask)
        return carry

    lax.fori_loop(0, n_chunks, body, 0)


def _hgrn(proj, lower_bound, n_ctx):
    b, s, _ = proj.shape
    ch = SCAN_CHUNK
    kernel = functools.partial(_hgrn_kernel, n_ctx=n_ctx, seq=s)
    col = lambda c: pl.BlockSpec((1, s, GROUP_W), lambda i: (i, 0, c))
    return pl.pallas_call(
        kernel,
        grid=(b,),
        in_specs=[col(8), col(9), col(10), col(11), pl.BlockSpec((1, GROUP_W), lambda i: (0, 0))],
        out_specs=pl.BlockSpec((1, s, GROUP_W), lambda i: (i, 0, 0)),
        out_shape=jax.ShapeDtypeStruct((b, s, GROUP_W), F32),
        scratch_shapes=[
            pltpu.VMEM((2, GROUP_W, GROUP_W), F32),
            pltpu.VMEM((2, 3 * ch, GROUP_W), F32),
            pltpu.VMEM((2, 3 * ch, GROUP_W), F32),
            pltpu.VMEM((2, 3 * ch, GROUP_W), F32),
            pltpu.VMEM((2, 3 * ch, GROUP_W), F32),
            pltpu.VMEM((2, OFFSET_GROUP * ch, GROUP_W), BF16),
        ],
        compiler_params=_cparams(1),
        name="hgrn2_bidirectional",
    )(proj, proj, proj, proj, lower_bound.reshape(1, GROUP_W))


def _post_kernel(ya_ref, yb_ref, yc_ref, od_ref, dg_ref, x_ref, mod_ref, gn_ref, wo_ref, nf_ref, wq_ref,
                 k1_ref, k2_ref, xo_ref, h_ref, sc_ref):
    yd = od_ref[0] * _silu(dg_ref[0])
    gn = gn_ref[...]
    parts = (ya_ref[0], yb_ref[0], yc_ref[0], yd)
    normed = [_rms(p) * gn[:, i * GROUP_W:(i + 1) * GROUP_W] for i, p in enumerate(parts)]
    y = _dot(jnp.concatenate(normed, axis=1).astype(BF16), wo_ref[...])
    xn = x_ref[0] + mod_ref[0, 2:3, :] * y
    xo_ref[0] = xn
    h = _rms(xn) * nf_ref[...] * (1.0 + mod_ref[0, 4:5, :]) + mod_ref[0, 3:4, :]
    h_ref[0] = h
    qv = _dot(h.astype(BF16), wq_ref[...])
    half = PEER_NKEYS
    for g in range(2 * PEER_HEADS):
        keys = k1_ref[...] if g % 2 == 0 else k2_ref[...]
        sc_ref[g] = _dot_nt(keys, qv[:, g * half:(g + 1) * half].astype(BF16))


def _post_mixer(ya, yb, yc, od, proj, xc, mods, gnorm, wo_bf16, nffn, wq_bf16, k1_bf16, k2_bf16, n_ctx_tiles, tile0):
    b, s, d = xc.shape
    tm = ROW_TILE
    nt = s // tm - tile0
    s_out = nt * tm
    ctx_cond = mods.shape[0] - 1
    nq = wq_bf16.shape[1]
    att0 = tile0 - (s - ya.shape[1]) // tm
    att = lambda: pl.BlockSpec((1, tm, GROUP_W), lambda i, j: (i, j + att0, 0))
    grp = lambda: pl.BlockSpec((1, tm, GROUP_W), lambda i, j: (i, j + tile0, 0))
    const = lambda shape: pl.BlockSpec(shape, lambda i, j: (0,) * len(shape))
    return pl.pallas_call(
        _post_kernel,
        grid=(b, nt),
        in_specs=[
            att(), att(), grp(), grp(),
            pl.BlockSpec((1, tm, GROUP_W), lambda i, j: (i, j + tile0, 12)),
            pl.BlockSpec((1, tm, d), lambda i, j: (i, j + tile0, 0)),
            pl.BlockSpec((1, N_MOD, d), lambda i, j: (jnp.where(j + tile0 < n_ctx_tiles, ctx_cond, i), 0, 0)),
            const((1, d)), const((d, d)), const((1, d)), const((d, nq)),
            const((PEER_NKEYS, PEER_NKEYS)), const((PEER_NKEYS, PEER_NKEYS)),
        ],
        out_specs=[
            pl.BlockSpec((1, tm, d), lambda i, j: (i, j, 0)),
            pl.BlockSpec((1, tm, d), lambda i, j: (i, j, 0)),
            pl.BlockSpec((2 * PEER_HEADS, PEER_NKEYS, tm), lambda i, j: (0, 0, i * nt + j)),
        ],
        out_shape=[
            jax.ShapeDtypeStruct((b, s_out, d), F32),
            jax.ShapeDtypeStruct((b, s_out, d), F32),
            jax.ShapeDtypeStruct((2 * PEER_HEADS, PEER_NKEYS, b * s_out), F32),
        ],
        compiler_params=_cparams(2),
        name="post_mixer",
    )(ya, yb, yc, od, proj, xc, mods, gnorm.reshape(1, d), wo_bf16, nffn.reshape(1, d), wq_bf16, k1_bf16, k2_bf16)


def _top16(s):
    n = s.shape[0]
    rows = lax.broadcasted_iota(jnp.int32, s.shape, 0).astype(F32)
    vals, ids = [], []
    cur = s
    for _ in range(PEER_TOPK):
        m = jnp.max(cur, axis=0, keepdims=True)
        am = jnp.min(jnp.where(cur == m, rows, float(n)), axis=0, keepdims=True)
        vals.append(m)
        ids.append(am)
        cur = jnp.where(rows == am, -jnp.inf, cur)
    return jnp.concatenate(vals, axis=0), jnp.concatenate(ids, axis=0)


def _pick(table, sel):
    out = jnp.zeros(sel.shape, table.dtype)
    for a in range(PEER_TOPK):
        out = out + jnp.where(sel == float(a), table[a:a + 1, :], 0.0)
    return out


def _route_kernel(sc_ref, idx_ref, gate_ref):
    for h in range(PEER_HEADS):
        v1, i1 = _top16(sc_ref[2 * h])
        v2, i2 = _top16(sc_ref[2 * h + 1])
        cand = jnp.concatenate([v1[a:a + 1, :] + v2 for a in range(PEER_TOPK)], axis=0)
        top_s, pos = _top16(cand)
        a_sel = jnp.floor(pos * (1.0 / PEER_TOPK))
        b_sel = pos - a_sel * PEER_TOPK
        idx_ref[h] = (_pick(i1, a_sel) * PEER_NKEYS + _pick(i2, b_sel)).astype(jnp.int32)
        e = jnp.exp(top_s - jnp.max(top_s, axis=0, keepdims=True))
        gate_ref[h] = e / jnp.sum(e, axis=0, keepdims=True)


def _routing(scores_t):
    g, nk, t = scores_t.shape
    tt = TOPK_TOKENS
    out_spec = pl.BlockSpec((PEER_HEADS, PEER_TOPK, tt), lambda i: (0, 0, i))
    return pl.pallas_call(
        _route_kernel,
        grid=(t // tt,),
        in_specs=[pl.BlockSpec((g, nk, tt), lambda i: (0, 0, i))],
        out_specs=[out_spec, out_spec],
        out_shape=[jax.ShapeDtypeStruct((PEER_HEADS, PEER_TOPK, t), jnp.int32),
                   jax.ShapeDtypeStruct((PEER_HEADS, PEER_TOPK, t), F32)],
        compiler_params=_cparams(1),
        name="peer_routing",
    )(scores_t)


def _gelu_tanh(x):
    return 0.5 * x * (1.0 + jnp.tanh(math.sqrt(2.0 / math.pi) * (x + 0.044715 * (x * x * x))))


def _peer_kernel(idx_hbm, uv_hbm, h_ref, gate_ref, x_ref, mod_ref, fn_ref, o_ref, idx_smem, buf, row_sem, idx_sem,
                 *, n_blocks, final):
    i = pl.program_id(0)
    tb = PEER_TOKENS
    per_tok = PEER_HEADS * PEER_TOPK
    npair = tb * per_tok
    d = h_ref.shape[1]

    def idx_copy(blk, slot):
        return pltpu.make_async_copy(idx_hbm.at[pl.ds(blk * npair, npair)],
                                     idx_smem.at[pl.ds(slot * npair, npair)], idx_sem.at[slot])

    def issue_rows(slot):
        def body(r, carry):
            e = idx_smem[slot * npair + r]
            pltpu.make_async_copy(uv_hbm.at[pl.ds(e, 1)], buf.at[slot, pl.ds(r, 1)], row_sem.at[slot]).start()
            return carry
        lax.fori_loop(0, npair, body, 0, unroll=8)

    def wait_rows(slot):
        pltpu.make_async_copy(buf.at[slot], buf.at[slot], row_sem.at[slot]).wait()

    @pl.when(i == 0)
    def _():
        first = idx_copy(0, 0)
        first.start()
        first.wait()
        issue_rows(0)
        if n_blocks > 1:
            idx_copy(1, 1).start()

    @pl.when(i + 1 < n_blocks)
    def _():
        nslot = (i + 1) % 2
        idx_copy(i + 1, nslot).wait()
        issue_rows(nslot)

    @pl.when(i + 2 < n_blocks)
    def _():
        idx_copy(i + 2, i % 2).start()

    slot = i % 2
    wait_rows(slot)
    eye = (lax.broadcasted_iota(jnp.int32, (per_tok, per_tok), 0)
           == lax.broadcasted_iota(jnp.int32, (per_tok, per_tok), 1))
    outs = []
    for t in range(tb):
        rows = slice(t * per_tok, (t + 1) * per_tok)
        u = buf[slot, rows, 0:d]
        s = jnp.sum(u * h_ref[t:t + 1, :], axis=1, keepdims=True)
        gate_col = jnp.sum(jnp.where(eye, gate_ref[t:t + 1, :], 0.0), axis=1, keepdims=True)
        act = _gelu_tanh(s) * gate_col
        v = buf[slot, rows, d:2 * d]
        outs.append(jnp.sum(act * v, axis=0, keepdims=True))
    xn = x_ref[...] + mod_ref[0, 5:6, :] * jnp.concatenate(outs, axis=0)
    if final:
        xn = _rms(xn) * fn_ref[...]
    o_ref[...] = xn


def _peer_experts(h2, idx, gate, uv, x_mid, mods, final_gain, s_out, ctx_rows, final):
    t, d = h2.shape
    tb = PEER_TOKENS
    per_tok = PEER_HEADS * PEER_TOPK
    n_blocks = t // tb
    ctx_cond = mods.shape[0] - 1

    def mod_index(i):
        tok = i * tb
        return (jnp.where(tok % s_out < ctx_rows, ctx_cond, tok // s_out), 0, 0)

    kernel = functools.partial(_peer_kernel, n_blocks=n_blocks, final=final)
    return pl.pallas_call(
        kernel,
        grid=(n_blocks,),
        in_specs=[
            pl.BlockSpec(memory_space=pl.ANY),
            pl.BlockSpec(memory_space=pl.ANY),
            pl.BlockSpec((tb, d), lambda i: (i, 0)),
            pl.BlockSpec((tb, per_tok), lambda i: (i, 0)),
            pl.BlockSpec((tb, d), lambda i: (i, 0)),
            pl.BlockSpec((1, N_MOD, d), mod_index),
            pl.BlockSpec((1, d), lambda i: (0, 0)),
        ],
        out_specs=pl.BlockSpec((tb, d), lambda i: (i, 0)),
        out_shape=jax.ShapeDtypeStruct((t, d), F32),
        scratch_shapes=[
            pltpu.SMEM((2 * tb * per_tok,), jnp.int32),
            pltpu.VMEM((2, tb * per_tok, 2 * d), F32),
            pltpu.SemaphoreType.DMA((2,)),
            pltpu.SemaphoreType.DMA((2,)),
        ],
        compiler_params=_cparams(1),
        name="peer_experts",
    )(idx, uv, h2, gate, x_mid, mods, final_gain.reshape(1, d))


def _rope_tables(n_ctx, n_lat):
    t = jnp.arange(n_lat)
    inv = ROPE_THETA ** (-jnp.arange(0, AXIS_DIM, 2, dtype=F32) / AXIS_DIM)
    row = (t // GRID_W).astype(F32)[:, None] * inv
    col = (t % GRID_W).astype(F32)[:, None] * inv
    cos = jnp.concatenate([jnp.cos(row), jnp.cos(row), jnp.cos(col), jnp.cos(col)], axis=1)
    sin = jnp.concatenate([-jnp.sin(row), jnp.sin(row), -jnp.sin(col), jnp.sin(col)], axis=1)
    cos = jnp.concatenate([jnp.ones((n_ctx, HEAD_DIM), F32), cos], axis=0)
    sin = jnp.concatenate([jnp.zeros((n_ctx, HEAD_DIM), F32), sin], axis=0)
    return jnp.tile(cos, (1, A_HEADS)), jnp.tile(sin, (1, A_HEADS))


def kernel(x, c, ctx, c_ctx, w_mod, b_mod, norm_mix, norm_ffn, w_in, conv_w, attn_sink, na_rpb, lb_logits, group_norm, w_out, peer_wq, peer_k1, peer_k2, peer_u, peer_v, final_norm):
    b, n_lat, d = x.shape
    n_ctx = ctx.shape[1]
    depth = w_mod.shape[0]
    seq = n_ctx + n_lat
    assert n_ctx % ROW_TILE == 0 and n_lat % ROW_TILE == 0 and n_lat // GRID_W >= NB_ROWS
    assert n_lat >= WIN_BLOCK + 2 * WINDOW and w_in.shape[2] == IN_W
    n_ctx_tiles = n_ctx // ROW_TILE

    n_cond = b + 1
    pad = (-n_cond) % 8
    cvec = jnp.concatenate([c, c_ctx[None, :], jnp.zeros((pad, d), F32)], axis=0)
    mods = _modulation(cvec, w_mod, b_mod)[:, :n_cond].reshape(depth, n_cond, N_MOD, d)

    lb_soft = jax.nn.softmax(lb_logits.astype(F32), axis=0)
    lower_bounds = jnp.cumsum(lb_soft, axis=0) - lb_soft[0:1]
    cos_t, sin_t = _rope_tables(n_ctx, n_lat)

    xc = jnp.concatenate([ctx, x], axis=1)
    out = None
    for l in range(depth):
        last = l == depth - 1
        proj = _in_projection(xc, mods[l], norm_mix[l], w_in[l].astype(BF16), n_ctx_tiles)
        ya = _attention_a(proj, cos_t, sin_t, attn_sink[l], n_ctx, not last)
        yb = _attention_b(proj, _neighbourhood_bias(na_rpb[l]), n_ctx, not last)
        yc = _short_conv(proj, conv_w[l], n_ctx)
        od = _hgrn(proj, lower_bounds[l], n_ctx)
        tile0 = n_ctx_tiles if last else 0
        x_mid, h2, scores_t = _post_mixer(ya, yb, yc, od, proj, xc, mods[l], group_norm[l], w_out[l].astype(BF16),
                                          norm_ffn[l], peer_wq[l].astype(BF16), peer_k1[l].astype(BF16),
                                          peer_k2[l].astype(BF16), n_ctx_tiles, tile0)
        s_out = x_mid.shape[1]
        tokens = b * s_out
        idx_t, gate_t = _routing(scores_t)
        per_tok = PEER_HEADS * PEER_TOPK
        idx = idx_t.reshape(per_tok, tokens).T.reshape(tokens * per_tok)
        gate = gate_t.reshape(per_tok, tokens).T
        uv = jnp.concatenate([peer_u[l], peer_v[l]], axis=1)
        res = _peer_experts(h2.reshape(tokens, d), idx, gate, uv, x_mid.reshape(tokens, d), mods[l], final_norm,
                            s_out, 0 if last else n_ctx, last)
        if last:
            out = res.reshape(b, s_out, d)
        else:
            xc = res.reshape(b, s_out, d)
    return out
```

```python
import functools
import math

import jax
import jax.numpy as jnp
import numpy as np
from jax import lax
from jax.experimental import pallas as pl
from jax.experimental.pallas import tpu as pltpu

F32 = jnp.float32
BF16 = jnp.bfloat16

GRID_W = 64
HEAD_DIM = 64
GROUP_W = 256
A_HEADS = 4
A_KV_HEADS = 2
WINDOW = 128
WIN_BLOCK = 128
B_HEADS = 4
NB_ROWS = 8
NB_COLS = 16
CONV_W = 3
D_HEADS = 4
SCAN_CHUNK = 64
LB_FLOOR = 1e-20
ROPE_THETA = 10000.0
AXIS_DIM = HEAD_DIM // 2
PEER_HEADS = 8
PEER_NKEYS = 128
PEER_TOPK = 16
N_MOD = 6
EPS = 1e-6
MASK_VALUE = -1e30
IN_W = 13 * GROUP_W

ROW_TILE = 256
MOD_COL_TILE = 1536
TOPK_TOKENS = 128
PEER_TOKENS = 8
VMEM_LIMIT = 48 * 1024 * 1024


def _cparams(n_axes):
    return pltpu.CompilerParams(dimension_semantics=("arbitrary",) * n_axes, vmem_limit_bytes=VMEM_LIMIT)


def _dot(a, b):
    return jnp.dot(a, b, preferred_element_type=F32)


def _dot_nt(a, b):
    return lax.dot_general(a, b, (((1,), (1,)), ((), ())), preferred_element_type=F32)


def _dot_tn(a, b):
    return lax.dot_general(a, b, (((0,), (0,)), ((), ())), preferred_element_type=F32)


def _rms(x):
    return x * lax.rsqrt(jnp.mean(x * x, axis=-1, keepdims=True) + EPS)


def _silu(x):
    return x / (1.0 + jnp.exp(-x))


def _mod_kernel(c_ref, w_ref, b_ref, o_ref):
    s = _silu(c_ref[...])
    o_ref[0] = _dot(s.astype(BF16), w_ref[0].astype(BF16)) + b_ref[0]


def _modulation(cvec, w_mod, b_mod):
    depth, d, n = w_mod.shape
    r = cvec.shape[0]
    tn = MOD_COL_TILE
    return pl.pallas_call(
        _mod_kernel,
        grid=(depth, n // tn),
        in_specs=[
            pl.BlockSpec((r, d), lambda l, j: (0, 0)),
            pl.BlockSpec((1, d, tn), lambda l, j: (l, 0, j)),
            pl.BlockSpec((1, 1, tn), lambda l, j: (l, 0, j)),
        ],
        out_specs=pl.BlockSpec((1, r, tn), lambda l, j: (l, 0, j)),
        out_shape=jax.ShapeDtypeStruct((depth, r, n), F32),
        compiler_params=_cparams(2),
        name="adaln_modulation",
    )(cvec, w_mod, b_mod.reshape(depth, 1, n))


def _inproj_kernel(x_ref, mod_ref, g_ref, w_ref, o_ref):
    y = _rms(x_ref[0]) * g_ref[...]
    h = y * (1.0 + mod_ref[0, 1:2, :]) + mod_ref[0, 0:1, :]
    o_ref[0] = _dot(h.astype(BF16), w_ref[...])


def _in_projection(xc, mods, gain, w_bf16, n_ctx_tiles):
    b, s, d = xc.shape
    n = w_bf16.shape[1]
    tm = ROW_TILE
    ctx_cond = mods.shape[0] - 1
    return pl.pallas_call(
        _inproj_kernel,
        grid=(b, s // tm),
        in_specs=[
            pl.BlockSpec((1, tm, d), lambda i, j: (i, j, 0)),
            pl.BlockSpec((1, N_MOD, d), lambda i, j: (jnp.where(j < n_ctx_tiles, ctx_cond, i), 0, 0)),
            pl.BlockSpec((1, d), lambda i, j: (0, 0)),
            pl.BlockSpec((d, n), lambda i, j: (0, 0)),
        ],
        out_specs=pl.BlockSpec((1, tm, n), lambda i, j: (i, j, 0)),
        out_shape=jax.ShapeDtypeStruct((b, s, n), F32),
        compiler_params=_cparams(2),
        name="in_projection",
    )(xc, mods, gain.reshape(1, d), w_bf16)


def _rope(x, cos, sin_signed):
    n = x.shape[1]
    lane = lax.broadcasted_iota(jnp.int32, x.shape, 1)
    fwd = pltpu.roll(x, n - AXIS_DIM // 2, 1)
    bwd = pltpu.roll(x, AXIS_DIM // 2, 1)
    swapped = jnp.where((lane % AXIS_DIM) < AXIS_DIM // 2, fwd, bwd)
    return x * cos + swapped * sin_signed


def _attn_a_kernel(sink_ref, q_ref, k_ref, v_ref, cq_ref, sq_ref, ck_ref, sk_ref, o_ref, kh_ref, vh_ref,
                   *, n_ctx, n_lat, ctx_steps):
    t = pl.program_id(1)
    span = WIN_BLOCK + 2 * WINDOW
    g = A_HEADS // A_KV_HEADS

    @pl.when(t == 0)
    def _():
        kr = _rope(k_ref[0], ck_ref[...], sk_ref[...])
        v = v_ref[0]
        for h in range(A_KV_HEADS):
            kh_ref[h] = kr[:, h * HEAD_DIM:(h + 1) * HEAD_DIM].astype(BF16)
            vh_ref[h] = v[:, h * HEAD_DIM:(h + 1) * HEAD_DIM].astype(BF16)

    q = _rope(q_ref[0], cq_ref[...], sq_ref[...]) * (HEAD_DIM ** -0.5)
    row = lax.broadcasted_iota(jnp.int32, (g * WIN_BLOCK, 1), 0)

    def q_pair(hk):
        parts = [q[:, (hk * g + j) * HEAD_DIM:(hk * g + j + 1) * HEAD_DIM] for j in range(g)]
        return jnp.concatenate(parts, axis=0).astype(BF16)

    def sink_col(hk):
        col = jnp.full((g * WIN_BLOCK, 1), sink_ref[hk * g], F32)
        for j in range(1, g):
            col = jnp.where(row >= j * WIN_BLOCK, sink_ref[hk * g + j], col)
        return col

    def assemble(outs):
        pieces = []
        for hk in range(A_KV_HEADS):
            for j in range(g):
                pieces.append(outs[hk][j * WIN_BLOCK:(j + 1) * WIN_BLOCK])
        return jnp.concatenate(pieces, axis=1)

    def latent_block():
        p0 = (t - ctx_steps) * WIN_BLOCK
        ws = jnp.clip(p0 - WINDOW, 0, n_lat - span)
        kstart = pl.multiple_of(n_ctx + ws, WIN_BLOCK)
        qpos = p0 + row % WIN_BLOCK
        kpos = ws + lax.broadcasted_iota(jnp.int32, (1, span), 1)
        mask = jnp.abs(qpos - kpos) <= WINDOW
        outs = []
        for hk in range(A_KV_HEADS):
            q2 = q_pair(hk)
            s_loc = _dot_nt(q2, kh_ref[hk, pl.ds(kstart, span), :])
            s_loc = jnp.where(mask, s_loc, MASK_VALUE)
            s_ctx = _dot_nt(q2, kh_ref[hk, 0:n_ctx, :])
            sk = sink_col(hk)
            m = jnp.maximum(jnp.maximum(jnp.max(s_loc, axis=1, keepdims=True),
                                        jnp.max(s_ctx, axis=1, keepdims=True)), sk)
            p_loc = jnp.exp(s_loc - m)
            p_ctx = jnp.exp(s_ctx - m)
            den = (jnp.sum(p_loc, axis=1, keepdims=True) + jnp.sum(p_ctx, axis=1, keepdims=True)
                   + jnp.exp(sk - m))
            o = (_dot(p_loc.astype(BF16), vh_ref[hk, pl.ds(kstart, span), :])
                 + _dot(p_ctx.astype(BF16), vh_ref[hk, 0:n_ctx, :]))
            outs.append(o / den)
        o_ref[0] = assemble(outs)

    def ctx_block():
        outs = []
        for hk in range(A_KV_HEADS):
            q2 = q_pair(hk)
            s_ctx = _dot_nt(q2, kh_ref[hk, 0:n_ctx, :])
            sk = sink_col(hk)
            m = jnp.maximum(jnp.max(s_ctx, axis=1, keepdims=True), sk)
            p_ctx = jnp.exp(s_ctx - m)
            den = jnp.sum(p_ctx, axis=1, keepdims=True) + jnp.exp(sk - m)
            outs.append(_dot(p_ctx.astype(BF16), vh_ref[hk, 0:n_ctx, :]) / den)
        o_ref[0] = assemble(outs)

    if ctx_steps:
        pl.when(t < ctx_steps)(ctx_block)
        pl.when(t >= ctx_steps)(latent_block)
    else:
        latent_block()


def _attention_a(proj, cos_t, sin_t, sink, n_ctx, with_ctx):
    b, s, _ = proj.shape
    n_lat = s - n_ctx
    ctx_blocks = n_ctx // WIN_BLOCK
    ctx_steps = ctx_blocks if with_ctx else 0
    off = ctx_blocks - ctx_steps
    steps = ctx_steps + n_lat // WIN_BLOCK
    kw = A_KV_HEADS * HEAD_DIM
    kernel = functools.partial(_attn_a_kernel, n_ctx=n_ctx, n_lat=n_lat, ctx_steps=ctx_steps)
    return pl.pallas_call(
        kernel,
        grid=(b, steps),
        in_specs=[
            pl.BlockSpec(memory_space=pltpu.SMEM),
            pl.BlockSpec((1, WIN_BLOCK, GROUP_W), lambda i, t: (i, t + off, 0)),
            pl.BlockSpec((1, s, kw), lambda i, t: (i, 0, 2)),
            pl.BlockSpec((1, s, kw), lambda i, t: (i, 0, 3)),
            pl.BlockSpec((WIN_BLOCK, GROUP_W), lambda i, t: (t + off, 0)),
            pl.BlockSpec((WIN_BLOCK, GROUP_W), lambda i, t: (t + off, 0)),
            pl.BlockSpec((s, kw), lambda i, t: (0, 0)),
            pl.BlockSpec((s, kw), lambda i, t: (0, 0)),
        ],
        out_specs=pl.BlockSpec((1, WIN_BLOCK, GROUP_W), lambda i, t: (i, t, 0)),
        out_shape=jax.ShapeDtypeStruct((b, steps * WIN_BLOCK, GROUP_W), F32),
        scratch_shapes=[pltpu.VMEM((A_KV_HEADS, s, HEAD_DIM), BF16), pltpu.VMEM((A_KV_HEADS, s, HEAD_DIM), BF16)],
        compiler_params=_cparams(2),
        name="window_attention",
    )(sink, proj, proj, proj, cos_t, sin_t, cos_t, sin_t)


def _attn_b_kernel(q_ref, k_ref, v_ref, bias_ref, o_ref, kh_ref, vh_ref, *, n_ctx, n_lat, ctx_steps):
    t = pl.program_id(1)
    rows = n_lat // GRID_W
    nkeys = NB_ROWS * GRID_W

    @pl.when(t == 0)
    def _():
        k = k_ref[0]
        v = v_ref[0]
        for h in range(B_HEADS):
            kh_ref[h] = k[:, h * HEAD_DIM:(h + 1) * HEAD_DIM].astype(BF16)
            vh_ref[h] = v[:, h * HEAD_DIM:(h + 1) * HEAD_DIM].astype(BF16)

    q = q_ref[0] * (HEAD_DIM ** -0.5)

    def latent_block():
        r = t - ctx_steps
        r0 = jnp.clip(r - NB_ROWS // 2, 0, rows - NB_ROWS)
        kstart = pl.multiple_of(n_ctx + r0 * GRID_W, GRID_W)
        outs = []
        for h in range(B_HEADS):
            qh = q[:, h * HEAD_DIM:(h + 1) * HEAD_DIM].astype(BF16)
            s_loc = _dot_nt(qh, kh_ref[h, pl.ds(kstart, nkeys), :]) + bias_ref[0, h]
            s_ctx = _dot_nt(qh, kh_ref[h, 0:n_ctx, :])
            m = jnp.maximum(jnp.max(s_loc, axis=1, keepdims=True), jnp.max(s_ctx, axis=1, keepdims=True))
            p_loc = jnp.exp(s_loc - m)
            p_ctx = jnp.exp(s_ctx - m)
            den = jnp.sum(p_loc, axis=1, keepdims=True) + jnp.sum(p_ctx, axis=1, keepdims=True)
            o = (_dot(p_loc.astype(BF16), vh_ref[h, pl.ds(kstart, nkeys), :])
                 + _dot(p_ctx.astype(BF16), vh_ref[h, 0:n_ctx, :]))
            outs.append(o / den)
        o_ref[0] = jnp.concatenate(outs, axis=1)

    def ctx_block():
        outs = []
        for h in range(B_HEADS):
            qh = q[:, h * HEAD_DIM:(h + 1) * HEAD_DIM].astype(BF16)
            s_ctx = _dot_nt(qh, kh_ref[h, 0:n_ctx, :])
            m = jnp.max(s_ctx, axis=1, keepdims=True)
            p_ctx = jnp.exp(s_ctx - m)
            den = jnp.sum(p_ctx, axis=1, keepdims=True)
            outs.append(_dot(p_ctx.astype(BF16), vh_ref[h, 0:n_ctx, :]) / den)
        o_ref[0] = jnp.concatenate(outs, axis=1)

    if ctx_steps:
        pl.when(t < ctx_steps)(ctx_block)
        pl.when(t >= ctx_steps)(latent_block)
    else:
        latent_block()


def _attention_b(proj, bias, n_ctx, with_ctx):
    b, s, _ = proj.shape
    n_lat = s - n_ctx
    rows = n_lat // GRID_W
    ctx_blocks = n_ctx // GRID_W
    ctx_steps = ctx_blocks if with_ctx else 0
    off = ctx_blocks - ctx_steps
    steps = ctx_steps + rows

    def bias_index(i, t):
        r = jnp.maximum(t - ctx_steps, 0)
        r0 = jnp.clip(r - NB_ROWS // 2, 0, rows - NB_ROWS)
        return (r - r0, 0, 0, 0)

    kernel = functools.partial(_attn_b_kernel, n_ctx=n_ctx, n_lat=n_lat, ctx_steps=ctx_steps)
    return pl.pallas_call(
        kernel,
        grid=(b, steps),
        in_specs=[
            pl.BlockSpec((1, GRID_W, GROUP_W), lambda i, t: (i, t + off, 2)),
            pl.BlockSpec((1, s, GROUP_W), lambda i, t: (i, 0, 3)),
            pl.BlockSpec((1, s, GROUP_W), lambda i, t: (i, 0, 4)),
            pl.BlockSpec((1, B_HEADS, GRID_W, NB_ROWS * GRID_W), bias_index),
        ],
        out_specs=pl.BlockSpec((1, GRID_W, GROUP_W), lambda i, t: (i, t, 0)),
        out_shape=jax.ShapeDtypeStruct((b, steps * GRID_W, GROUP_W), F32),
        scratch_shapes=[pltpu.VMEM((B_HEADS, s, HEAD_DIM), BF16), pltpu.VMEM((B_HEADS, s, HEAD_DIM), BF16)],
        compiler_params=_cparams(2),
        name="neighbourhood_attention",
    )(proj, proj, proj, bias)


def _neighbourhood_bias(rpb):
    qc = np.arange(GRID_W)[:, None]
    kc = np.arange(GRID_W)[None, :]
    win0 = np.clip(qc - NB_COLS // 2, 0, GRID_W - NB_COLS)
    ok = (kc >= win0) & (kc < win0 + NB_COLS)
    d_col = np.clip(kc - qc, -(NB_COLS - 1), NB_COLS - 1) + (NB_COLS - 1)
    onehot = (d_col[None] == np.arange(2 * NB_COLS - 1)[:, None, None]).astype(np.float32)
    by_rel = jnp.stack([rpb[:, NB_ROWS - 1 - rel:2 * NB_ROWS - 1 - rel, :] for rel in range(NB_ROWS)], axis=0)
    tab = jnp.einsum('rhkc,cqj->rhqkj', by_rel.astype(F32), jnp.asarray(onehot), precision=lax.Precision.HIGHEST)
    tab = jnp.where(ok[None, None, :, None, :], tab, MASK_VALUE)
    return tab.reshape(NB_ROWS, B_HEADS, GRID_W, NB_ROWS * GRID_W)


def _conv_kernel(cx_ref, cb_ref, cc_ref, w_ref, o_ref, pad_ref, *, n_ctx, seq):
    edge = 8
    chunk = ROW_TILE
    pad_ref[0:edge, :] = jnp.zeros((edge, GROUP_W), F32)
    pad_ref[edge + seq:2 * edge + seq, :] = jnp.zeros((edge, GROUP_W), F32)
    for c in range(seq // chunk):
        sl = slice(c * chunk, (c + 1) * chunk)
        pad_ref[edge + c * chunk:edge + (c + 1) * chunk, :] = cc_ref[0, sl, :] * cx_ref[0, sl, :]
    w = w_ref[...]
    for c in range(seq // chunk):
        lo = edge + c * chunk
        row = c * chunk + lax.broadcasted_iota(jnp.int32, (chunk, 1), 0)
        prev = pad_ref[lo - 1:lo - 1 + chunk, :]
        cur = pad_ref[lo:lo + chunk, :]
        nxt = pad_ref[lo + 1:lo + 1 + chunk, :]
        prev = jnp.where((row == 0) | (row == n_ctx), 0.0, prev)
        nxt = jnp.where((row == n_ctx - 1) | (row == seq - 1), 0.0, nxt)
        conv = w[0:1] * prev + w[1:2] * cur + w[2:3] * nxt
        o_ref[0, c * chunk:(c + 1) * chunk, :] = cb_ref[0, c * chunk:(c + 1) * chunk, :] * conv


def _short_conv(proj, conv_w, n_ctx):
    b, s, _ = proj.shape
    kernel = functools.partial(_conv_kernel, n_ctx=n_ctx, seq=s)
    return pl.pallas_call(
        kernel,
        grid=(b,),
        in_specs=[
            pl.BlockSpec((1, s, GROUP_W), lambda i: (i, 0, 5)),
            pl.BlockSpec((1, s, GROUP_W), lambda i: (i, 0, 6)),
            pl.BlockSpec((1, s, GROUP_W), lambda i: (i, 0, 7)),
            pl.BlockSpec((CONV_W, GROUP_W), lambda i: (0, 0)),
        ],
        out_specs=pl.BlockSpec((1, s, GROUP_W), lambda i: (i, 0, 0)),
        out_shape=jax.ShapeDtypeStruct((b, s, GROUP_W), F32),
        scratch_shapes=[pltpu.VMEM((s + 16, GROUP_W), F32)],
        compiler_params=_cparams(1),
        name="gated_short_conv",
    )(proj, proj, proj, conv_w)


OFFSET_GROUP = 16


def _hgrn_chunk(d, base, q_ref, z_ref, i_ref, gate_consts, o_ref, st_ref, kst, fst, vst, sst, pbuf, ones_bd, bd_mask):
    ch = SCAN_CHUNK
    sgn = 1 if d == 0 else -1
    lb_floor, one_minus_lb, floor_excess = gate_consts
    rows = pl.ds(base, ch)
    z = z_ref[0, rows, :]
    q = q_ref[0, rows, :]
    v = i_ref[0, rows, :]
    sig = 1.0 / (1.0 + jnp.exp(-z))
    f = lb_floor + one_minus_lb * sig
    k = one_minus_lb * (1.0 - sig) - floor_excess
    kst[d, ch:2 * ch, :] = k
    fst[d, ch:2 * ch, :] = f
    vst[d, ch:2 * ch, :] = v

    def shifted(ref, o):
        return ref[d, ch - sgn * o:2 * ch - sgn * o, :]

    inc = f
    exc = f
    step = 1
    while step < ch:
        sst[d, ch:2 * ch, :] = inc
        inc = inc * shifted(sst, step)
        step *= 2
    step = 1
    while step < ch:
        sst[d, ch:2 * ch, :] = exc
        exc = exc * shifted(sst, -step)
        step *= 2
    sst[d, ch:2 * ch, :] = exc
    exc = shifted(sst, -1)
    total = inc[ch - 1:ch, :] if d == 0 else inc[0:1, :]

    decay = jnp.ones((ch, GROUP_W), F32)
    acc = jnp.zeros((ch, GROUP_W), F32)
    for g0 in range(0, ch, OFFSET_GROUP):
        for oo in range(OFFSET_GROUP):
            o = g0 + oo
            if o > 0:
                decay = decay * shifted(fst, o - 1)
            pbuf[d, oo * ch:(oo + 1) * ch, :] = (q * shifted(kst, o) * decay).astype(BF16)
        head_sums = _dot(pbuf[d], ones_bd)
        for oo in range(OFFSET_GROUP):
            acc = acc + head_sums[oo * ch:(oo + 1) * ch, :] * shifted(vst, g0 + oo)

    st = st_ref[d]
    o_inter = _dot_nt((q * inc).astype(BF16), st.astype(BF16))
    upd = _dot_tn(v.astype(BF16), (k * exc).astype(BF16))
    st_ref[d] = st * total + upd * bd_mask
    o_ref[0, rows, :] = o_ref[0, rows, :] + acc + o_inter


def _hgrn_kernel(q_ref, zf_ref, zb_ref, i_ref, lb_ref, o_ref, st_ref, kst, fst, vst, sst, pbuf, *, n_ctx, seq):
    ch = SCAN_CHUNK
    n_chunks = seq // ch
    ctx_chunks = n_ctx // ch
    lb = lb_ref[...]
    lb_floor = jnp.maximum(lb, LB_FLOOR)
    gate_consts = (lb_floor, 1.0 - lb, lb_floor - lb)
    hi = lax.broadcasted_iota(jnp.int32, (GROUP_W, GROUP_W), 0) // HEAD_DIM
    hj = lax.broadcasted_iota(jnp.int32, (GROUP_W, GROUP_W), 1) // HEAD_DIM
    bd_mask = jnp.where(hi == hj, 1.0, 0.0).astype(F32)
    ones_bd = bd_mask.astype(BF16)

    o_ref[...] = jnp.zeros(o_ref.shape, F32)
    st_ref[...] = jnp.zeros(st_ref.shape, F32)
    kst[...] = jnp.zeros(kst.shape, F32)
    fst[...] = jnp.zeros(fst.shape, F32)
    vst[...] = jnp.zeros(vst.shape, F32)
    sst[...] = jnp.ones(sst.shape, F32)

    def body(s, carry):
        cf = s
        cb = jnp.where(s < ctx_chunks, ctx_chunks - 1 - s, n_chunks - 1 - (s - ctx_chunks))
        for d, c, z_ref in ((0, cf, zf_ref), (1, cb, zb_ref)):
            base = pl.multiple_of(c * ch, ch)
            _hgrn_chunk(d, base, q_ref, z_ref, i_ref, gate_consts, o_ref, st_ref, kst, fst, vst, sst, pbuf,
                        ones_bd, bd_mask)
        return carry

    lax.fori_loop(0, n_chunks, body, 0)


def _hgrn(proj, lower_bound, n_ctx):
    b, s, _ = proj.shape
    ch = SCAN_CHUNK
    kernel = functools.partial(_hgrn_kernel, n_ctx=n_ctx, seq=s)
    col = lambda c: pl.BlockSpec((1, s, GROUP_W), lambda i: (i, 0, c))
    return pl.pallas_call(
        kernel,
        grid=(b,),
        in_specs=[col(8), col(9), col(10), col(11), pl.BlockSpec((1, GROUP_W), lambda i: (0, 0))],
        out_specs=pl.BlockSpec((1, s, GROUP_W), lambda i: (i, 0, 0)),
        out_shape=jax.ShapeDtypeStruct((b, s, GROUP_W), F32),
        scratch_shapes=[
            pltpu.VMEM((2, GROUP_W, GROUP_W), F32),
            pltpu.VMEM((2, 3 * ch, GROUP_W), F32),
            pltpu.VMEM((2, 3 * ch, GROUP_W), F32),
            pltpu.VMEM((2, 3 * ch, GROUP_W), F32),
            pltpu.VMEM((2, 3 * ch, GROUP_W), F32),
            pltpu.VMEM((2, OFFSET_GROUP * ch, GROUP_W), BF16),
        ],
        compiler_params=_cparams(1),
        name="hgrn2_bidirectional",
    )(proj, proj, proj, proj, lower_bound.reshape(1, GROUP_W))


def _post_kernel(ya_ref, yb_ref, yc_ref, od_ref, dg_ref, x_ref, mod_ref, gn_ref, wo_ref, nf_ref, wq_ref,
                 k1_ref, k2_ref, xo_ref, h_ref, sc_ref):
    yd = od_ref[0] * _silu(dg_ref[0])
    gn = gn_ref[...]
    parts = (ya_ref[0], yb_ref[0], yc_ref[0], yd)
    normed = [_rms(p) * gn[:, i * GROUP_W:(i + 1) * GROUP_W] for i, p in enumerate(parts)]
    y = _dot(jnp.concatenate(normed, axis=1).astype(BF16), wo_ref[...])
    xn = x_ref[0] + mod_ref[0, 2:3, :] * y
    xo_ref[0] = xn
    h = _rms(xn) * nf_ref[...] * (1.0 + mod_ref[0, 4:5, :]) + mod_ref[0, 3:4, :]
    h_ref[0] = h
    qv = _dot(h.astype(BF16), wq_ref[...])
    half = PEER_NKEYS
    for g in range(2 * PEER_HEADS):
        keys = k1_ref[...] if g % 2 == 0 else k2_ref[...]
        sc_ref[g] = _dot_nt(keys, qv[:, g * half:(g + 1) * half].astype(BF16))


def _post_mixer(ya, yb, yc, od, proj, xc, mods, gnorm, wo_bf16, nffn, wq_bf16, k1_bf16, k2_bf16, n_ctx_tiles, tile0):
    b, s, d = xc.shape
    tm = ROW_TILE
    nt = s // tm - tile0
    s_out = nt * tm
    ctx_cond = mods.shape[0] - 1
    nq = wq_bf16.shape[1]
    att0 = tile0 - (s - ya.shape[1]) // tm
    att = lambda: pl.BlockSpec((1, tm, GROUP_W), lambda i, j: (i, j + att0, 0))
    grp = lambda: pl.BlockSpec((1, tm, GROUP_W), lambda i, j: (i, j + tile0, 0))
    const = lambda shape: pl.BlockSpec(shape, lambda i, j: (0,) * len(shape))
    return pl.pallas_call(
        _post_kernel,
        grid=(b, nt),
        in_specs=[
            att(), att(), grp(), grp(),
            pl.BlockSpec((1, tm, GROUP_W), lambda i, j: (i, j + tile0, 12)),
            pl.BlockSpec((1, tm, d), lambda i, j: (i, j + tile0, 0)),
            pl.BlockSpec((1, N_MOD, d), lambda i, j: (jnp.where(j + tile0 < n_ctx_tiles, ctx_cond, i), 0, 0)),
            const((1, d)), const((d, d)), const((1, d)), const((d, nq)),
            const((PEER_NKEYS, PEER_NKEYS)), const((PEER_NKEYS, PEER_NKEYS)),
        ],
        out_specs=[
            pl.BlockSpec((1, tm, d), lambda i, j: (i, j, 0)),
            pl.BlockSpec((1, tm, d), lambda i, j: (i, j, 0)),
            pl.BlockSpec((2 * PEER_HEADS, PEER_NKEYS, tm), lambda i, j: (0, 0, i * nt + j)),
        ],
        out_shape=[
            jax.ShapeDtypeStruct((b, s_out, d), F32),
            jax.ShapeDtypeStruct((b, s_out, d), F32),
            jax.ShapeDtypeStruct((2 * PEER_HEADS, PEER_NKEYS, b * s_out), F32),
        ],
        compiler_params=_cparams(2),
        name="post_mixer",
    )(ya, yb, yc, od, proj, xc, mods, gnorm.reshape(1, d), wo_bf16, nffn.reshape(1, d), wq_bf16, k1_bf16, k2_bf16)


def _top16(s):
    n = s.shape[0]
    rows = lax.broadcasted_iota(jnp.int32, s.shape, 0).astype(F32)
    vals, ids = [], []
    cur = s
    for _ in range(PEER_TOPK):
        m = jnp.max(cur, axis=0, keepdims=True)
        am = jnp.min(jnp.where(cur == m, rows, float(n)), axis=0, keepdims=True)
        vals.append(m)
        ids.append(am)
        cur = jnp.where(rows == am, -jnp.inf, cur)
    return jnp.concatenate(vals, axis=0), jnp.concatenate(ids, axis=0)


def _pick(table, sel):
    out = jnp.zeros(sel.shape, table.dtype)
    for a in range(PEER_TOPK):
        out = out + jnp.where(sel == float(a), table[a:a + 1, :], 0.0)
    return out


def _route_kernel(sc_ref, idx_ref, gate_ref):
    for h in range(PEER_HEADS):
        v1, i1 = _top16(sc_ref[2 * h])
        v2, i2 = _top16(sc_ref[2 * h + 1])
        cand = jnp.concatenate([v1[a:a + 1, :] + v2 for a in range(PEER_TOPK)], axis=0)
        top_s, pos = _top16(cand)
        a_sel = jnp.floor(pos * (1.0 / PEER_TOPK))
        b_sel = pos - a_sel * PEER_TOPK
        idx_ref[h] = (_pick(i1, a_sel) * PEER_NKEYS + _pick(i2, b_sel)).astype(jnp.int32)
        e = jnp.exp(top_s - jnp.max(top_s, axis=0, keepdims=True))
        gate_ref[h] = e / jnp.sum(e, axis=0, keepdims=True)


def _routing(scores_t):
    g, nk, t = scores_t.shape
    tt = TOPK_TOKENS
    out_spec = pl.BlockSpec((PEER_HEADS, PEER_TOPK, tt), lambda i: (0, 0, i))
    return pl.pallas_call(
        _route_kernel,
        grid=(t // tt,),
        in_specs=[pl.BlockSpec((g, nk, tt), lambda i: (0, 0, i))],
        out_specs=[out_spec, out_spec],
        out_shape=[jax.ShapeDtypeStruct((PEER_HEADS, PEER_TOPK, t), jnp.int32),
                   jax.ShapeDtypeStruct((PEER_HEADS, PEER_TOPK, t), F32)],
        compiler_params=_cparams(1),
        name="peer_routing",
    )(scores_t)


def _gelu_tanh(x):
    return 0.5 * x * (1.0 + jnp.tanh(math.sqrt(2.0 / math.pi) * (x + 0.044715 * (x * x * x))))


FOLD = (8, 128)


def _sublane_pair_sum(x, y, k):
    sub = lax.broadcasted_iota(jnp.int32, FOLD, 0)
    keep = (sub % (2 * k)) < k
    return jnp.where(keep, x, pltpu.roll(y, k, 0)) + jnp.where(keep, pltpu.roll(x, FOLD[0] - k, 0), y)


def _sublane_sums(p):
    z = [_sublane_pair_sum(p[j], p[j + 4], 4) for j in range(4)]
    w = [_sublane_pair_sum(z[0], z[2], 2), _sublane_pair_sum(z[1], z[3], 2)]
    return _sublane_pair_sum(w[0], w[1], 1)


def _peer_kernel(idx_hbm, uv_hbm, h_ref, gate_ref, x_ref, mod_ref, fn_ref, o_ref, idx_smem, buf, act_rep,
                 row_sem, idx_sem, *, n_blocks, final):
    i = pl.program_id(0)
    tb = PEER_TOKENS
    per_tok = PEER_HEADS * PEER_TOPK
    npair = tb * per_tok
    sub = FOLD[0]

    def idx_copy(blk, slot):
        return pltpu.make_async_copy(idx_hbm.at[pl.ds(blk * npair, npair)],
                                     idx_smem.at[pl.ds(slot * npair, npair)], idx_sem.at[slot])

    def row_copy(slot, r, prio):
        e = idx_smem[slot * npair + r]
        pltpu.make_async_copy(uv_hbm.at[e], buf.at[slot, r], row_sem.at[slot]).start(priority=prio)

    def wait_rows(slot):
        pltpu.make_async_copy(buf.at[slot], buf.at[slot], row_sem.at[slot]).wait()

    @pl.when(i == 0)
    def _():
        first = idx_copy(0, 0)
        first.start()
        first.wait()

        def body(r, carry):
            row_copy(0, r, 0)
            return carry
        lax.fori_loop(0, npair, body, 0, unroll=8)
        if n_blocks > 1:
            idx_copy(1, 1).start()

    @pl.when(i + 1 < n_blocks)
    def _():
        idx_copy(i + 1, (i + 1) % 2).wait()

    @pl.when(i + 2 < n_blocks)
    def _():
        idx_copy(i + 2, i % 2).start()

    slot = i % 2
    nslot = 1 - slot
    wait_rows(slot)
    eye = (lax.broadcasted_iota(jnp.int32, (per_tok, per_tok), 0)
           == lax.broadcasted_iota(jnp.int32, (per_tok, per_tok), 1))
    g2 = mod_ref[0, 5]
    for t in range(tb):
        base = t * per_tok
        hf = h_ref[t]
        parts = []
        for g in range(per_tok // sub):
            prods = []
            for j in range(sub):
                e = g * sub + j
                if e % 2 == 0:
                    row_copy(nslot, base + e // 2, (e // 2) % 2)
                prods.append(buf[slot, base + e, 0:sub, :] * hf)
            parts.append(_sublane_sums(prods))
        s = jnp.sum(jnp.concatenate(parts, axis=0), axis=1, keepdims=True)
        gate_col = jnp.sum(jnp.where(eye, gate_ref[t:t + 1, :], 0.0), axis=1, keepdims=True)
        act_rep[...] = jnp.broadcast_to(_gelu_tanh(s) * gate_col, (per_tok, FOLD[1]))
        accs = [jnp.zeros(FOLD, F32) for _ in range(4)]
        for e in range(per_tok):
            if e % 2 == 0:
                row_copy(nslot, base + per_tok // 2 + e // 2, (e // 2) % 2)
            a = jnp.broadcast_to(act_rep[e:e + 1, :], FOLD)
            accs[e % 4] = accs[e % 4] + a * buf[slot, base + e, sub:2 * sub, :]
        xn = x_ref[t] + g2 * ((accs[0] + accs[1]) + (accs[2] + accs[3]))
        if final:
            ms = jnp.sum(jnp.sum(xn * xn, axis=1, keepdims=True), axis=0, keepdims=True) * (1.0 / (FOLD[0] * FOLD[1]))
            xn = xn * lax.rsqrt(ms + EPS) * fn_ref[...]
        o_ref[t] = xn

    @pl.when(i == n_blocks - 1)
    def _():
        wait_rows(nslot)


def _peer_experts(h2, idx, gate, uv, x_mid, mods, final_gain, s_out, ctx_rows, final):
    t = h2.shape[0]
    tb = PEER_TOKENS
    per_tok = PEER_HEADS * PEER_TOPK
    n_blocks = t // tb
    assert n_blocks >= 2
    ctx_cond = mods.shape[0] - 1

    def mod_index(i):
        tok = i * tb
        return (jnp.where(tok % s_out < ctx_rows, ctx_cond, tok // s_out), 0, 0, 0)

    kernel = functools.partial(_peer_kernel, n_blocks=n_blocks, final=final)
    fold_spec = pl.BlockSpec((tb,) + FOLD, lambda i: (i, 0, 0))
    return pl.pallas_call(
        kernel,
        grid=(n_blocks,),
        in_specs=[
            pl.BlockSpec(memory_space=pl.ANY),
            pl.BlockSpec(memory_space=pl.ANY),
            fold_spec,
            pl.BlockSpec((tb, per_tok), lambda i: (i, 0)),
            fold_spec,
            pl.BlockSpec((1, N_MOD) + FOLD, mod_index),
            pl.BlockSpec(FOLD, lambda i: (0, 0)),
        ],
        out_specs=fold_spec,
        out_shape=jax.ShapeDtypeStruct((t,) + FOLD, F32),
        scratch_shapes=[
            pltpu.SMEM((2 * tb * per_tok,), jnp.int32),
            pltpu.VMEM((2, tb * per_tok, 2 * FOLD[0], FOLD[1]), F32),
            pltpu.VMEM((per_tok, FOLD[1]), F32),
            pltpu.SemaphoreType.DMA((2,)),
            pltpu.SemaphoreType.DMA((2,)),
        ],
        compiler_params=_cparams(1),
        name="peer_experts",
    )(idx, uv, h2, gate, x_mid, mods.reshape(mods.shape[0], N_MOD, *FOLD), final_gain.reshape(FOLD))


def _rope_tables(n_ctx, n_lat):
    t = jnp.arange(n_lat)
    inv = ROPE_THETA ** (-jnp.arange(0, AXIS_DIM, 2, dtype=F32) / AXIS_DIM)
    row = (t // GRID_W).astype(F32)[:, None] * inv
    col = (t % GRID_W).astype(F32)[:, None] * inv
    cos = jnp.concatenate([jnp.cos(row), jnp.cos(row), jnp.cos(col), jnp.cos(col)], axis=1)
    sin = jnp.concatenate([-jnp.sin(row), jnp.sin(row), -jnp.sin(col), jnp.sin(col)], axis=1)
    cos = jnp.concatenate([jnp.ones((n_ctx, HEAD_DIM), F32), cos], axis=0)
    sin = jnp.concatenate([jnp.zeros((n_ctx, HEAD_DIM), F32), sin], axis=0)
    return jnp.tile(cos, (1, A_HEADS)), jnp.tile(sin, (1, A_HEADS))


def kernel(x, c, ctx, c_ctx, w_mod, b_mod, norm_mix, norm_ffn, w_in, conv_w, attn_sink, na_rpb, lb_logits, group_norm, w_out, peer_wq, peer_k1, peer_k2, peer_u, peer_v, final_norm):
    b, n_lat, d = x.shape
    n_ctx = ctx.shape[1]
    depth = w_mod.shape[0]
    seq = n_ctx + n_lat
    assert n_ctx % ROW_TILE == 0 and n_lat % ROW_TILE == 0 and n_lat // GRID_W >= NB_ROWS
    assert n_lat >= WIN_BLOCK + 2 * WINDOW and w_in.shape[2] == IN_W
    n_ctx_tiles = n_ctx // ROW_TILE

    n_cond = b + 1
    pad = (-n_cond) % 8
    cvec = jnp.concatenate([c, c_ctx[None, :], jnp.zeros((pad, d), F32)], axis=0)
    mods = _modulation(cvec, w_mod, b_mod)[:, :n_cond].reshape(depth, n_cond, N_MOD, d)

    lb_soft = jax.nn.softmax(lb_logits.astype(F32), axis=0)
    lower_bounds = jnp.cumsum(lb_soft, axis=0) - lb_soft[0:1]
    cos_t, sin_t = _rope_tables(n_ctx, n_lat)

    xc = jnp.concatenate([ctx, x], axis=1)
    out = None
    for l in range(depth):
        last = l == depth - 1
        proj = _in_projection(xc, mods[l], norm_mix[l], w_in[l].astype(BF16), n_ctx_tiles)
        ya = _attention_a(proj, cos_t, sin_t, attn_sink[l], n_ctx, not last)
        yb = _attention_b(proj, _neighbourhood_bias(na_rpb[l]), n_ctx, not last)
        yc = _short_conv(proj, conv_w[l], n_ctx)
        od = _hgrn(proj, lower_bounds[l], n_ctx)
        tile0 = n_ctx_tiles if last else 0
        x_mid, h2, scores_t = _post_mixer(ya, yb, yc, od, proj, xc, mods[l], group_norm[l], w_out[l].astype(BF16),
                                          norm_ffn[l], peer_wq[l].astype(BF16), peer_k1[l].astype(BF16),
                                          peer_k2[l].astype(BF16), n_ctx_tiles, tile0)
        s_out = x_mid.shape[1]
        tokens = b * s_out
        idx_t, gate_t = _routing(scores_t)
        per_tok = PEER_HEADS * PEER_TOPK
        idx = idx_t.reshape(per_tok, tokens).T.reshape(tokens * per_tok)
        gate = gate_t.reshape(per_tok, tokens).T
        n_exp = peer_u.shape[1]
        uv = jnp.concatenate([peer_u[l].reshape(n_exp, *FOLD), peer_v[l].reshape(n_exp, *FOLD)], axis=1)
        res = _peer_experts(h2.reshape(tokens, *FOLD), idx, gate, uv, x_mid.reshape(tokens, *FOLD), mods[l],
                            final_norm, s_out, 0 if last else n_ctx, last)
        if last:
            out = res.reshape(b, s_out, d)
        else:
            xc = res.reshape(b, s_out, d)
    return out
```

```python
import functools
import math

import jax
import jax.numpy as jnp
import numpy as np
from jax import lax
from jax.experimental import pallas as pl
from jax.experimental.pallas import tpu as pltpu

F32 = jnp.float32
BF16 = jnp.bfloat16

GRID_W = 64
HEAD_DIM = 64
GROUP_W = 256
A_HEADS = 4
A_KV_HEADS = 2
WINDOW = 128
WIN_BLOCK = 128
B_HEADS = 4
NB_ROWS = 8
NB_COLS = 16
CONV_W = 3
D_HEADS = 4
SCAN_CHUNK = 64
LB_FLOOR = 1e-20
ROPE_THETA = 10000.0
AXIS_DIM = HEAD_DIM // 2
PEER_HEADS = 8
PEER_NKEYS = 128
PEER_TOPK = 16
N_MOD = 6
EPS = 1e-6
MASK_VALUE = -1e30
IN_W = 13 * GROUP_W

ROW_TILE = 256
MOD_COL_TILE = 1536
TOPK_TOKENS = 128
PEER_TOKENS = 8
VMEM_LIMIT = 48 * 1024 * 1024


def _cparams(n_axes):
    return pltpu.CompilerParams(dimension_semantics=("arbitrary",) * n_axes, vmem_limit_bytes=VMEM_LIMIT)


def _dot(a, b):
    return jnp.dot(a, b, preferred_element_type=F32)


def _dot_nt(a, b):
    return lax.dot_general(a, b, (((1,), (1,)), ((), ())), preferred_element_type=F32)


def _dot_tn(a, b):
    return lax.dot_general(a, b, (((0,), (0,)), ((), ())), preferred_element_type=F32)


def _rms(x):
    return x * lax.rsqrt(jnp.mean(x * x, axis=-1, keepdims=True) + EPS)


def _silu(x):
    return x / (1.0 + jnp.exp(-x))


def _mod_kernel(c_ref, w_ref, b_ref, o_ref):
    s = _silu(c_ref[...])
    o_ref[0] = _dot(s.astype(BF16), w_ref[0].astype(BF16)) + b_ref[0]


def _modulation(cvec, w_mod, b_mod):
    depth, d, n = w_mod.shape
    r = cvec.shape[0]
    tn = MOD_COL_TILE
    return pl.pallas_call(
        _mod_kernel,
        grid=(depth, n // tn),
        in_specs=[
            pl.BlockSpec((r, d), lambda l, j: (0, 0)),
            pl.BlockSpec((1, d, tn), lambda l, j: (l, 0, j)),
            pl.BlockSpec((1, 1, tn), lambda l, j: (l, 0, j)),
        ],
        out_specs=pl.BlockSpec((1, r, tn), lambda l, j: (l, 0, j)),
        out_shape=jax.ShapeDtypeStruct((depth, r, n), F32),
        compiler_params=_cparams(2),
        name="adaln_modulation",
    )(cvec, w_mod, b_mod.reshape(depth, 1, n))


def _inproj_kernel(x_ref, mod_ref, g_ref, w_ref, o_ref):
    y = _rms(x_ref[0]) * g_ref[...]
    h = y * (1.0 + mod_ref[0, 1:2, :]) + mod_ref[0, 0:1, :]
    o_ref[0] = _dot(h.astype(BF16), w_ref[...])


def _in_projection(xc, mods, gain, w_bf16, n_ctx_tiles):
    b, s, d = xc.shape
    n = w_bf16.shape[1]
    tm = ROW_TILE
    ctx_cond = mods.shape[0] - 1
    return pl.pallas_call(
        _inproj_kernel,
        grid=(b, s // tm),
        in_specs=[
            pl.BlockSpec((1, tm, d), lambda i, j: (i, j, 0)),
            pl.BlockSpec((1, N_MOD, d), lambda i, j: (jnp.where(j < n_ctx_tiles, ctx_cond, i), 0, 0)),
            pl.BlockSpec((1, d), lambda i, j: (0, 0)),
            pl.BlockSpec((d, n), lambda i, j: (0, 0)),
        ],
        out_specs=pl.BlockSpec((1, tm, n), lambda i, j: (i, j, 0)),
        out_shape=jax.ShapeDtypeStruct((b, s, n), F32),
        compiler_params=_cparams(2),
        name="in_projection",
    )(xc, mods, gain.reshape(1, d), w_bf16)


def _rope(x, cos, sin_signed):
    n = x.shape[1]
    lane = lax.broadcasted_iota(jnp.int32, x.shape, 1)
    fwd = pltpu.roll(x, n - AXIS_DIM // 2, 1)
    bwd = pltpu.roll(x, AXIS_DIM // 2, 1)
    swapped = jnp.where((lane % AXIS_DIM) < AXIS_DIM // 2, fwd, bwd)
    return x * cos + swapped * sin_signed


def _attn_a_kernel(sink_ref, q_ref, k_ref, v_ref, cq_ref, sq_ref, ck_ref, sk_ref, o_ref, kh_ref, vh_ref,
                   *, n_ctx, n_lat, ctx_steps):
    t = pl.program_id(1)
    span = WIN_BLOCK + 2 * WINDOW
    g = A_HEADS // A_KV_HEADS

    @pl.when(t == 0)
    def _():
        kr = _rope(k_ref[0], ck_ref[...], sk_ref[...])
        v = v_ref[0]
        for h in range(A_KV_HEADS):
            kh_ref[h] = kr[:, h * HEAD_DIM:(h + 1) * HEAD_DIM].astype(BF16)
            vh_ref[h] = v[:, h * HEAD_DIM:(h + 1) * HEAD_DIM].astype(BF16)

    q = _rope(q_ref[0], cq_ref[...], sq_ref[...]) * (HEAD_DIM ** -0.5)
    row = lax.broadcasted_iota(jnp.int32, (g * WIN_BLOCK, 1), 0)

    def q_pair(hk):
        parts = [q[:, (hk * g + j) * HEAD_DIM:(hk * g + j + 1) * HEAD_DIM] for j in range(g)]
        return jnp.concatenate(parts, axis=0).astype(BF16)

    def sink_col(hk):
        col = jnp.full((g * WIN_BLOCK, 1), sink_ref[hk * g], F32)
        for j in range(1, g):
            col = jnp.where(row >= j * WIN_BLOCK, sink_ref[hk * g + j], col)
        return col

    def assemble(outs):
        pieces = []
        for hk in range(A_KV_HEADS):
            for j in range(g):
                pieces.append(outs[hk][j * WIN_BLOCK:(j + 1) * WIN_BLOCK])
        return jnp.concatenate(pieces, axis=1)

    def latent_block():
        p0 = (t - ctx_steps) * WIN_BLOCK
        ws = jnp.clip(p0 - WINDOW, 0, n_lat - span)
        kstart = pl.multiple_of(n_ctx + ws, WIN_BLOCK)
        qpos = p0 + row % WIN_BLOCK
        kpos = ws + lax.broadcasted_iota(jnp.int32, (1, span), 1)
        mask = jnp.abs(qpos - kpos) <= WINDOW
        outs = []
        for hk in range(A_KV_HEADS):
            q2 = q_pair(hk)
            s_loc = _dot_nt(q2, kh_ref[hk, pl.ds(kstart, span), :])
            s_loc = jnp.where(mask, s_loc, MASK_VALUE)
            s_ctx = _dot_nt(q2, kh_ref[hk, 0:n_ctx, :])
            sk = sink_col(hk)
            m = jnp.maximum(jnp.maximum(jnp.max(s_loc, axis=1, keepdims=True),
                                        jnp.max(s_ctx, axis=1, keepdims=True)), sk)
            p_loc = jnp.exp(s_loc - m)
            p_ctx = jnp.exp(s_ctx - m)
            den = (jnp.sum(p_loc, axis=1, keepdims=True) + jnp.sum(p_ctx, axis=1, keepdims=True)
                   + jnp.exp(sk - m))
            o = (_dot(p_loc.astype(BF16), vh_ref[hk, pl.ds(kstart, span), :])
                 + _dot(p_ctx.astype(BF16), vh_ref[hk, 0:n_ctx, :]))
            outs.append(o / den)
        o_ref[0] = assemble(outs)

    def ctx_block():
        outs = []
        for hk in range(A_KV_HEADS):
            q2 = q_pair(hk)
            s_ctx = _dot_nt(q2, kh_ref[hk, 0:n_ctx, :])
            sk = sink_col(hk)
            m = jnp.maximum(jnp.max(s_ctx, axis=1, keepdims=True), sk)
            p_ctx = jnp.exp(s_ctx - m)
            den = jnp.sum(p_ctx, axis=1, keepdims=True) + jnp.exp(sk - m)
            outs.append(_dot(p_ctx.astype(BF16), vh_ref[hk, 0:n_ctx, :]) / den)
        o_ref[0] = assemble(outs)

    if ctx_steps:
        pl.when(t < ctx_steps)(ctx_block)
        pl.when(t >= ctx_steps)(latent_block)
    else:
        latent_block()


def _attention_a(proj, cos_t, sin_t, sink, n_ctx, with_ctx):
    b, s, _ = proj.shape
    n_lat = s - n_ctx
    ctx_blocks = n_ctx // WIN_BLOCK
    ctx_steps = ctx_blocks if with_ctx else 0
    off = ctx_blocks - ctx_steps
    steps = ctx_steps + n_lat // WIN_BLOCK
    kw = A_KV_HEADS * HEAD_DIM
    kernel = functools.partial(_attn_a_kernel, n_ctx=n_ctx, n_lat=n_lat, ctx_steps=ctx_steps)
    return pl.pallas_call(
        kernel,
        grid=(b, steps),
        in_specs=[
            pl.BlockSpec(memory_space=pltpu.SMEM),
            pl.BlockSpec((1, WIN_BLOCK, GROUP_W), lambda i, t: (i, t + off, 0)),
            pl.BlockSpec((1, s, kw), lambda i, t: (i, 0, 2)),
            pl.BlockSpec((1, s, kw), lambda i, t: (i, 0, 3)),
            pl.BlockSpec((WIN_BLOCK, GROUP_W), lambda i, t: (t + off, 0)),
            pl.BlockSpec((WIN_BLOCK, GROUP_W), lambda i, t: (t + off, 0)),
            pl.BlockSpec((s, kw), lambda i, t: (0, 0)),
            pl.BlockSpec((s, kw), lambda i, t: (0, 0)),
        ],
        out_specs=pl.BlockSpec((1, WIN_BLOCK, GROUP_W), lambda i, t: (i, t, 0)),
        out_shape=jax.ShapeDtypeStruct((b, steps * WIN_BLOCK, GROUP_W), F32),
        scratch_shapes=[pltpu.VMEM((A_KV_HEADS, s, HEAD_DIM), BF16), pltpu.VMEM((A_KV_HEADS, s, HEAD_DIM), BF16)],
        compiler_params=_cparams(2),
        name="window_attention",
    )(sink, proj, proj, proj, cos_t, sin_t, cos_t, sin_t)


def _attn_b_kernel(q_ref, k_ref, v_ref, bias_ref, o_ref, kh_ref, vh_ref, *, n_ctx, n_lat, ctx_steps):
    t = pl.program_id(1)
    rows = n_lat // GRID_W
    nkeys = NB_ROWS * GRID_W

    @pl.when(t == 0)
    def _():
        k = k_ref[0]
        v = v_ref[0]
        for h in range(B_HEADS):
            kh_ref[h] = k[:, h * HEAD_DIM:(h + 1) * HEAD_DIM].astype(BF16)
            vh_ref[h] = v[:, h * HEAD_DIM:(h + 1) * HEAD_DIM].astype(BF16)

    q = q_ref[0] * (HEAD_DIM ** -0.5)

    def latent_block():
        r = t - ctx_steps
        r0 = jnp.clip(r - NB_ROWS // 2, 0, rows - NB_ROWS)
        kstart = pl.multiple_of(n_ctx + r0 * GRID_W, GRID_W)
        outs = []
        for h in range(B_HEADS):
            qh = q[:, h * HEAD_DIM:(h + 1) * HEAD_DIM].astype(BF16)
            s_loc = _dot_nt(qh, kh_ref[h, pl.ds(kstart, nkeys), :]) + bias_ref[0, h]
            s_ctx = _dot_nt(qh, kh_ref[h, 0:n_ctx, :])
            m = jnp.maximum(jnp.max(s_loc, axis=1, keepdims=True), jnp.max(s_ctx, axis=1, keepdims=True))
            p_loc = jnp.exp(s_loc - m)
            p_ctx = jnp.exp(s_ctx - m)
            den = jnp.sum(p_loc, axis=1, keepdims=True) + jnp.sum(p_ctx, axis=1, keepdims=True)
            o = (_dot(p_loc.astype(BF16), vh_ref[h, pl.ds(kstart, nkeys), :])
                 + _dot(p_ctx.astype(BF16), vh_ref[h, 0:n_ctx, :]))
            outs.append(o / den)
        o_ref[0] = jnp.concatenate(outs, axis=1)

    def ctx_block():
        outs = []
        for h in range(B_HEADS):
            qh = q[:, h * HEAD_DIM:(h + 1) * HEAD_DIM].astype(BF16)
            s_ctx = _dot_nt(qh, kh_ref[h, 0:n_ctx, :])
            m = jnp.max(s_ctx, axis=1, keepdims=True)
            p_ctx = jnp.exp(s_ctx - m)
            den = jnp.sum(p_ctx, axis=1, keepdims=True)
            outs.append(_dot(p_ctx.astype(BF16), vh_ref[h, 0:n_ctx, :]) / den)
        o_ref[0] = jnp.concatenate(outs, axis=1)

    if ctx_steps:
        pl.when(t < ctx_steps)(ctx_block)
        pl.when(t >= ctx_steps)(latent_block)
    else:
        latent_block()


def _attention_b(proj, bias, n_ctx, with_ctx):
    b, s, _ = proj.shape
    n_lat = s - n_ctx
    rows = n_lat // GRID_W
    ctx_blocks = n_ctx // GRID_W
    ctx_steps = ctx_blocks if with_ctx else 0
    off = ctx_blocks - ctx_steps
    steps = ctx_steps + rows

    def bias_index(i, t):
        r = jnp.maximum(t - ctx_steps, 0)
        r0 = jnp.clip(r - NB_ROWS // 2, 0, rows - NB_ROWS)
        return (r - r0, 0, 0, 0)

    kernel = functools.partial(_attn_b_kernel, n_ctx=n_ctx, n_lat=n_lat, ctx_steps=ctx_steps)
    return pl.pallas_call(
        kernel,
        grid=(b, steps),
        in_specs=[
            pl.BlockSpec((1, GRID_W, GROUP_W), lambda i, t: (i, t + off, 2)),
            pl.BlockSpec((1, s, GROUP_W), lambda i, t: (i, 0, 3)),
            pl.BlockSpec((1, s, GROUP_W), lambda i, t: (i, 0, 4)),
            pl.BlockSpec((1, B_HEADS, GRID_W, NB_ROWS * GRID_W), bias_index),
        ],
        out_specs=pl.BlockSpec((1, GRID_W, GROUP_W), lambda i, t: (i, t, 0)),
        out_shape=jax.ShapeDtypeStruct((b, steps * GRID_W, GROUP_W), F32),
        scratch_shapes=[pltpu.VMEM((B_HEADS, s, HEAD_DIM), BF16), pltpu.VMEM((B_HEADS, s, HEAD_DIM), BF16)],
        compiler_params=_cparams(2),
        name="neighbourhood_attention",
    )(proj, proj, proj, bias)


def _neighbourhood_bias(rpb):
    qc = np.arange(GRID_W)[:, None]
    kc = np.arange(GRID_W)[None, :]
    win0 = np.clip(qc - NB_COLS // 2, 0, GRID_W - NB_COLS)
    ok = (kc >= win0) & (kc < win0 + NB_COLS)
    d_col = np.clip(kc - qc, -(NB_COLS - 1), NB_COLS - 1) + (NB_COLS - 1)
    onehot = (d_col[None] == np.arange(2 * NB_COLS - 1)[:, None, None]).astype(np.float32)
    by_rel = jnp.stack([rpb[:, NB_ROWS - 1 - rel:2 * NB_ROWS - 1 - rel, :] for rel in range(NB_ROWS)], axis=0)
    tab = jnp.einsum('rhkc,cqj->rhqkj', by_rel.astype(F32), jnp.asarray(onehot), precision=lax.Precision.HIGHEST)
    tab = jnp.where(ok[None, None, :, None, :], tab, MASK_VALUE)
    return tab.reshape(NB_ROWS, B_HEADS, GRID_W, NB_ROWS * GRID_W)


def _conv_kernel(cx_ref, cb_ref, cc_ref, w_ref, o_ref, pad_ref, *, n_ctx, seq):
    edge = 8
    chunk = ROW_TILE
    pad_ref[0:edge, :] = jnp.zeros((edge, GROUP_W), F32)
    pad_ref[edge + seq:2 * edge + seq, :] = jnp.zeros((edge, GROUP_W), F32)
    for c in range(seq // chunk):
        sl = slice(c * chunk, (c + 1) * chunk)
        pad_ref[edge + c * chunk:edge + (c + 1) * chunk, :] = cc_ref[0, sl, :] * cx_ref[0, sl, :]
    w = w_ref[...]
    for c in range(seq // chunk):
        lo = edge + c * chunk
        row = c * chunk + lax.broadcasted_iota(jnp.int32, (chunk, 1), 0)
        prev = pad_ref[lo - 1:lo - 1 + chunk, :]
        cur = pad_ref[lo:lo + chunk, :]
        nxt = pad_ref[lo + 1:lo + 1 + chunk, :]
        prev = jnp.where((row == 0) | (row == n_ctx), 0.0, prev)
        nxt = jnp.where((row == n_ctx - 1) | (row == seq - 1), 0.0, nxt)
        conv = w[0:1] * prev + w[1:2] * cur + w[2:3] * nxt
        o_ref[0, c * chunk:(c + 1) * chunk, :] = cb_ref[0, c * chunk:(c + 1) * chunk, :] * conv


def _short_conv(proj, conv_w, n_ctx):
    b, s, _ = proj.shape
    kernel = functools.partial(_conv_kernel, n_ctx=n_ctx, seq=s)
    return pl.pallas_call(
        kernel,
        grid=(b,),
        in_specs=[
            pl.BlockSpec((1, s, GROUP_W), lambda i: (i, 0, 5)),
            pl.BlockSpec((1, s, GROUP_W), lambda i: (i, 0, 6)),
            pl.BlockSpec((1, s, GROUP_W), lambda i: (i, 0, 7)),
            pl.BlockSpec((CONV_W, GROUP_W), lambda i: (0, 0)),
        ],
        out_specs=pl.BlockSpec((1, s, GROUP_W), lambda i: (i, 0, 0)),
        out_shape=jax.ShapeDtypeStruct((b, s, GROUP_W), F32),
        scratch_shapes=[pltpu.VMEM((s + 16, GROUP_W), F32)],
        compiler_params=_cparams(1),
        name="gated_short_conv",
    )(proj, proj, proj, conv_w)


OFFSET_GROUP = 16


def _hgrn_chunk(d, base, q_ref, z_ref, i_ref, gate_consts, o_ref, st_ref, kst, fst, vst, sst, pbuf, ones_bd, bd_mask):
    ch = SCAN_CHUNK
    sgn = 1 if d == 0 else -1
    lb_floor, one_minus_lb, floor_excess = gate_consts
    rows = pl.ds(base, ch)
    z = z_ref[0, rows, :]
    q = q_ref[0, rows, :]
    v = i_ref[0, rows, :]
    sig = 1.0 / (1.0 + jnp.exp(-z))
    f = lb_floor + one_minus_lb * sig
    k = one_minus_lb * (1.0 - sig) - floor_excess
    kst[d, ch:2 * ch, :] = k
    fst[d, ch:2 * ch, :] = f
    vst[d, ch:2 * ch, :] = v

    def shifted(ref, o):
        return ref[d, ch - sgn * o:2 * ch - sgn * o, :]

    inc = f
    exc = f
    step = 1
    while step < ch:
        sst[d, ch:2 * ch, :] = inc
        inc = inc * shifted(sst, step)
        step *= 2
    step = 1
    while step < ch:
        sst[d, ch:2 * ch, :] = exc
        exc = exc * shifted(sst, -step)
        step *= 2
    sst[d, ch:2 * ch, :] = exc
    exc = shifted(sst, -1)
    total = inc[ch - 1:ch, :] if d == 0 else inc[0:1, :]

    decay = jnp.ones((ch, GROUP_W), F32)
    acc = jnp.zeros((ch, GROUP_W), F32)
    for g0 in range(0, ch, OFFSET_GROUP):
        for oo in range(OFFSET_GROUP):
            o = g0 + oo
            if o > 0:
                decay = decay * shifted(fst, o - 1)
            pbuf[d, oo * ch:(oo + 1) * ch, :] = (q * shifted(kst, o) * decay).astype(BF16)
        head_sums = _dot(pbuf[d], ones_bd)
        for oo in range(OFFSET_GROUP):
            acc = acc + head_sums[oo * ch:(oo + 1) * ch, :] * shifted(vst, g0 + oo)

    st = st_ref[d]
    o_inter = _dot_nt((q * inc).astype(BF16), st.astype(BF16))
    upd = _dot_tn(v.astype(BF16), (k * exc).astype(BF16))
    st_ref[d] = st * total + upd * bd_mask
    o_ref[0, rows, :] = o_ref[0, rows, :] + acc + o_inter


def _hgrn_kernel(q_ref, zf_ref, zb_ref, i_ref, lb_ref, o_ref, st_ref, kst, fst, vst, sst, pbuf, *, n_ctx, seq):
    ch = SCAN_CHUNK
    n_chunks = seq // ch
    ctx_chunks = n_ctx // ch
    lb = lb_ref[...]
    lb_floor = jnp.maximum(lb, LB_FLOOR)
    gate_consts = (lb_floor, 1.0 - lb, lb_floor - lb)
    hi = lax.broadcasted_iota(jnp.int32, (GROUP_W, GROUP_W), 0) // HEAD_DIM
    hj = lax.broadcasted_iota(jnp.int32, (GROUP_W, GROUP_W), 1) // HEAD_DIM
    bd_mask = jnp.where(hi == hj, 1.0, 0.0).astype(F32)
    ones_bd = bd_mask.astype(BF16)

    o_ref[...] = jnp.zeros(o_ref.shape, F32)
    st_ref[...] = jnp.zeros(st_ref.shape, F32)
    kst[...] = jnp.zeros(kst.shape, F32)
    fst[...] = jnp.zeros(fst.shape, F32)
    vst[...] = jnp.zeros(vst.shape, F32)
    sst[...] = jnp.ones(sst.shape, F32)

    def body(s, carry):
        cf = s
        cb = jnp.where(s < ctx_chunks, ctx_chunks - 1 - s, n_chunks - 1 - (s - ctx_chunks))
        for d, c, z_ref in ((0, cf, zf_ref), (1, cb, zb_ref)):
            base = pl.multiple_of(c * ch, ch)
            _hgrn_chunk(d, base, q_ref, z_ref, i_ref, gate_consts, o_ref, st_ref, kst, fst, vst, sst, pbuf,
                        ones_bd, bd_mask)
        return carry

    lax.fori_loop(0, n_chunks, body, 0)


def _hgrn(proj, lower_bound, n_ctx):
    b, s, _ = proj.shape
    ch = SCAN_CHUNK
    kernel = functools.partial(_hgrn_kernel, n_ctx=n_ctx, seq=s)
    col = lambda c: pl.BlockSpec((1, s, GROUP_W), lambda i: (i, 0, c))
    return pl.pallas_call(
        kernel,
        grid=(b,),
        in_specs=[col(8), col(9), col(10), col(11), pl.BlockSpec((1, GROUP_W), lambda i: (0, 0))],
        out_specs=pl.BlockSpec((1, s, GROUP_W), lambda i: (i, 0, 0)),
        out_shape=jax.ShapeDtypeStruct((b, s, GROUP_W), F32),
        scratch_shapes=[
            pltpu.VMEM((2, GROUP_W, GROUP_W), F32),
            pltpu.VMEM((2, 3 * ch, GROUP_W), F32),
            pltpu.VMEM((2, 3 * ch, GROUP_W), F32),
            pltpu.VMEM((2, 3 * ch, GROUP_W), F32),
            pltpu.VMEM((2, 3 * ch, GROUP_W), F32),
            pltpu.VMEM((2, OFFSET_GROUP * ch, GROUP_W), BF16),
        ],
        compiler_params=_cparams(1),
        name="hgrn2_bidirectional",
    )(proj, proj, proj, proj, lower_bound.reshape(1, GROUP_W))


def _post_kernel(ya_ref, yb_ref, yc_ref, od_ref, dg_ref, x_ref, mod_ref, gn_ref, wo_ref, nf_ref, wq_ref,
                 k1_ref, k2_ref, xo_ref, h_ref, sc_ref):
    yd = od_ref[0] * _silu(dg_ref[0])
    gn = gn_ref[...]
    parts = (ya_ref[0], yb_ref[0], yc_ref[0], yd)
    normed = [_rms(p) * gn[:, i * GROUP_W:(i + 1) * GROUP_W] for i, p in enumerate(parts)]
    y = _dot(jnp.concatenate(normed, axis=1).astype(BF16), wo_ref[...])
    xn = x_ref[0] + mod_ref[0, 2:3, :] * y
    xo_ref[0] = xn
    h = _rms(xn) * nf_ref[...] * (1.0 + mod_ref[0, 4:5, :]) + mod_ref[0, 3:4, :]
    h_ref[0] = h
    qv = _dot(h.astype(BF16), wq_ref[...])
    half = PEER_NKEYS
    for g in range(2 * PEER_HEADS):
        keys = k1_ref[...] if g % 2 == 0 else k2_ref[...]
        sc_ref[g] = _dot_nt(keys, qv[:, g * half:(g + 1) * half].astype(BF16))


def _post_mixer(ya, yb, yc, od, proj, xc, mods, gnorm, wo_bf16, nffn, wq_bf16, k1_bf16, k2_bf16, n_ctx_tiles, tile0):
    b, s, d = xc.shape
    tm = ROW_TILE
    nt = s // tm - tile0
    s_out = nt * tm
    ctx_cond = mods.shape[0] - 1
    nq = wq_bf16.shape[1]
    att0 = tile0 - (s - ya.shape[1]) // tm
    att = lambda: pl.BlockSpec((1, tm, GROUP_W), lambda i, j: (i, j + att0, 0))
    grp = lambda: pl.BlockSpec((1, tm, GROUP_W), lambda i, j: (i, j + tile0, 0))
    const = lambda shape: pl.BlockSpec(shape, lambda i, j: (0,) * len(shape))
    return pl.pallas_call(
        _post_kernel,
        grid=(b, nt),
        in_specs=[
            att(), att(), grp(), grp(),
            pl.BlockSpec((1, tm, GROUP_W), lambda i, j: (i, j + tile0, 12)),
            pl.BlockSpec((1, tm, d), lambda i, j: (i, j + tile0, 0)),
            pl.BlockSpec((1, N_MOD, d), lambda i, j: (jnp.where(j + tile0 < n_ctx_tiles, ctx_cond, i), 0, 0)),
            const((1, d)), const((d, d)), const((1, d)), const((d, nq)),
            const((PEER_NKEYS, PEER_NKEYS)), const((PEER_NKEYS, PEER_NKEYS)),
        ],
        out_specs=[
            pl.BlockSpec((1, tm, d), lambda i, j: (i, j, 0)),
            pl.BlockSpec((1, tm, d), lambda i, j: (i, j, 0)),
            pl.BlockSpec((2 * PEER_HEADS, PEER_NKEYS, tm), lambda i, j: (0, 0, i * nt + j)),
        ],
        out_shape=[
            jax.ShapeDtypeStruct((b, s_out, d), F32),
            jax.ShapeDtypeStruct((b, s_out, d), F32),
            jax.ShapeDtypeStruct((2 * PEER_HEADS, PEER_NKEYS, b * s_out), F32),
        ],
        compiler_params=_cparams(2),
        name="post_mixer",
    )(ya, yb, yc, od, proj, xc, mods, gnorm.reshape(1, d), wo_bf16, nffn.reshape(1, d), wq_bf16, k1_bf16, k2_bf16)


def _top16(s):
    n = s.shape[0]
    rows = lax.broadcasted_iota(jnp.int32, s.shape, 0).astype(F32)
    vals, ids = [], []
    cur = s
    for _ in range(PEER_TOPK):
        m = jnp.max(cur, axis=0, keepdims=True)
        am = jnp.min(jnp.where(cur == m, rows, float(n)), axis=0, keepdims=True)
        vals.append(m)
        ids.append(am)
        cur = jnp.where(rows == am, -jnp.inf, cur)
    return jnp.concatenate(vals, axis=0), jnp.concatenate(ids, axis=0)


def _pick(table, sel):
    out = jnp.zeros(sel.shape, table.dtype)
    for a in range(PEER_TOPK):
        out = out + jnp.where(sel == float(a), table[a:a + 1, :], 0.0)
    return out


def _route_kernel(sc_ref, idx_ref, gate_ref):
    for h in range(PEER_HEADS):
        v1, i1 = _top16(sc_ref[2 * h])
        v2, i2 = _top16(sc_ref[2 * h + 1])
        cand = jnp.concatenate([v1[a:a + 1, :] + v2 for a in range(PEER_TOPK)], axis=0)
        top_s, pos = _top16(cand)
        a_sel = jnp.floor(pos * (1.0 / PEER_TOPK))
        b_sel = pos - a_sel * PEER_TOPK
        idx_ref[h] = (_pick(i1, a_sel) * PEER_NKEYS + _pick(i2, b_sel)).astype(jnp.int32)
        e = jnp.exp(top_s - jnp.max(top_s, axis=0, keepdims=True))
        gate_ref[h] = e / jnp.sum(e, axis=0, keepdims=True)


def _routing(scores_t):
    g, nk, t = scores_t.shape
    tt = TOPK_TOKENS
    out_spec = pl.BlockSpec((PEER_HEADS, PEER_TOPK, tt), lambda i: (0, 0, i))
    return pl.pallas_call(
        _route_kernel,
        grid=(t // tt,),
        in_specs=[pl.BlockSpec((g, nk, tt), lambda i: (0, 0, i))],
        out_specs=[out_spec, out_spec],
        out_shape=[jax.ShapeDtypeStruct((PEER_HEADS, PEER_TOPK, t), jnp.int32),
                   jax.ShapeDtypeStruct((PEER_HEADS, PEER_TOPK, t), F32)],
        compiler_params=_cparams(1),
        name="peer_routing",
    )(scores_t)


def _gelu_tanh(x):
    return 0.5 * x * (1.0 + jnp.tanh(math.sqrt(2.0 / math.pi) * (x + 0.044715 * (x * x * x))))


FOLD = (8, 128)


def _sublane_pair_sum(x, y, k):
    sub = lax.broadcasted_iota(jnp.int32, FOLD, 0)
    keep = (sub % (2 * k)) < k
    return jnp.where(keep, x, pltpu.roll(y, k, 0)) + jnp.where(keep, pltpu.roll(x, FOLD[0] - k, 0), y)


def _sublane_sums(p):
    z = [_sublane_pair_sum(p[j], p[j + 4], 4) for j in range(4)]
    w = [_sublane_pair_sum(z[0], z[2], 2), _sublane_pair_sum(z[1], z[3], 2)]
    return _sublane_pair_sum(w[0], w[1], 1)


def _peer_kernel(idx_hbm, uv_hbm, h_ref, gate_ref, x_ref, mod_ref, fn_ref, o_ref, idx_smem, buf, act_rep,
                 row_sem, idx_sem, *, n_steps, final):
    j = pl.program_id(0)
    tb = PEER_TOKENS
    per_tok = PEER_HEADS * PEER_TOPK
    npair = tb * per_tok
    sub = FOLD[0]

    def idx_copy(blk, half):
        return pltpu.make_async_copy(idx_hbm.at[pl.ds(blk * npair, npair)],
                                     idx_smem.at[pl.ds(half * npair, npair)], idx_sem.at[half])

    def row_copy(half, r, prio):
        e = idx_smem[half * npair + r]
        pltpu.make_async_copy(uv_hbm.at[e], buf.at[half, r], row_sem.at[half]).start(priority=prio)

    def wait_rows(half):
        pltpu.make_async_copy(buf.at[half], buf.at[half], row_sem.at[half]).wait()

    eye = (lax.broadcasted_iota(jnp.int32, (per_tok, per_tok), 0)
           == lax.broadcasted_iota(jnp.int32, (per_tok, per_tok), 1))
    g2 = mod_ref[0, 5]
    hi_mask = jnp.uint32(0xFFFF0000)

    def consume(half):
        other = 1 - half
        for t in range(tb):
            tok = half * tb + t
            base = t * per_tok
            hf = h_ref[tok]
            parts = []
            for g in range(per_tok // sub):
                prods = []
                for k in range(sub):
                    e = g * sub + k
                    if e % 2 == 0:
                        row_copy(other, base + e // 2, (e // 2) % 2)
                    word = buf[half, base + e]
                    u = pltpu.bitcast(word << 16, F32)
                    prods.append(u * hf)
                parts.append(_sublane_sums(prods))
            s = jnp.sum(jnp.concatenate(parts, axis=0), axis=1, keepdims=True)
            gate_col = jnp.sum(jnp.where(eye, gate_ref[tok:tok + 1, :], 0.0), axis=1, keepdims=True)
            act_rep[half] = jnp.broadcast_to(_gelu_tanh(s) * gate_col, (per_tok, FOLD[1]))
            accs = [jnp.zeros(FOLD, F32) for _ in range(4)]
            for e in range(per_tok):
                if e % 2 == 0:
                    row_copy(other, base + per_tok // 2 + e // 2, (e // 2) % 2)
                a = jnp.broadcast_to(act_rep[half, e:e + 1, :], FOLD)
                v = pltpu.bitcast(buf[half, base + e] & hi_mask, F32)
                accs[e % 4] = accs[e % 4] + a * v
            xn = x_ref[tok] + g2 * ((accs[0] + accs[1]) + (accs[2] + accs[3]))
            if final:
                ms = (jnp.sum(jnp.sum(xn * xn, axis=1, keepdims=True), axis=0, keepdims=True)
                      * (1.0 / (FOLD[0] * FOLD[1])))
                xn = xn * lax.rsqrt(ms + EPS) * fn_ref[...]
            o_ref[tok] = xn

    @pl.when(j == 0)
    def _():
        first = idx_copy(0, 0)
        first.start()
        first.wait()

        def body(r, carry):
            row_copy(0, r, 0)
            return carry
        lax.fori_loop(0, npair, body, 0, unroll=8)
        idx_copy(1, 1).start()

    more = j + 1 < n_steps
    idx_copy(2 * j + 1, 1).wait()

    @pl.when(more)
    def _():
        idx_copy(2 * j + 2, 0).start()

    wait_rows(0)
    consume(0)

    @pl.when(more)
    def _():
        idx_copy(2 * j + 3, 1).start()

    wait_rows(1)

    @pl.when(more)
    def _():
        idx_copy(2 * j + 2, 0).wait()

    consume(1)

    @pl.when(j == n_steps - 1)
    def _():
        wait_rows(0)


def _peer_experts(h2, idx, gate, uv, x_mid, mods, final_gain, s_out, ctx_rows, final):
    t = h2.shape[0]
    tb = 2 * PEER_TOKENS
    per_tok = PEER_HEADS * PEER_TOPK
    n_steps = t // tb
    assert t % tb == 0 and s_out % tb == 0 and ctx_rows % tb == 0
    ctx_cond = mods.shape[0] - 1

    def mod_index(i):
        tok = i * tb
        return (jnp.where(tok % s_out < ctx_rows, ctx_cond, tok // s_out), 0, 0, 0)

    kernel = functools.partial(_peer_kernel, n_steps=n_steps, final=final)
    fold_spec = pl.BlockSpec((tb,) + FOLD, lambda i: (i, 0, 0))
    return pl.pallas_call(
        kernel,
        grid=(n_steps,),
        in_specs=[
            pl.BlockSpec(memory_space=pl.ANY),
            pl.BlockSpec(memory_space=pl.ANY),
            fold_spec,
            pl.BlockSpec((tb, per_tok), lambda i: (i, 0)),
            fold_spec,
            pl.BlockSpec((1, N_MOD) + FOLD, mod_index),
            pl.BlockSpec(FOLD, lambda i: (0, 0)),
        ],
        out_specs=fold_spec,
        out_shape=jax.ShapeDtypeStruct((t,) + FOLD, F32),
        scratch_shapes=[
            pltpu.SMEM((2 * PEER_TOKENS * per_tok,), jnp.int32),
            pltpu.VMEM((2, PEER_TOKENS * per_tok) + FOLD, jnp.uint32),
            pltpu.VMEM((2, per_tok, FOLD[1]), F32),
            pltpu.SemaphoreType.DMA((2,)),
            pltpu.SemaphoreType.DMA((2,)),
        ],
        compiler_params=_cparams(1),
        name="peer_experts",
    )(idx, uv, h2, gate, x_mid, mods.reshape(mods.shape[0], N_MOD, *FOLD), final_gain.reshape(FOLD))


def _pack_experts(u, v):
    n_exp = u.shape[0]
    ub = lax.bitcast_convert_type(u.astype(BF16), jnp.uint16).astype(jnp.uint32)
    vb = lax.bitcast_convert_type(v.astype(BF16), jnp.uint16).astype(jnp.uint32)
    return ((vb << 16) | ub).reshape(n_exp, *FOLD)


def _rope_tables(n_ctx, n_lat):
    t = jnp.arange(n_lat)
    inv = ROPE_THETA ** (-jnp.arange(0, AXIS_DIM, 2, dtype=F32) / AXIS_DIM)
    row = (t // GRID_W).astype(F32)[:, None] * inv
    col = (t % GRID_W).astype(F32)[:, None] * inv
    cos = jnp.concatenate([jnp.cos(row), jnp.cos(row), jnp.cos(col), jnp.cos(col)], axis=1)
    sin = jnp.concatenate([-jnp.sin(row), jnp.sin(row), -jnp.sin(col), jnp.sin(col)], axis=1)
    cos = jnp.concatenate([jnp.ones((n_ctx, HEAD_DIM), F32), cos], axis=0)
    sin = jnp.concatenate([jnp.zeros((n_ctx, HEAD_DIM), F32), sin], axis=0)
    return jnp.tile(cos, (1, A_HEADS)), jnp.tile(sin, (1, A_HEADS))


def kernel(x, c, ctx, c_ctx, w_mod, b_mod, norm_mix, norm_ffn, w_in, conv_w, attn_sink, na_rpb, lb_logits, group_norm, w_out, peer_wq, peer_k1, peer_k2, peer_u, peer_v, final_norm):
    b, n_lat, d = x.shape
    n_ctx = ctx.shape[1]
    depth = w_mod.shape[0]
    seq = n_ctx + n_lat
    assert n_ctx % ROW_TILE == 0 and n_lat % ROW_TILE == 0 and n_lat // GRID_W >= NB_ROWS
    assert n_lat >= WIN_BLOCK + 2 * WINDOW and w_in.shape[2] == IN_W
    n_ctx_tiles = n_ctx // ROW_TILE

    n_cond = b + 1
    pad = (-n_cond) % 8
    cvec = jnp.concatenate([c, c_ctx[None, :], jnp.zeros((pad, d), F32)], axis=0)
    mods = _modulation(cvec, w_mod, b_mod)[:, :n_cond].reshape(depth, n_cond, N_MOD, d)

    lb_soft = jax.nn.softmax(lb_logits.astype(F32), axis=0)
    lower_bounds = jnp.cumsum(lb_soft, axis=0) - lb_soft[0:1]
    cos_t, sin_t = _rope_tables(n_ctx, n_lat)

    xc = jnp.concatenate([ctx, x], axis=1)
    out = None
    for l in range(depth):
        last = l == depth - 1
        proj = _in_projection(xc, mods[l], norm_mix[l], w_in[l].astype(BF16), n_ctx_tiles)
        ya = _attention_a(proj, cos_t, sin_t, attn_sink[l], n_ctx, not last)
        yb = _attention_b(proj, _neighbourhood_bias(na_rpb[l]), n_ctx, not last)
        yc = _short_conv(proj, conv_w[l], n_ctx)
        od = _hgrn(proj, lower_bounds[l], n_ctx)
        tile0 = n_ctx_tiles if last else 0
        x_mid, h2, scores_t = _post_mixer(ya, yb, yc, od, proj, xc, mods[l], group_norm[l], w_out[l].astype(BF16),
                                          norm_ffn[l], peer_wq[l].astype(BF16), peer_k1[l].astype(BF16),
                                          peer_k2[l].astype(BF16), n_ctx_tiles, tile0)
        s_out = x_mid.shape[1]
        tokens = b * s_out
        idx_t, gate_t = _routing(scores_t)
        per_tok = PEER_HEADS * PEER_TOPK
        idx = idx_t.reshape(per_tok, tokens).T.reshape(tokens * per_tok)
        gate = gate_t.reshape(per_tok, tokens).T
        uv = _pack_experts(peer_u[l], peer_v[l])
        res = _peer_experts(h2.reshape(tokens, *FOLD), idx, gate, uv, x_mid.reshape(tokens, *FOLD), mods[l],
                            final_norm, s_out, 0 if last else n_ctx, last)
        if last:
            out = res.reshape(b, s_out, d)
        else:
            xc = res.reshape(b, s_out, d)
    return out
```

```python
import functools
import math

import jax
import jax.numpy as jnp
import numpy as np
from jax import lax
from jax.experimental import pallas as pl
from jax.experimental.pallas import tpu as pltpu

F32 = jnp.float32
BF16 = jnp.bfloat16

GRID_W = 64
HEAD_DIM = 64
GROUP_W = 256
A_HEADS = 4
A_KV_HEADS = 2
WINDOW = 128
WIN_BLOCK = 128
B_HEADS = 4
NB_ROWS = 8
NB_COLS = 16
CONV_W = 3
D_HEADS = 4
SCAN_CHUNK = 64
LB_FLOOR = 1e-20
ROPE_THETA = 10000.0
AXIS_DIM = HEAD_DIM // 2
PEER_HEADS = 8
PEER_NKEYS = 128
PEER_TOPK = 16
N_MOD = 6
EPS = 1e-6
MASK_VALUE = -1e30
IN_W = 13 * GROUP_W

ROW_TILE = 256
MOD_COL_TILE = 1536
TOPK_TOKENS = 128
PEER_TOKENS = 8
VMEM_LIMIT = 48 * 1024 * 1024


def _cparams(n_axes):
    return pltpu.CompilerParams(dimension_semantics=("arbitrary",) * n_axes, vmem_limit_bytes=VMEM_LIMIT)


def _dot(a, b):
    return jnp.dot(a, b, preferred_element_type=F32)


def _dot_nt(a, b):
    return lax.dot_general(a, b, (((1,), (1,)), ((), ())), preferred_element_type=F32)


def _dot_tn(a, b):
    return lax.dot_general(a, b, (((0,), (0,)), ((), ())), preferred_element_type=F32)


def _rms(x):
    return x * lax.rsqrt(jnp.mean(x * x, axis=-1, keepdims=True) + EPS)


def _silu(x):
    return x / (1.0 + jnp.exp(-x))


def _mod_kernel(c_ref, w_ref, b_ref, o_ref):
    s = _silu(c_ref[...])
    o_ref[0] = _dot(s.astype(BF16), w_ref[0].astype(BF16)) + b_ref[0]


def _modulation(cvec, w_mod, b_mod):
    depth, d, n = w_mod.shape
    r = cvec.shape[0]
    tn = MOD_COL_TILE
    return pl.pallas_call(
        _mod_kernel,
        grid=(depth, n // tn),
        in_specs=[
            pl.BlockSpec((r, d), lambda l, j: (0, 0)),
            pl.BlockSpec((1, d, tn), lambda l, j: (l, 0, j)),
            pl.BlockSpec((1, 1, tn), lambda l, j: (l, 0, j)),
        ],
        out_specs=pl.BlockSpec((1, r, tn), lambda l, j: (l, 0, j)),
        out_shape=jax.ShapeDtypeStruct((depth, r, n), F32),
        compiler_params=_cparams(2),
        name="adaln_modulation",
    )(cvec, w_mod, b_mod.reshape(depth, 1, n))


def _inproj_kernel(x_ref, mod_ref, g_ref, w_ref, o_ref):
    y = _rms(x_ref[0]) * g_ref[...]
    h = y * (1.0 + mod_ref[0, 1:2, :]) + mod_ref[0, 0:1, :]
    o_ref[0] = _dot(h.astype(BF16), w_ref[...])


def _in_projection(xc, mods, gain, w_bf16, n_ctx_tiles):
    b, s, d = xc.shape
    n = w_bf16.shape[1]
    tm = ROW_TILE
    ctx_cond = mods.shape[0] - 1
    return pl.pallas_call(
        _inproj_kernel,
        grid=(b, s // tm),
        in_specs=[
            pl.BlockSpec((1, tm, d), lambda i, j: (i, j, 0)),
            pl.BlockSpec((1, N_MOD, d), lambda i, j: (jnp.where(j < n_ctx_tiles, ctx_cond, i), 0, 0)),
            pl.BlockSpec((1, d), lambda i, j: (0, 0)),
            pl.BlockSpec((d, n), lambda i, j: (0, 0)),
        ],
        out_specs=pl.BlockSpec((1, tm, n), lambda i, j: (i, j, 0)),
        out_shape=jax.ShapeDtypeStruct((b, s, n), F32),
        compiler_params=_cparams(2),
        name="in_projection",
    )(xc, mods, gain.reshape(1, d), w_bf16)


def _rope(x, cos, sin_signed):
    n = x.shape[1]
    lane = lax.broadcasted_iota(jnp.int32, x.shape, 1)
    fwd = pltpu.roll(x, n - AXIS_DIM // 2, 1)
    bwd = pltpu.roll(x, AXIS_DIM // 2, 1)
    swapped = jnp.where((lane % AXIS_DIM) < AXIS_DIM // 2, fwd, bwd)
    return x * cos + swapped * sin_signed


def _attn_a_kernel(sink_ref, q_ref, k_ref, v_ref, cq_ref, sq_ref, ck_ref, sk_ref, o_ref, kh_ref, vh_ref,
                   *, n_ctx, n_lat, ctx_steps):
    t = pl.program_id(1)
    span = WIN_BLOCK + 2 * WINDOW
    g = A_HEADS // A_KV_HEADS

    @pl.when(t == 0)
    def _():
        kr = _rope(k_ref[0], ck_ref[...], sk_ref[...])
        v = v_ref[0]
        for h in range(A_KV_HEADS):
            kh_ref[h] = kr[:, h * HEAD_DIM:(h + 1) * HEAD_DIM].astype(BF16)
            vh_ref[h] = v[:, h * HEAD_DIM:(h + 1) * HEAD_DIM].astype(BF16)

    q = _rope(q_ref[0], cq_ref[...], sq_ref[...]) * (HEAD_DIM ** -0.5)
    row = lax.broadcasted_iota(jnp.int32, (g * WIN_BLOCK, 1), 0)

    def q_pair(hk):
        parts = [q[:, (hk * g + j) * HEAD_DIM:(hk * g + j + 1) * HEAD_DIM] for j in range(g)]
        return jnp.concatenate(parts, axis=0).astype(BF16)

    def sink_col(hk):
        col = jnp.full((g * WIN_BLOCK, 1), sink_ref[hk * g], F32)
        for j in range(1, g):
            col = jnp.where(row >= j * WIN_BLOCK, sink_ref[hk * g + j], col)
        return col

    def assemble(outs):
        pieces = []
        for hk in range(A_KV_HEADS):
            for j in range(g):
                pieces.append(outs[hk][j * WIN_BLOCK:(j + 1) * WIN_BLOCK])
        return jnp.concatenate(pieces, axis=1)

    def latent_block():
        p0 = (t - ctx_steps) * WIN_BLOCK
        ws = jnp.clip(p0 - WINDOW, 0, n_lat - span)
        kstart = pl.multiple_of(n_ctx + ws, WIN_BLOCK)
        qpos = p0 + row % WIN_BLOCK
        kpos = ws + lax.broadcasted_iota(jnp.int32, (1, span), 1)
        mask = jnp.abs(qpos - kpos) <= WINDOW
        outs = []
        for hk in range(A_KV_HEADS):
            q2 = q_pair(hk)
            s_loc = _dot_nt(q2, kh_ref[hk, pl.ds(kstart, span), :])
            s_loc = jnp.where(mask, s_loc, MASK_VALUE)
            s_ctx = _dot_nt(q2, kh_ref[hk, 0:n_ctx, :])
            sk = sink_col(hk)
            m = jnp.maximum(jnp.maximum(jnp.max(s_loc, axis=1, keepdims=True),
                                        jnp.max(s_ctx, axis=1, keepdims=True)), sk)
            p_loc = jnp.exp(s_loc - m)
            p_ctx = jnp.exp(s_ctx - m)
            den = (jnp.sum(p_loc, axis=1, keepdims=True) + jnp.sum(p_ctx, axis=1, keepdims=True)
                   + jnp.exp(sk - m))
            o = (_dot(p_loc.astype(BF16), vh_ref[hk, pl.ds(kstart, span), :])
                 + _dot(p_ctx.astype(BF16), vh_ref[hk, 0:n_ctx, :]))
            outs.append(o / den)
        o_ref[0] = assemble(outs)

    def ctx_block():
        outs = []
        for hk in range(A_KV_HEADS):
            q2 = q_pair(hk)
            s_ctx = _dot_nt(q2, kh_ref[hk, 0:n_ctx, :])
            sk = sink_col(hk)
            m = jnp.maximum(jnp.max(s_ctx, axis=1, keepdims=True), sk)
            p_ctx = jnp.exp(s_ctx - m)
            den = jnp.sum(p_ctx, axis=1, keepdims=True) + jnp.exp(sk - m)
            outs.append(_dot(p_ctx.astype(BF16), vh_ref[hk, 0:n_ctx, :]) / den)
        o_ref[0] = assemble(outs)

    if ctx_steps:
        pl.when(t < ctx_steps)(ctx_block)
        pl.when(t >= ctx_steps)(latent_block)
    else:
        latent_block()


def _attention_a(proj, cos_t, sin_t, sink, n_ctx, with_ctx):
    b, s, _ = proj.shape
    n_lat = s - n_ctx
    ctx_blocks = n_ctx // WIN_BLOCK
    ctx_steps = ctx_blocks if with_ctx else 0
    off = ctx_blocks - ctx_steps
    steps = ctx_steps + n_lat // WIN_BLOCK
    kw = A_KV_HEADS * HEAD_DIM
    kernel = functools.partial(_attn_a_kernel, n_ctx=n_ctx, n_lat=n_lat, ctx_steps=ctx_steps)
    return pl.pallas_call(
        kernel,
        grid=(b, steps),
        in_specs=[
            pl.BlockSpec(memory_space=pltpu.SMEM),
            pl.BlockSpec((1, WIN_BLOCK, GROUP_W), lambda i, t: (i, t + off, 0)),
            pl.BlockSpec((1, s, kw), lambda i, t: (i, 0, 2)),
            pl.BlockSpec((1, s, kw), lambda i, t: (i, 0, 3)),
            pl.BlockSpec((WIN_BLOCK, GROUP_W), lambda i, t: (t + off, 0)),
            pl.BlockSpec((WIN_BLOCK, GROUP_W), lambda i, t: (t + off, 0)),
            pl.BlockSpec((s, kw), lambda i, t: (0, 0)),
            pl.BlockSpec((s, kw), lambda i, t: (0, 0)),
        ],
        out_specs=pl.BlockSpec((1, WIN_BLOCK, GROUP_W), lambda i, t: (i, t, 0)),
        out_shape=jax.ShapeDtypeStruct((b, steps * WIN_BLOCK, GROUP_W), F32),
        scratch_shapes=[pltpu.VMEM((A_KV_HEADS, s, HEAD_DIM), BF16), pltpu.VMEM((A_KV_HEADS, s, HEAD_DIM), BF16)],
        compiler_params=_cparams(2),
        name="window_attention",
    )(sink, proj, proj, proj, cos_t, sin_t, cos_t, sin_t)


def _attn_b_kernel(q_ref, k_ref, v_ref, bias_ref, o_ref, kh_ref, vh_ref, *, n_ctx, n_lat, ctx_steps):
    t = pl.program_id(1)
    rows = n_lat // GRID_W
    nkeys = NB_ROWS * GRID_W

    @pl.when(t == 0)
    def _():
        k = k_ref[0]
        v = v_ref[0]
        for h in range(B_HEADS):
            kh_ref[h] = k[:, h * HEAD_DIM:(h + 1) * HEAD_DIM].astype(BF16)
            vh_ref[h] = v[:, h * HEAD_DIM:(h + 1) * HEAD_DIM].astype(BF16)

    q = q_ref[0] * (HEAD_DIM ** -0.5)

    def latent_block():
        r = t - ctx_steps
        r0 = jnp.clip(r - NB_ROWS // 2, 0, rows - NB_ROWS)
        kstart = pl.multiple_of(n_ctx + r0 * GRID_W, GRID_W)
        outs = []
        for h in range(B_HEADS):
            qh = q[:, h * HEAD_DIM:(h + 1) * HEAD_DIM].astype(BF16)
            s_loc = _dot_nt(qh, kh_ref[h, pl.ds(kstart, nkeys), :]) + bias_ref[0, h]
            s_ctx = _dot_nt(qh, kh_ref[h, 0:n_ctx, :])
            m = jnp.maximum(jnp.max(s_loc, axis=1, keepdims=True), jnp.max(s_ctx, axis=1, keepdims=True))
            p_loc = jnp.exp(s_loc - m)
            p_ctx = jnp.exp(s_ctx - m)
            den = jnp.sum(p_loc, axis=1, keepdims=True) + jnp.sum(p_ctx, axis=1, keepdims=True)
            o = (_dot(p_loc.astype(BF16), vh_ref[h, pl.ds(kstart, nkeys), :])
                 + _dot(p_ctx.astype(BF16), vh_ref[h, 0:n_ctx, :]))
            outs.append(o / den)
        o_ref[0] = jnp.concatenate(outs, axis=1)

    def ctx_block():
        outs = []
        for h in range(B_HEADS):
            qh = q[:, h * HEAD_DIM:(h + 1) * HEAD_DIM].astype(BF16)
            s_ctx = _dot_nt(qh, kh_ref[h, 0:n_ctx, :])
            m = jnp.max(s_ctx, axis=1, keepdims=True)
            p_ctx = jnp.exp(s_ctx - m)
            den = jnp.sum(p_ctx, axis=1, keepdims=True)
            outs.append(_dot(p_ctx.astype(BF16), vh_ref[h, 0:n_ctx, :]) / den)
        o_ref[0] = jnp.concatenate(outs, axis=1)

    if ctx_steps:
        pl.when(t < ctx_steps)(ctx_block)
        pl.when(t >= ctx_steps)(latent_block)
    else:
        latent_block()


def _attention_b(proj, bias, n_ctx, with_ctx):
    b, s, _ = proj.shape
    n_lat = s - n_ctx
    rows = n_lat // GRID_W
    ctx_blocks = n_ctx // GRID_W
    ctx_steps = ctx_blocks if with_ctx else 0
    off = ctx_blocks - ctx_steps
    steps = ctx_steps + rows

    def bias_index(i, t):
        r = jnp.maximum(t - ctx_steps, 0)
        r0 = jnp.clip(r - NB_ROWS // 2, 0, rows - NB_ROWS)
        return (r - r0, 0, 0, 0)

    kernel = functools.partial(_attn_b_kernel, n_ctx=n_ctx, n_lat=n_lat, ctx_steps=ctx_steps)
    return pl.pallas_call(
        kernel,
        grid=(b, steps),
        in_specs=[
            pl.BlockSpec((1, GRID_W, GROUP_W), lambda i, t: (i, t + off, 2)),
            pl.BlockSpec((1, s, GROUP_W), lambda i, t: (i, 0, 3)),
            pl.BlockSpec((1, s, GROUP_W), lambda i, t: (i, 0, 4)),
            pl.BlockSpec((1, B_HEADS, GRID_W, NB_ROWS * GRID_W), bias_index),
        ],
        out_specs=pl.BlockSpec((1, GRID_W, GROUP_W), lambda i, t: (i, t, 0)),
        out_shape=jax.ShapeDtypeStruct((b, steps * GRID_W, GROUP_W), F32),
        scratch_shapes=[pltpu.VMEM((B_HEADS, s, HEAD_DIM), BF16), pltpu.VMEM((B_HEADS, s, HEAD_DIM), BF16)],
        compiler_params=_cparams(2),
        name="neighbourhood_attention",
    )(proj, proj, proj, bias)


def _neighbourhood_bias(rpb):
    qc = np.arange(GRID_W)[:, None]
    kc = np.arange(GRID_W)[None, :]
    win0 = np.clip(qc - NB_COLS // 2, 0, GRID_W - NB_COLS)
    ok = (kc >= win0) & (kc < win0 + NB_COLS)
    d_col = np.clip(kc - qc, -(NB_COLS - 1), NB_COLS - 1) + (NB_COLS - 1)
    onehot = (d_col[None] == np.arange(2 * NB_COLS - 1)[:, None, None]).astype(np.float32)
    by_rel = jnp.stack([rpb[:, NB_ROWS - 1 - rel:2 * NB_ROWS - 1 - rel, :] for rel in range(NB_ROWS)], axis=0)
    tab = jnp.einsum('rhkc,cqj->rhqkj', by_rel.astype(F32), jnp.asarray(onehot), precision=lax.Precision.HIGHEST)
    tab = jnp.where(ok[None, None, :, None, :], tab, MASK_VALUE)
    return tab.reshape(NB_ROWS, B_HEADS, GRID_W, NB_ROWS * GRID_W)


def _conv_kernel(cx_ref, cb_ref, cc_ref, w_ref, o_ref, pad_ref, *, n_ctx, seq):
    edge = 8
    chunk = ROW_TILE
    pad_ref[0:edge, :] = jnp.zeros((edge, GROUP_W), F32)
    pad_ref[edge + seq:2 * edge + seq, :] = jnp.zeros((edge, GROUP_W), F32)
    for c in range(seq // chunk):
        sl = slice(c * chunk, (c + 1) * chunk)
        pad_ref[edge + c * chunk:edge + (c + 1) * chunk, :] = cc_ref[0, sl, :] * cx_ref[0, sl, :]
    w = w_ref[...]
    for c in range(seq // chunk):
        lo = edge + c * chunk
        row = c * chunk + lax.broadcasted_iota(jnp.int32, (chunk, 1), 0)
        prev = pad_ref[lo - 1:lo - 1 + chunk, :]
        cur = pad_ref[lo:lo + chunk, :]
        nxt = pad_ref[lo + 1:lo + 1 + chunk, :]
        prev = jnp.where((row == 0) | (row == n_ctx), 0.0, prev)
        nxt = jnp.where((row == n_ctx - 1) | (row == seq - 1), 0.0, nxt)
        conv = w[0:1] * prev + w[1:2] * cur + w[2:3] * nxt
        o_ref[0, c * chunk:(c + 1) * chunk, :] = cb_ref[0, c * chunk:(c + 1) * chunk, :] * conv


def _short_conv(proj, conv_w, n_ctx):
    b, s, _ = proj.shape
    kernel = functools.partial(_conv_kernel, n_ctx=n_ctx, seq=s)
    return pl.pallas_call(
        kernel,
        grid=(b,),
        in_specs=[
            pl.BlockSpec((1, s, GROUP_W), lambda i: (i, 0, 5)),
            pl.BlockSpec((1, s, GROUP_W), lambda i: (i, 0, 6)),
            pl.BlockSpec((1, s, GROUP_W), lambda i: (i, 0, 7)),
            pl.BlockSpec((CONV_W, GROUP_W), lambda i: (0, 0)),
        ],
        out_specs=pl.BlockSpec((1, s, GROUP_W), lambda i: (i, 0, 0)),
        out_shape=jax.ShapeDtypeStruct((b, s, GROUP_W), F32),
        scratch_shapes=[pltpu.VMEM((s + 16, GROUP_W), F32)],
        compiler_params=_cparams(1),
        name="gated_short_conv",
    )(proj, proj, proj, conv_w)


OFFSET_GROUP = 16


def _hgrn_chunk(d, base, q_ref, z_ref, i_ref, gate_consts, o_ref, st_ref, kst, fst, vst, sst, pbuf, ones_bd, bd_mask):
    ch = SCAN_CHUNK
    sgn = 1 if d == 0 else -1
    lb_floor, one_minus_lb, floor_excess = gate_consts
    rows = pl.ds(base, ch)
    z = z_ref[0, rows, :]
    q = q_ref[0, rows, :]
    v = i_ref[0, rows, :]
    sig = 1.0 / (1.0 + jnp.exp(-z))
    f = lb_floor + one_minus_lb * sig
    k = one_minus_lb * (1.0 - sig) - floor_excess
    kst[d, ch:2 * ch, :] = k
    fst[d, ch:2 * ch, :] = f
    vst[d, ch:2 * ch, :] = v

    def shifted(ref, o):
        return ref[d, ch - sgn * o:2 * ch - sgn * o, :]

    inc = f
    exc = f
    step = 1
    while step < ch:
        sst[d, ch:2 * ch, :] = inc
        inc = inc * shifted(sst, step)
        step *= 2
    step = 1
    while step < ch:
        sst[d, ch:2 * ch, :] = exc
        exc = exc * shifted(sst, -step)
        step *= 2
    sst[d, ch:2 * ch, :] = exc
    exc = shifted(sst, -1)
    total = inc[ch - 1:ch, :] if d == 0 else inc[0:1, :]

    decay = jnp.ones((ch, GROUP_W), F32)
    acc = jnp.zeros((ch, GROUP_W), F32)
    for g0 in range(0, ch, OFFSET_GROUP):
        for oo in range(OFFSET_GROUP):
            o = g0 + oo
            if o > 0:
                decay = decay * shifted(fst, o - 1)
            pbuf[d, oo * ch:(oo + 1) * ch, :] = (q * shifted(kst, o) * decay).astype(BF16)
        head_sums = _dot(pbuf[d], ones_bd)
        for oo in range(OFFSET_GROUP):
            acc = acc + head_sums[oo * ch:(oo + 1) * ch, :] * shifted(vst, g0 + oo)

    st = st_ref[d]
    o_inter = _dot_nt((q * inc).astype(BF16), st.astype(BF16))
    upd = _dot_tn(v.astype(BF16), (k * exc).astype(BF16))
    st_ref[d] = st * total + upd * bd_mask
    o_ref[0, rows, :] = o_ref[0, rows, :] + acc + o_inter


def _hgrn_kernel(q_ref, zf_ref, zb_ref, i_ref, lb_ref, o_ref, st_ref, kst, fst, vst, sst, pbuf, *, n_ctx, seq):
    ch = SCAN_CHUNK
    n_chunks = seq // ch
    ctx_chunks = n_ctx // ch
    lb = lb_ref[...]
    lb_floor = jnp.maximum(lb, LB_FLOOR)
    gate_consts = (lb_floor, 1.0 - lb, lb_floor - lb)
    hi = lax.broadcasted_iota(jnp.int32, (GROUP_W, GROUP_W), 0) // HEAD_DIM
    hj = lax.broadcasted_iota(jnp.int32, (GROUP_W, GROUP_W), 1) // HEAD_DIM
    bd_mask = jnp.where(hi == hj, 1.0, 0.0).astype(F32)
    ones_bd = bd_mask.astype(BF16)

    o_ref[...] = jnp.zeros(o_ref.shape, F32)
    st_ref[...] = jnp.zeros(st_ref.shape, F32)
    kst[...] = jnp.zeros(kst.shape, F32)
    fst[...] = jnp.zeros(fst.shape, F32)
    vst[...] = jnp.zeros(vst.shape, F32)
    sst[...] = jnp.ones(sst.shape, F32)

    def body(s, carry):
        cf = s
        cb = jnp.where(s < ctx_chunks, ctx_chunks - 1 - s, n_chunks - 1 - (s - ctx_chunks))
        for d, c, z_ref in ((0, cf, zf_ref), (1, cb, zb_ref)):
            base = pl.multiple_of(c * ch, ch)
            _hgrn_chunk(d, base, q_ref, z_ref, i_ref, gate_consts, o_ref, st_ref, kst, fst, vst, sst, pbuf,
                        ones_bd, bd_mask)
        return carry

    lax.fori_loop(0, n_chunks, body, 0)


def _hgrn(proj, lower_bound, n_ctx):
    b, s, _ = proj.shape
    ch = SCAN_CHUNK
    kernel = functools.partial(_hgrn_kernel, n_ctx=n_ctx, seq=s)
    col = lambda c: pl.BlockSpec((1, s, GROUP_W), lambda i: (i, 0, c))
    return pl.pallas_call(
        kernel,
        grid=(b,),
        in_specs=[col(8), col(9), col(10), col(11), pl.BlockSpec((1, GROUP_W), lambda i: (0, 0))],
        out_specs=pl.BlockSpec((1, s, GROUP_W), lambda i: (i, 0, 0)),
        out_shape=jax.ShapeDtypeStruct((b, s, GROUP_W), F32),
        scratch_shapes=[
            pltpu.VMEM((2, GROUP_W, GROUP_W), F32),
            pltpu.VMEM((2, 3 * ch, GROUP_W), F32),
            pltpu.VMEM((2, 3 * ch, GROUP_W), F32),
            pltpu.VMEM((2, 3 * ch, GROUP_W), F32),
            pltpu.VMEM((2, 3 * ch, GROUP_W), F32),
            pltpu.VMEM((2, OFFSET_GROUP * ch, GROUP_W), BF16),
        ],
        compiler_params=_cparams(1),
        name="hgrn2_bidirectional",
    )(proj, proj, proj, proj, lower_bound.reshape(1, GROUP_W))


def _post_kernel(ya_ref, yb_ref, yc_ref, od_ref, dg_ref, x_ref, mod_ref, gn_ref, wo_ref, nf_ref, wq_ref,
                 k1_ref, k2_ref, xo_ref, h_ref, sc_ref):
    yd = od_ref[0] * _silu(dg_ref[0])
    gn = gn_ref[...]
    parts = (ya_ref[0], yb_ref[0], yc_ref[0], yd)
    normed = [_rms(p) * gn[:, i * GROUP_W:(i + 1) * GROUP_W] for i, p in enumerate(parts)]
    y = _dot(jnp.concatenate(normed, axis=1).astype(BF16), wo_ref[...])
    xn = x_ref[0] + mod_ref[0, 2:3, :] * y
    xo_ref[0] = xn
    h = _rms(xn) * nf_ref[...] * (1.0 + mod_ref[0, 4:5, :]) + mod_ref[0, 3:4, :]
    h_ref[0] = h
    qv = _dot(h.astype(BF16), wq_ref[...])
    half = PEER_NKEYS
    for g in range(2 * PEER_HEADS):
        keys = k1_ref[...] if g % 2 == 0 else k2_ref[...]
        sc_ref[g] = _dot_nt(keys, qv[:, g * half:(g + 1) * half].astype(BF16))


def _post_mixer(ya, yb, yc, od, proj, xc, mods, gnorm, wo_bf16, nffn, wq_bf16, k1_bf16, k2_bf16, n_ctx_tiles, tile0):
    b, s, d = xc.shape
    tm = ROW_TILE
    nt = s // tm - tile0
    s_out = nt * tm
    ctx_cond = mods.shape[0] - 1
    nq = wq_bf16.shape[1]
    att0 = tile0 - (s - ya.shape[1]) // tm
    att = lambda: pl.BlockSpec((1, tm, GROUP_W), lambda i, j: (i, j + att0, 0))
    grp = lambda: pl.BlockSpec((1, tm, GROUP_W), lambda i, j: (i, j + tile0, 0))
    const = lambda shape: pl.BlockSpec(shape, lambda i, j: (0,) * len(shape))
    return pl.pallas_call(
        _post_kernel,
        grid=(b, nt),
        in_specs=[
            att(), att(), grp(), grp(),
            pl.BlockSpec((1, tm, GROUP_W), lambda i, j: (i, j + tile0, 12)),
            pl.BlockSpec((1, tm, d), lambda i, j: (i, j + tile0, 0)),
            pl.BlockSpec((1, N_MOD, d), lambda i, j: (jnp.where(j + tile0 < n_ctx_tiles, ctx_cond, i), 0, 0)),
            const((1, d)), const((d, d)), const((1, d)), const((d, nq)),
            const((PEER_NKEYS, PEER_NKEYS)), const((PEER_NKEYS, PEER_NKEYS)),
        ],
        out_specs=[
            pl.BlockSpec((1, tm, d), lambda i, j: (i, j, 0)),
            pl.BlockSpec((1, tm, d), lambda i, j: (i, j, 0)),
            pl.BlockSpec((2 * PEER_HEADS, PEER_NKEYS, tm), lambda i, j: (0, 0, i * nt + j)),
        ],
        out_shape=[
            jax.ShapeDtypeStruct((b, s_out, d), F32),
            jax.ShapeDtypeStruct((b, s_out, d), F32),
            jax.ShapeDtypeStruct((2 * PEER_HEADS, PEER_NKEYS, b * s_out), F32),
        ],
        compiler_params=_cparams(2),
        name="post_mixer",
    )(ya, yb, yc, od, proj, xc, mods, gnorm.reshape(1, d), wo_bf16, nffn.reshape(1, d), wq_bf16, k1_bf16, k2_bf16)


def _top16(s):
    n = s.shape[0]
    rows = lax.broadcasted_iota(jnp.int32, s.shape, 0).astype(F32)
    vals, ids = [], []
    cur = s
    for _ in range(PEER_TOPK):
        m = jnp.max(cur, axis=0, keepdims=True)
        am = jnp.min(jnp.where(cur == m, rows, float(n)), axis=0, keepdims=True)
        vals.append(m)
        ids.append(am)
        cur = jnp.where(rows == am, -jnp.inf, cur)
    return jnp.concatenate(vals, axis=0), jnp.concatenate(ids, axis=0)


def _pick(table, sel):
    out = jnp.zeros(sel.shape, table.dtype)
    for a in range(PEER_TOPK):
        out = out + jnp.where(sel == float(a), table[a:a + 1, :], 0.0)
    return out


def _route_kernel(sc_ref, idx_ref, gate_ref):
    for h in range(PEER_HEADS):
        v1, i1 = _top16(sc_ref[2 * h])
        v2, i2 = _top16(sc_ref[2 * h + 1])
        cand = jnp.concatenate([v1[a:a + 1, :] + v2 for a in range(PEER_TOPK)], axis=0)
        top_s, pos = _top16(cand)
        a_sel = jnp.floor(pos * (1.0 / PEER_TOPK))
        b_sel = pos - a_sel * PEER_TOPK
        idx_ref[h] = (_pick(i1, a_sel) * PEER_NKEYS + _pick(i2, b_sel)).astype(jnp.int32)
        e = jnp.exp(top_s - jnp.max(top_s, axis=0, keepdims=True))
        gate_ref[h] = e / jnp.sum(e, axis=0, keepdims=True)


def _routing(scores_t):
    g, nk, t = scores_t.shape
    tt = TOPK_TOKENS
    out_spec = pl.BlockSpec((PEER_HEADS, PEER_TOPK, tt), lambda i: (0, 0, i))
    return pl.pallas_call(
        _route_kernel,
        grid=(t // tt,),
        in_specs=[pl.BlockSpec((g, nk, tt), lambda i: (0, 0, i))],
        out_specs=[out_spec, out_spec],
        out_shape=[jax.ShapeDtypeStruct((PEER_HEADS, PEER_TOPK, t), jnp.int32),
                   jax.ShapeDtypeStruct((PEER_HEADS, PEER_TOPK, t), F32)],
        compiler_params=_cparams(1),
        name="peer_routing",
    )(scores_t)


def _gelu_tanh(x):
    return 0.5 * x * (1.0 + jnp.tanh(math.sqrt(2.0 / math.pi) * (x + 0.044715 * (x * x * x))))


FOLD = (8, 128)


def _sublane_pair_sum(x, y, k):
    sub = lax.broadcasted_iota(jnp.int32, FOLD, 0)
    keep = (sub % (2 * k)) < k
    return jnp.where(keep, x, pltpu.roll(y, k, 0)) + jnp.where(keep, pltpu.roll(x, FOLD[0] - k, 0), y)


def _sublane_sums(p):
    z = [_sublane_pair_sum(p[j], p[j + 4], 4) for j in range(4)]
    w = [_sublane_pair_sum(z[0], z[2], 2), _sublane_pair_sum(z[1], z[3], 2)]
    return _sublane_pair_sum(w[0], w[1], 1)


def _peer_kernel(idx_hbm, tbl_hbm, h_ref, gate_ref, x_ref, mod_ref, fn_ref, o_ref, idx_smem, tbl, act_rep,
                 tbl_sem, idx_sem, *, n_steps, part, final):
    j = pl.program_id(0)
    tb = PEER_TOKENS
    per_tok = PEER_HEADS * PEER_TOPK
    npair = tb * per_tok
    sub = FOLD[0]
    n_rows = tbl.shape[0] - 1

    def idx_copy(blk, half):
        return pltpu.make_async_copy(idx_hbm.at[pl.ds(blk * npair, npair)],
                                     idx_smem.at[pl.ds(half * npair, npair)], idx_sem.at[half])

    eye = (lax.broadcasted_iota(jnp.int32, (per_tok, per_tok), 0)
           == lax.broadcasted_iota(jnp.int32, (per_tok, per_tok), 1))
    g2 = mod_ref[0, 5]
    hi_mask = jnp.uint32(0xFFFF0000)

    def consume(half):
        for t in range(tb):
            tok = half * tb + t
            hf = h_ref[tok]
            rows = [idx_smem[half * npair + t * per_tok + e] for e in range(per_tok)]
            parts = []
            for g in range(per_tok // sub):
                prods = []
                for k in range(sub):
                    word = tbl[rows[g * sub + k]]
                    prods.append(pltpu.bitcast(word << 16, F32) * hf)
                parts.append(_sublane_sums(prods))
            s = jnp.sum(jnp.concatenate(parts, axis=0), axis=1, keepdims=True)
            gate_col = jnp.sum(jnp.where(eye, gate_ref[tok:tok + 1, :], 0.0), axis=1, keepdims=True)
            act_rep[half] = jnp.broadcast_to(_gelu_tanh(s) * gate_col, (per_tok, FOLD[1]))
            accs = [jnp.zeros(FOLD, F32) for _ in range(4)]
            for e in range(per_tok):
                a = jnp.broadcast_to(act_rep[half, e:e + 1, :], FOLD)
                v = pltpu.bitcast(tbl[rows[e]] & hi_mask, F32)
                accs[e % 4] = accs[e % 4] + a * v
            xn = x_ref[tok] + g2 * ((accs[0] + accs[1]) + (accs[2] + accs[3]))
            if final:
                ms = (jnp.sum(jnp.sum(xn * xn, axis=1, keepdims=True), axis=0, keepdims=True)
                      * (1.0 / (FOLD[0] * FOLD[1])))
                xn = xn * lax.rsqrt(ms + EPS) * fn_ref[...]
            o_ref[tok] = xn

    @pl.when(j == 0)
    def _():
        idx_copy(0, 0).start()
        idx_copy(1, 1).start()
        whole = pltpu.make_async_copy(tbl_hbm.at[pl.ds(part * n_rows, n_rows)], tbl.at[pl.ds(0, n_rows)], tbl_sem)
        whole.start()
        tbl[n_rows] = jnp.zeros(FOLD, jnp.uint32)
        whole.wait()

    more = j + 1 < n_steps
    idx_copy(2 * j, 0).wait()
    consume(0)

    @pl.when(more)
    def _():
        idx_copy(2 * j + 2, 0).start()

    idx_copy(2 * j + 1, 1).wait()
    consume(1)

    @pl.when(more)
    def _():
        idx_copy(2 * j + 3, 1).start()


def _peer_pass(h2, rows, gate, uv, x_in, mods, final_gain, s_out, ctx_rows, part, n_parts, final):
    t = h2.shape[0]
    tb = 2 * PEER_TOKENS
    per_tok = PEER_HEADS * PEER_TOPK
    n_steps = t // tb
    n_rows = uv.shape[0] // n_parts
    assert t % tb == 0 and s_out % tb == 0 and ctx_rows % tb == 0
    ctx_cond = mods.shape[0] - 1

    def mod_index(i):
        tok = i * tb
        return (jnp.where(tok % s_out < ctx_rows, ctx_cond, tok // s_out), 0, 0, 0)

    kernel = functools.partial(_peer_kernel, n_steps=n_steps, part=part, final=final)
    fold_spec = pl.BlockSpec((tb,) + FOLD, lambda i: (i, 0, 0))
    return pl.pallas_call(
        kernel,
        grid=(n_steps,),
        in_specs=[
            pl.BlockSpec(memory_space=pl.ANY),
            pl.BlockSpec(memory_space=pl.ANY),
            fold_spec,
            pl.BlockSpec((tb, per_tok), lambda i: (i, 0)),
            fold_spec,
            pl.BlockSpec((1, N_MOD) + FOLD, mod_index),
            pl.BlockSpec(FOLD, lambda i: (0, 0)),
        ],
        out_specs=fold_spec,
        out_shape=jax.ShapeDtypeStruct((t,) + FOLD, F32),
        scratch_shapes=[
            pltpu.SMEM((2 * PEER_TOKENS * per_tok,), jnp.int32),
            pltpu.VMEM((n_rows + 1,) + FOLD, jnp.uint32),
            pltpu.VMEM((2, per_tok, FOLD[1]), F32),
            pltpu.SemaphoreType.DMA,
            pltpu.SemaphoreType.DMA((2,)),
        ],
        compiler_params=_cparams(1),
        name="peer_experts",
    )(rows, uv, h2, gate, x_in, mods.reshape(mods.shape[0], N_MOD, *FOLD), final_gain.reshape(FOLD))


PEER_TABLE_PARTS = 2


def _peer_experts(h2, idx, gate, uv, x_mid, mods, final_gain, s_out, ctx_rows, final):
    n_rows = uv.shape[0] // PEER_TABLE_PARTS
    x_cur = x_mid
    for part in range(PEER_TABLE_PARTS):
        rows = jnp.where(idx // n_rows == part, idx % n_rows, n_rows)
        x_cur = _peer_pass(h2, rows, gate, uv, x_cur, mods, final_gain, s_out, ctx_rows, part, PEER_TABLE_PARTS,
                           final and part == PEER_TABLE_PARTS - 1)
    return x_cur


def _pack_experts(u, v):
    n_exp = u.shape[0]
    ub = lax.bitcast_convert_type(u.astype(BF16), jnp.uint16).astype(jnp.uint32)
    vb = lax.bitcast_convert_type(v.astype(BF16), jnp.uint16).astype(jnp.uint32)
    return ((vb << 16) | ub).reshape(n_exp, *FOLD)


def _rope_tables(n_ctx, n_lat):
    t = jnp.arange(n_lat)
    inv = ROPE_THETA ** (-jnp.arange(0, AXIS_DIM, 2, dtype=F32) / AXIS_DIM)
    row = (t // GRID_W).astype(F32)[:, None] * inv
    col = (t % GRID_W).astype(F32)[:, None] * inv
    cos = jnp.concatenate([jnp.cos(row), jnp.cos(row), jnp.cos(col), jnp.cos(col)], axis=1)
    sin = jnp.concatenate([-jnp.sin(row), jnp.sin(row), -jnp.sin(col), jnp.sin(col)], axis=1)
    cos = jnp.concatenate([jnp.ones((n_ctx, HEAD_DIM), F32), cos], axis=0)
    sin = jnp.concatenate([jnp.zeros((n_ctx, HEAD_DIM), F32), sin], axis=0)
    return jnp.tile(cos, (1, A_HEADS)), jnp.tile(sin, (1, A_HEADS))


def kernel(x, c, ctx, c_ctx, w_mod, b_mod, norm_mix, norm_ffn, w_in, conv_w, attn_sink, na_rpb, lb_logits, group_norm, w_out, peer_wq, peer_k1, peer_k2, peer_u, peer_v, final_norm):
    b, n_lat, d = x.shape
    n_ctx = ctx.shape[1]
    depth = w_mod.shape[0]
    seq = n_ctx + n_lat
    assert n_ctx % ROW_TILE == 0 and n_lat % ROW_TILE == 0 and n_lat // GRID_W >= NB_ROWS
    assert n_lat >= WIN_BLOCK + 2 * WINDOW and w_in.shape[2] == IN_W
    n_ctx_tiles = n_ctx // ROW_TILE

    n_cond = b + 1
    pad = (-n_cond) % 8
    cvec = jnp.concatenate([c, c_ctx[None, :], jnp.zeros((pad, d), F32)], axis=0)
    mods = _modulation(cvec, w_mod, b_mod)[:, :n_cond].reshape(depth, n_cond, N_MOD, d)

    lb_soft = jax.nn.softmax(lb_logits.astype(F32), axis=0)
    lower_bounds = jnp.cumsum(lb_soft, axis=0) - lb_soft[0:1]
    cos_t, sin_t = _rope_tables(n_ctx, n_lat)

    xc = jnp.concatenate([ctx, x], axis=1)
    out = None
    for l in range(depth):
        last = l == depth - 1
        proj = _in_projection(xc, mods[l], norm_mix[l], w_in[l].astype(BF16), n_ctx_tiles)
        ya = _attention_a(proj, cos_t, sin_t, attn_sink[l], n_ctx, not last)
        yb = _attention_b(proj, _neighbourhood_bias(na_rpb[l]), n_ctx, not last)
        yc = _short_conv(proj, conv_w[l], n_ctx)
        od = _hgrn(proj, lower_bounds[l], n_ctx)
        tile0 = n_ctx_tiles if last else 0
        x_mid, h2, scores_t = _post_mixer(ya, yb, yc, od, proj, xc, mods[l], group_norm[l], w_out[l].astype(BF16),
                                          norm_ffn[l], peer_wq[l].astype(BF16), peer_k1[l].astype(BF16),
                                          peer_k2[l].astype(BF16), n_ctx_tiles, tile0)
        s_out = x_mid.shape[1]
        tokens = b * s_out
        idx_t, gate_t = _routing(scores_t)
        per_tok = PEER_HEADS * PEER_TOPK
        idx = idx_t.reshape(per_tok, tokens).T.reshape(tokens * per_tok)
        gate = gate_t.reshape(per_tok, tokens).T
        uv = _pack_experts(peer_u[l], peer_v[l])
        res = _peer_experts(h2.reshape(tokens, *FOLD), idx, gate, uv, x_mid.reshape(tokens, *FOLD), mods[l],
                            final_norm, s_out, 0 if last else n_ctx, last)
        if last:
            out = res.reshape(b, s_out, d)
        else:
            xc = res.reshape(b, s_out, d)
    return out
```

```python
import functools
import math

import jax
import jax.numpy as jnp
import numpy as np
from jax import lax
from jax.experimental import pallas as pl
from jax.experimental.pallas import tpu as pltpu

F32 = jnp.float32
BF16 = jnp.bfloat16

GRID_W = 64
HEAD_DIM = 64
GROUP_W = 256
A_HEADS = 4
A_KV_HEADS = 2
WINDOW = 128
WIN_BLOCK = 128
B_HEADS = 4
NB_ROWS = 8
NB_COLS = 16
CONV_W = 3
D_HEADS = 4
SCAN_CHUNK = 64
LB_FLOOR = 1e-20
ROPE_THETA = 10000.0
AXIS_DIM = HEAD_DIM // 2
PEER_HEADS = 8
PEER_NKEYS = 128
PEER_TOPK = 16
N_MOD = 6
EPS = 1e-6
MASK_VALUE = -1e30
IN_W = 13 * GROUP_W

ROW_TILE = 256
MOD_COL_TILE = 1536
TOPK_TOKENS = 128
PEER_TOKENS = 8
VMEM_LIMIT = 48 * 1024 * 1024


def _cparams(n_axes):
    return pltpu.CompilerParams(dimension_semantics=("arbitrary",) * n_axes, vmem_limit_bytes=VMEM_LIMIT)


def _dot(a, b):
    return jnp.dot(a, b, preferred_element_type=F32)


def _dot_nt(a, b):
    return lax.dot_general(a, b, (((1,), (1,)), ((), ())), preferred_element_type=F32)


def _dot_tn(a, b):
    return lax.dot_general(a, b, (((0,), (0,)), ((), ())), preferred_element_type=F32)


def _rms(x):
    return x * lax.rsqrt(jnp.mean(x * x, axis=-1, keepdims=True) + EPS)


def _silu(x):
    return x / (1.0 + jnp.exp(-x))


def _mod_kernel(c_ref, w_ref, b_ref, o_ref):
    s = _silu(c_ref[...])
    o_ref[0] = _dot(s.astype(BF16), w_ref[0].astype(BF16)) + b_ref[0]


def _modulation(cvec, w_mod, b_mod):
    depth, d, n = w_mod.shape
    r = cvec.shape[0]
    tn = MOD_COL_TILE
    return pl.pallas_call(
        _mod_kernel,
        grid=(depth, n // tn),
        in_specs=[
            pl.BlockSpec((r, d), lambda l, j: (0, 0)),
            pl.BlockSpec((1, d, tn), lambda l, j: (l, 0, j)),
            pl.BlockSpec((1, 1, tn), lambda l, j: (l, 0, j)),
        ],
        out_specs=pl.BlockSpec((1, r, tn), lambda l, j: (l, 0, j)),
        out_shape=jax.ShapeDtypeStruct((depth, r, n), F32),
        compiler_params=_cparams(2),
        name="adaln_modulation",
    )(cvec, w_mod, b_mod.reshape(depth, 1, n))


def _inproj_kernel(x_ref, mod_ref, g_ref, w_ref, o_ref):
    y = _rms(x_ref[0]) * g_ref[...]
    h = y * (1.0 + mod_ref[0, 1:2, :]) + mod_ref[0, 0:1, :]
    o_ref[0] = _dot(h.astype(BF16), w_ref[...])


def _in_projection(xc, mods, gain, w_bf16, n_ctx_tiles):
    b, s, d = xc.shape
    n = w_bf16.shape[1]
    tm = ROW_TILE
    ctx_cond = mods.shape[0] - 1
    return pl.pallas_call(
        _inproj_kernel,
        grid=(b, s // tm),
        in_specs=[
            pl.BlockSpec((1, tm, d), lambda i, j: (i, j, 0)),
            pl.BlockSpec((1, N_MOD, d), lambda i, j: (jnp.where(j < n_ctx_tiles, ctx_cond, i), 0, 0)),
            pl.BlockSpec((1, d), lambda i, j: (0, 0)),
            pl.BlockSpec((d, n), lambda i, j: (0, 0)),
        ],
        out_specs=pl.BlockSpec((1, tm, n), lambda i, j: (i, j, 0)),
        out_shape=jax.ShapeDtypeStruct((b, s, n), F32),
        compiler_params=_cparams(2),
        name="in_projection",
    )(xc, mods, gain.reshape(1, d), w_bf16)


def _rope(x, cos, sin_signed):
    n = x.shape[1]
    lane = lax.broadcasted_iota(jnp.int32, x.shape, 1)
    fwd = pltpu.roll(x, n - AXIS_DIM // 2, 1)
    bwd = pltpu.roll(x, AXIS_DIM // 2, 1)
    swapped = jnp.where((lane % AXIS_DIM) < AXIS_DIM // 2, fwd, bwd)
    return x * cos + swapped * sin_signed


def _attn_a_kernel(sink_ref, q_ref, k_ref, v_ref, cq_ref, sq_ref, ck_ref, sk_ref, o_ref, kh_ref, vh_ref,
                   *, n_ctx, n_lat, ctx_steps):
    t = pl.program_id(1)
    span = WIN_BLOCK + 2 * WINDOW
    g = A_HEADS // A_KV_HEADS

    @pl.when(t == 0)
    def _():
        kr = _rope(k_ref[0], ck_ref[...], sk_ref[...])
        v = v_ref[0]
        for h in range(A_KV_HEADS):
            kh_ref[h] = kr[:, h * HEAD_DIM:(h + 1) * HEAD_DIM].astype(BF16)
            vh_ref[h] = v[:, h * HEAD_DIM:(h + 1) * HEAD_DIM].astype(BF16)

    q = _rope(q_ref[0], cq_ref[...], sq_ref[...]) * (HEAD_DIM ** -0.5)
    row = lax.broadcasted_iota(jnp.int32, (g * WIN_BLOCK, 1), 0)

    def q_pair(hk):
        parts = [q[:, (hk * g + j) * HEAD_DIM:(hk * g + j + 1) * HEAD_DIM] for j in range(g)]
        return jnp.concatenate(parts, axis=0).astype(BF16)

    def sink_col(hk):
        col = jnp.full((g * WIN_BLOCK, 1), sink_ref[hk * g], F32)
        for j in range(1, g):
            col = jnp.where(row >= j * WIN_BLOCK, sink_ref[hk * g + j], col)
        return col

    def assemble(outs):
        pieces = []
        for hk in range(A_KV_HEADS):
            for j in range(g):
                pieces.append(outs[hk][j * WIN_BLOCK:(j + 1) * WIN_BLOCK])
        return jnp.concatenate(pieces, axis=1)

    def latent_block():
        p0 = (t - ctx_steps) * WIN_BLOCK
        ws = jnp.clip(p0 - WINDOW, 0, n_lat - span)
        kstart = pl.multiple_of(n_ctx + ws, WIN_BLOCK)
        qpos = p0 + row % WIN_BLOCK
        kpos = ws + lax.broadcasted_iota(jnp.int32, (1, span), 1)
        mask = jnp.abs(qpos - kpos) <= WINDOW
        outs = []
        for hk in range(A_KV_HEADS):
            q2 = q_pair(hk)
            s_loc = _dot_nt(q2, kh_ref[hk, pl.ds(kstart, span), :])
            s_loc = jnp.where(mask, s_loc, MASK_VALUE)
            s_ctx = _dot_nt(q2, kh_ref[hk, 0:n_ctx, :])
            sk = sink_col(hk)
            m = jnp.maximum(jnp.maximum(jnp.max(s_loc, axis=1, keepdims=True),
                                        jnp.max(s_ctx, axis=1, keepdims=True)), sk)
            p_loc = jnp.exp(s_loc - m)
            p_ctx = jnp.exp(s_ctx - m)
            den = (jnp.sum(p_loc, axis=1, keepdims=True) + jnp.sum(p_ctx, axis=1, keepdims=True)
                   + jnp.exp(sk - m))
            o = (_dot(p_loc.astype(BF16), vh_ref[hk, pl.ds(kstart, span), :])
                 + _dot(p_ctx.astype(BF16), vh_ref[hk, 0:n_ctx, :]))
            outs.append(o / den)
        o_ref[0] = assemble(outs)

    def ctx_block():
        outs = []
        for hk in range(A_KV_HEADS):
            q2 = q_pair(hk)
            s_ctx = _dot_nt(q2, kh_ref[hk, 0:n_ctx, :])
            sk = sink_col(hk)
            m = jnp.maximum(jnp.max(s_ctx, axis=1, keepdims=True), sk)
            p_ctx = jnp.exp(s_ctx - m)
            den = jnp.sum(p_ctx, axis=1, keepdims=True) + jnp.exp(sk - m)
            outs.append(_dot(p_ctx.astype(BF16), vh_ref[hk, 0:n_ctx, :]) / den)
        o_ref[0] = assemble(outs)

    if ctx_steps:
        pl.when(t < ctx_steps)(ctx_block)
        pl.when(t >= ctx_steps)(latent_block)
    else:
        latent_block()


def _attention_a(proj, cos_t, sin_t, sink, n_ctx, with_ctx):
    b, s, _ = proj.shape
    n_lat = s - n_ctx
    ctx_blocks = n_ctx // WIN_BLOCK
    ctx_steps = ctx_blocks if with_ctx else 0
    off = ctx_blocks - ctx_steps
    steps = ctx_steps + n_lat // WIN_BLOCK
    kw = A_KV_HEADS * HEAD_DIM
    kernel = functools.partial(_attn_a_kernel, n_ctx=n_ctx, n_lat=n_lat, ctx_steps=ctx_steps)
    return pl.pallas_call(
        kernel,
        grid=(b, steps),
        in_specs=[
            pl.BlockSpec(memory_space=pltpu.SMEM),
            pl.BlockSpec((1, WIN_BLOCK, GROUP_W), lambda i, t: (i, t + off, 0)),
            pl.BlockSpec((1, s, kw), lambda i, t: (i, 0, 2)),
            pl.BlockSpec((1, s, kw), lambda i, t: (i, 0, 3)),
            pl.BlockSpec((WIN_BLOCK, GROUP_W), lambda i, t: (t + off, 0)),
            pl.BlockSpec((WIN_BLOCK, GROUP_W), lambda i, t: (t + off, 0)),
            pl.BlockSpec((s, kw), lambda i, t: (0, 0)),
            pl.BlockSpec((s, kw), lambda i, t: (0, 0)),
        ],
        out_specs=pl.BlockSpec((1, WIN_BLOCK, GROUP_W), lambda i, t: (i, t, 0)),
        out_shape=jax.ShapeDtypeStruct((b, steps * WIN_BLOCK, GROUP_W), F32),
        scratch_shapes=[pltpu.VMEM((A_KV_HEADS, s, HEAD_DIM), BF16), pltpu.VMEM((A_KV_HEADS, s, HEAD_DIM), BF16)],
        compiler_params=_cparams(2),
        name="window_attention",
    )(sink, proj, proj, proj, cos_t, sin_t, cos_t, sin_t)


def _attn_b_kernel(q_ref, k_ref, v_ref, bias_ref, o_ref, kh_ref, vh_ref, *, n_ctx, n_lat, ctx_steps):
    t = pl.program_id(1)
    rows = n_lat // GRID_W
    nkeys = NB_ROWS * GRID_W

    @pl.when(t == 0)
    def _():
        k = k_ref[0]
        v = v_ref[0]
        for h in range(B_HEADS):
            kh_ref[h] = k[:, h * HEAD_DIM:(h + 1) * HEAD_DIM].astype(BF16)
            vh_ref[h] = v[:, h * HEAD_DIM:(h + 1) * HEAD_DIM].astype(BF16)

    q = q_ref[0] * (HEAD_DIM ** -0.5)

    def latent_block():
        r = t - ctx_steps
        r0 = jnp.clip(r - NB_ROWS // 2, 0, rows - NB_ROWS)
        kstart = pl.multiple_of(n_ctx + r0 * GRID_W, GRID_W)
        outs = []
        for h in range(B_HEADS):
            qh = q[:, h * HEAD_DIM:(h + 1) * HEAD_DIM].astype(BF16)
            s_loc = _dot_nt(qh, kh_ref[h, pl.ds(kstart, nkeys), :]) + bias_ref[0, h]
            s_ctx = _dot_nt(qh, kh_ref[h, 0:n_ctx, :])
            m = jnp.maximum(jnp.max(s_loc, axis=1, keepdims=True), jnp.max(s_ctx, axis=1, keepdims=True))
            p_loc = jnp.exp(s_loc - m)
            p_ctx = jnp.exp(s_ctx - m)
            den = jnp.sum(p_loc, axis=1, keepdims=True) + jnp.sum(p_ctx, axis=1, keepdims=True)
            o = (_dot(p_loc.astype(BF16), vh_ref[h, pl.ds(kstart, nkeys), :])
                 + _dot(p_ctx.astype(BF16), vh_ref[h, 0:n_ctx, :]))
            outs.append(o / den)
        o_ref[0] = jnp.concatenate(outs, axis=1)

    def ctx_block():
        outs = []
        for h in range(B_HEADS):
            qh = q[:, h * HEAD_DIM:(h + 1) * HEAD_DIM].astype(BF16)
            s_ctx = _dot_nt(qh, kh_ref[h, 0:n_ctx, :])
            m = jnp.max(s_ctx, axis=1, keepdims=True)
            p_ctx = jnp.exp(s_ctx - m)
            den = jnp.sum(p_ctx, axis=1, keepdims=True)
            outs.append(_dot(p_ctx.astype(BF16), vh_ref[h, 0:n_ctx, :]) / den)
        o_ref[0] = jnp.concatenate(outs, axis=1)

    if ctx_steps:
        pl.when(t < ctx_steps)(ctx_block)
        pl.when(t >= ctx_steps)(latent_block)
    else:
        latent_block()


def _attention_b(proj, bias, n_ctx, with_ctx):
    b, s, _ = proj.shape
    n_lat = s - n_ctx
    rows = n_lat // GRID_W
    ctx_blocks = n_ctx // GRID_W
    ctx_steps = ctx_blocks if with_ctx else 0
    off = ctx_blocks - ctx_steps
    steps = ctx_steps + rows

    def bias_index(i, t):
        r = jnp.maximum(t - ctx_steps, 0)
        r0 = jnp.clip(r - NB_ROWS // 2, 0, rows - NB_ROWS)
        return (r - r0, 0, 0, 0)

    kernel = functools.partial(_attn_b_kernel, n_ctx=n_ctx, n_lat=n_lat, ctx_steps=ctx_steps)
    return pl.pallas_call(
        kernel,
        grid=(b, steps),
        in_specs=[
            pl.BlockSpec((1, GRID_W, GROUP_W), lambda i, t: (i, t + off, 2)),
            pl.BlockSpec((1, s, GROUP_W), lambda i, t: (i, 0, 3)),
            pl.BlockSpec((1, s, GROUP_W), lambda i, t: (i, 0, 4)),
            pl.BlockSpec((1, B_HEADS, GRID_W, NB_ROWS * GRID_W), bias_index),
        ],
        out_specs=pl.BlockSpec((1, GRID_W, GROUP_W), lambda i, t: (i, t, 0)),
        out_shape=jax.ShapeDtypeStruct((b, steps * GRID_W, GROUP_W), F32),
        scratch_shapes=[pltpu.VMEM((B_HEADS, s, HEAD_DIM), BF16), pltpu.VMEM((B_HEADS, s, HEAD_DIM), BF16)],
        compiler_params=_cparams(2),
        name="neighbourhood_attention",
    )(proj, proj, proj, bias)


def _neighbourhood_bias(rpb):
    qc = np.arange(GRID_W)[:, None]
    kc = np.arange(GRID_W)[None, :]
    win0 = np.clip(qc - NB_COLS // 2, 0, GRID_W - NB_COLS)
    ok = (kc >= win0) & (kc < win0 + NB_COLS)
    d_col = np.clip(kc - qc, -(NB_COLS - 1), NB_COLS - 1) + (NB_COLS - 1)
    onehot = (d_col[None] == np.arange(2 * NB_COLS - 1)[:, None, None]).astype(np.float32)
    by_rel = jnp.stack([rpb[:, NB_ROWS - 1 - rel:2 * NB_ROWS - 1 - rel, :] for rel in range(NB_ROWS)], axis=0)
    tab = jnp.einsum('rhkc,cqj->rhqkj', by_rel.astype(F32), jnp.asarray(onehot), precision=lax.Precision.HIGHEST)
    tab = jnp.where(ok[None, None, :, None, :], tab, MASK_VALUE)
    return tab.reshape(NB_ROWS, B_HEADS, GRID_W, NB_ROWS * GRID_W)


def _conv_kernel(cx_ref, cb_ref, cc_ref, w_ref, o_ref, pad_ref, *, n_ctx, seq):
    edge = 8
    chunk = ROW_TILE
    pad_ref[0:edge, :] = jnp.zeros((edge, GROUP_W), F32)
    pad_ref[edge + seq:2 * edge + seq, :] = jnp.zeros((edge, GROUP_W), F32)
    for c in range(seq // chunk):
        sl = slice(c * chunk, (c + 1) * chunk)
        pad_ref[edge + c * chunk:edge + (c + 1) * chunk, :] = cc_ref[0, sl, :] * cx_ref[0, sl, :]
    w = w_ref[...]
    for c in range(seq // chunk):
        lo = edge + c * chunk
        row = c * chunk + lax.broadcasted_iota(jnp.int32, (chunk, 1), 0)
        prev = pad_ref[lo - 1:lo - 1 + chunk, :]
        cur = pad_ref[lo:lo + chunk, :]
        nxt = pad_ref[lo + 1:lo + 1 + chunk, :]
        prev = jnp.where((row == 0) | (row == n_ctx), 0.0, prev)
        nxt = jnp.where((row == n_ctx - 1) | (row == seq - 1), 0.0, nxt)
        conv = w[0:1] * prev + w[1:2] * cur + w[2:3] * nxt
        o_ref[0, c * chunk:(c + 1) * chunk, :] = cb_ref[0, c * chunk:(c + 1) * chunk, :] * conv


def _short_conv(proj, conv_w, n_ctx):
    b, s, _ = proj.shape
    kernel = functools.partial(_conv_kernel, n_ctx=n_ctx, seq=s)
    return pl.pallas_call(
        kernel,
        grid=(b,),
        in_specs=[
            pl.BlockSpec((1, s, GROUP_W), lambda i: (i, 0, 5)),
            pl.BlockSpec((1, s, GROUP_W), lambda i: (i, 0, 6)),
            pl.BlockSpec((1, s, GROUP_W), lambda i: (i, 0, 7)),
            pl.BlockSpec((CONV_W, GROUP_W), lambda i: (0, 0)),
        ],
        out_specs=pl.BlockSpec((1, s, GROUP_W), lambda i: (i, 0, 0)),
        out_shape=jax.ShapeDtypeStruct((b, s, GROUP_W), F32),
        scratch_shapes=[pltpu.VMEM((s + 16, GROUP_W), F32)],
        compiler_params=_cparams(1),
        name="gated_short_conv",
    )(proj, proj, proj, conv_w)


OFFSET_GROUP = 16


def _hgrn_chunk(d, base, q_ref, z_ref, i_ref, gate_consts, o_ref, st_ref, kst, fst, vst, sst, pbuf, ones_bd, bd_mask):
    ch = SCAN_CHUNK
    sgn = 1 if d == 0 else -1
    lb_floor, one_minus_lb, floor_excess = gate_consts
    rows = pl.ds(base, ch)
    z = z_ref[0, rows, :]
    q = q_ref[0, rows, :]
    v = i_ref[0, rows, :]
    sig = 1.0 / (1.0 + jnp.exp(-z))
    f = lb_floor + one_minus_lb * sig
    k = one_minus_lb * (1.0 - sig) - floor_excess
    kst[d, ch:2 * ch, :] = k
    fst[d, ch:2 * ch, :] = f
    vst[d, ch:2 * ch, :] = v

    def shifted(ref, o):
        return ref[d, ch - sgn * o:2 * ch - sgn * o, :]

    inc = f
    exc = f
    step = 1
    while step < ch:
        sst[d, ch:2 * ch, :] = inc
        inc = inc * shifted(sst, step)
        step *= 2
    step = 1
    while step < ch:
        sst[d, ch:2 * ch, :] = exc
        exc = exc * shifted(sst, -step)
        step *= 2
    sst[d, ch:2 * ch, :] = exc
    exc = shifted(sst, -1)
    total = inc[ch - 1:ch, :] if d == 0 else inc[0:1, :]

    decay = jnp.ones((ch, GROUP_W), F32)
    acc = jnp.zeros((ch, GROUP_W), F32)
    for g0 in range(0, ch, OFFSET_GROUP):
        lo, hi = (g0, ch) if d == 0 else (0, ch - g0)
        n = hi - lo
        if g0 > 0:
            decay = decay[OFFSET_GROUP:] if d == 0 else decay[:n]
        qs = q[lo:hi]

        def window(ref, o):
            return ref[d, ch - sgn * o + lo:ch - sgn * o + hi, :]

        for oo in range(OFFSET_GROUP):
            o = g0 + oo
            if o > 0:
                decay = decay * window(fst, o - 1)
            pbuf[d, oo * n:(oo + 1) * n, :] = (qs * window(kst, o) * decay).astype(BF16)
        head_sums = _dot(pbuf[d, 0:OFFSET_GROUP * n, :], ones_bd)
        part = jnp.zeros((n, GROUP_W), F32)
        for oo in range(OFFSET_GROUP):
            part = part + head_sums[oo * n:(oo + 1) * n, :] * window(vst, g0 + oo)
        pieces = [part]
        if lo:
            pieces.insert(0, jnp.zeros((lo, GROUP_W), F32))
        if ch - hi:
            pieces.append(jnp.zeros((ch - hi, GROUP_W), F32))
        acc = acc + jnp.concatenate(pieces, axis=0)

    st = st_ref[d]
    o_inter = _dot_nt((q * inc).astype(BF16), st.astype(BF16))
    upd = _dot_tn(v.astype(BF16), (k * exc).astype(BF16))
    st_ref[d] = st * total + upd * bd_mask
    o_ref[0, rows, :] = o_ref[0, rows, :] + acc + o_inter


def _hgrn_kernel(q_ref, zf_ref, zb_ref, i_ref, lb_ref, o_ref, st_ref, kst, fst, vst, sst, pbuf, *, n_ctx, seq):
    ch = SCAN_CHUNK
    n_chunks = seq // ch
    ctx_chunks = n_ctx // ch
    lb = lb_ref[...]
    lb_floor = jnp.maximum(lb, LB_FLOOR)
    gate_consts = (lb_floor, 1.0 - lb, lb_floor - lb)
    hi = lax.broadcasted_iota(jnp.int32, (GROUP_W, GROUP_W), 0) // HEAD_DIM
    hj = lax.broadcasted_iota(jnp.int32, (GROUP_W, GROUP_W), 1) // HEAD_DIM
    bd_mask = jnp.where(hi == hj, 1.0, 0.0).astype(F32)
    ones_bd = bd_mask.astype(BF16)

    o_ref[...] = jnp.zeros(o_ref.shape, F32)
    st_ref[...] = jnp.zeros(st_ref.shape, F32)
    kst[...] = jnp.zeros(kst.shape, F32)
    fst[...] = jnp.zeros(fst.shape, F32)
    vst[...] = jnp.zeros(vst.shape, F32)
    sst[...] = jnp.ones(sst.shape, F32)

    def body(s, carry):
        cf = s
        cb = jnp.where(s < ctx_chunks, ctx_chunks - 1 - s, n_chunks - 1 - (s - ctx_chunks))
        for d, c, z_ref in ((0, cf, zf_ref), (1, cb, zb_ref)):
            base = pl.multiple_of(c * ch, ch)
            _hgrn_chunk(d, base, q_ref, z_ref, i_ref, gate_consts, o_ref, st_ref, kst, fst, vst, sst, pbuf,
                        ones_bd, bd_mask)
        return carry

    lax.fori_loop(0, n_chunks, body, 0)


def _hgrn(proj, lower_bound, n_ctx):
    b, s, _ = proj.shape
    ch = SCAN_CHUNK
    kernel = functools.partial(_hgrn_kernel, n_ctx=n_ctx, seq=s)
    col = lambda c: pl.BlockSpec((1, s, GROUP_W), lambda i: (i, 0, c))
    return pl.pallas_call(
        kernel,
        grid=(b,),
        in_specs=[col(8), col(9), col(10), col(11), pl.BlockSpec((1, GROUP_W), lambda i: (0, 0))],
        out_specs=pl.BlockSpec((1, s, GROUP_W), lambda i: (i, 0, 0)),
        out_shape=jax.ShapeDtypeStruct((b, s, GROUP_W), F32),
        scratch_shapes=[
            pltpu.VMEM((2, GROUP_W, GROUP_W), F32),
            pltpu.VMEM((2, 3 * ch, GROUP_W), F32),
            pltpu.VMEM((2, 3 * ch, GROUP_W), F32),
            pltpu.VMEM((2, 3 * ch, GROUP_W), F32),
            pltpu.VMEM((2, 3 * ch, GROUP_W), F32),
            pltpu.VMEM((2, OFFSET_GROUP * ch, GROUP_W), BF16),
        ],
        compiler_params=_cparams(1),
        name="hgrn2_bidirectional",
    )(proj, proj, proj, proj, lower_bound.reshape(1, GROUP_W))


def _post_kernel(ya_ref, yb_ref, yc_ref, od_ref, dg_ref, x_ref, mod_ref, gn_ref, wo_ref, nf_ref, wq_ref,
                 k1_ref, k2_ref, xo_ref, h_ref, sc_ref):
    yd = od_ref[0] * _silu(dg_ref[0])
    gn = gn_ref[...]
    parts = (ya_ref[0], yb_ref[0], yc_ref[0], yd)
    normed = [_rms(p) * gn[:, i * GROUP_W:(i + 1) * GROUP_W] for i, p in enumerate(parts)]
    y = _dot(jnp.concatenate(normed, axis=1).astype(BF16), wo_ref[...])
    xn = x_ref[0] + mod_ref[0, 2:3, :] * y
    xo_ref[0] = xn
    h = _rms(xn) * nf_ref[...] * (1.0 + mod_ref[0, 4:5, :]) + mod_ref[0, 3:4, :]
    h_ref[0] = h
    qv = _dot(h.astype(BF16), wq_ref[...])
    half = PEER_NKEYS
    for g in range(2 * PEER_HEADS):
        keys = k1_ref[...] if g % 2 == 0 else k2_ref[...]
        sc_ref[g] = _dot_nt(keys, qv[:, g * half:(g + 1) * half].astype(BF16))


def _post_mixer(ya, yb, yc, od, proj, xc, mods, gnorm, wo_bf16, nffn, wq_bf16, k1_bf16, k2_bf16, n_ctx_tiles, tile0):
    b, s, d = xc.shape
    tm = ROW_TILE
    nt = s // tm - tile0
    s_out = nt * tm
    ctx_cond = mods.shape[0] - 1
    nq = wq_bf16.shape[1]
    att0 = tile0 - (s - ya.shape[1]) // tm
    att = lambda: pl.BlockSpec((1, tm, GROUP_W), lambda i, j: (i, j + att0, 0))
    grp = lambda: pl.BlockSpec((1, tm, GROUP_W), lambda i, j: (i, j + tile0, 0))
    const = lambda shape: pl.BlockSpec(shape, lambda i, j: (0,) * len(shape))
    return pl.pallas_call(
        _post_kernel,
        grid=(b, nt),
        in_specs=[
            att(), att(), grp(), grp(),
            pl.BlockSpec((1, tm, GROUP_W), lambda i, j: (i, j + tile0, 12)),
            pl.BlockSpec((1, tm, d), lambda i, j: (i, j + tile0, 0)),
            pl.BlockSpec((1, N_MOD, d), lambda i, j: (jnp.where(j + tile0 < n_ctx_tiles, ctx_cond, i), 0, 0)),
            const((1, d)), const((d, d)), const((1, d)), const((d, nq)),
            const((PEER_NKEYS, PEER_NKEYS)), const((PEER_NKEYS, PEER_NKEYS)),
        ],
        out_specs=[
            pl.BlockSpec((1, tm, d), lambda i, j: (i, j, 0)),
            pl.BlockSpec((1, tm, d), lambda i, j: (i, j, 0)),
            pl.BlockSpec((2 * PEER_HEADS, PEER_NKEYS, tm), lambda i, j: (0, 0, i * nt + j)),
        ],
        out_shape=[
            jax.ShapeDtypeStruct((b, s_out, d), F32),
            jax.ShapeDtypeStruct((b, s_out, d), F32),
            jax.ShapeDtypeStruct((2 * PEER_HEADS, PEER_NKEYS, b * s_out), F32),
        ],
        compiler_params=_cparams(2),
        name="post_mixer",
    )(ya, yb, yc, od, proj, xc, mods, gnorm.reshape(1, d), wo_bf16, nffn.reshape(1, d), wq_bf16, k1_bf16, k2_bf16)


def _top16(s, rows=None):
    if rows is None:
        rows = lax.broadcasted_iota(jnp.int32, s.shape, 0).astype(F32)
    vals, ids = [], []
    cur = s
    for _ in range(PEER_TOPK):
        m = jnp.max(cur, axis=0, keepdims=True)
        am = jnp.min(jnp.where(cur == m, rows, jnp.inf), axis=0, keepdims=True)
        vals.append(m)
        ids.append(am)
        cur = jnp.where(rows == am, -jnp.inf, cur)
    return jnp.concatenate(vals, axis=0), jnp.concatenate(ids, axis=0)


def _candidate_rows(a):
    return PEER_TOPK // (a + 1)


def _pick(table, sel):
    out = jnp.zeros(sel.shape, table.dtype)
    for a in range(PEER_TOPK):
        out = out + jnp.where(sel == float(a), table[a:a + 1, :], 0.0)
    return out


def _route_kernel(sc_ref, idx_ref, gate_ref):
    for h in range(PEER_HEADS):
        v1, i1 = _top16(sc_ref[2 * h])
        v2, i2 = _top16(sc_ref[2 * h + 1])
        nb = [_candidate_rows(a) for a in range(PEER_TOPK)]
        cand = jnp.concatenate([v1[a:a + 1, :] + v2[0:nb[a], :] for a in range(PEER_TOPK)], axis=0)
        flat = jnp.concatenate([lax.broadcasted_iota(jnp.int32, (nb[a], v1.shape[1]), 0).astype(F32)
                                + float(a * PEER_TOPK) for a in range(PEER_TOPK)], axis=0)
        top_s, pos = _top16(cand, flat)
        a_sel = jnp.floor(pos * (1.0 / PEER_TOPK))
        b_sel = pos - a_sel * PEER_TOPK
        idx_ref[h] = (_pick(i1, a_sel) * PEER_NKEYS + _pick(i2, b_sel)).astype(jnp.int32)
        e = jnp.exp(top_s - jnp.max(top_s, axis=0, keepdims=True))
        gate_ref[h] = e / jnp.sum(e, axis=0, keepdims=True)


def _routing(scores_t):
    g, nk, t = scores_t.shape
    tt = TOPK_TOKENS
    out_spec = pl.BlockSpec((PEER_HEADS, PEER_TOPK, tt), lambda i: (0, 0, i))
    return pl.pallas_call(
        _route_kernel,
        grid=(t // tt,),
        in_specs=[pl.BlockSpec((g, nk, tt), lambda i: (0, 0, i))],
        out_specs=[out_spec, out_spec],
        out_shape=[jax.ShapeDtypeStruct((PEER_HEADS, PEER_TOPK, t), jnp.int32),
                   jax.ShapeDtypeStruct((PEER_HEADS, PEER_TOPK, t), F32)],
        compiler_params=_cparams(1),
        name="peer_routing",
    )(scores_t)


def _gelu_tanh(x):
    return 0.5 * x * (1.0 + jnp.tanh(math.sqrt(2.0 / math.pi) * (x + 0.044715 * (x * x * x))))


FOLD = (8, 128)


def _sublane_pair_sum(x, y, k):
    sub = lax.broadcasted_iota(jnp.int32, FOLD, 0)
    keep = (sub % (2 * k)) < k
    return jnp.where(keep, x, pltpu.roll(y, k, 0)) + jnp.where(keep, pltpu.roll(x, FOLD[0] - k, 0), y)


def _sublane_sums(p):
    z = [_sublane_pair_sum(p[j], p[j + 4], 4) for j in range(4)]
    w = [_sublane_pair_sum(z[0], z[2], 2), _sublane_pair_sum(z[1], z[3], 2)]
    return _sublane_pair_sum(w[0], w[1], 1)


def _peer_kernel(idx_hbm, tbl_hbm, h_ref, gate_ref, x_ref, mod_ref, fn_ref, o_ref, idx_smem, tbl, act_rep,
                 tbl_sem, idx_sem, *, n_steps, part, final):
    j = pl.program_id(0)
    tb = PEER_TOKENS
    per_tok = PEER_HEADS * PEER_TOPK
    npair = tb * per_tok
    sub = FOLD[0]
    n_rows = tbl.shape[0] - 1

    def idx_copy(blk, half):
        return pltpu.make_async_copy(idx_hbm.at[pl.ds(blk * npair, npair)],
                                     idx_smem.at[pl.ds(half * npair, npair)], idx_sem.at[half])

    eye = (lax.broadcasted_iota(jnp.int32, (per_tok, per_tok), 0)
           == lax.broadcasted_iota(jnp.int32, (per_tok, per_tok), 1))
    g2 = mod_ref[0, 5]
    hi_mask = jnp.uint32(0xFFFF0000)

    def consume(half):
        for t in range(tb):
            tok = half * tb + t
            hf = h_ref[tok]
            rows = [idx_smem[half * npair + t * per_tok + e] for e in range(per_tok)]
            parts = []
            for g in range(per_tok // sub):
                prods = []
                for k in range(sub):
                    word = tbl[rows[g * sub + k]]
                    prods.append(pltpu.bitcast(word << 16, F32) * hf)
                parts.append(_sublane_sums(prods))
            s = jnp.sum(jnp.concatenate(parts, axis=0), axis=1, keepdims=True)
            gate_col = jnp.sum(jnp.where(eye, gate_ref[tok:tok + 1, :], 0.0), axis=1, keepdims=True)
            act_rep[half] = jnp.broadcast_to(_gelu_tanh(s) * gate_col, (per_tok, FOLD[1]))
            accs = [jnp.zeros(FOLD, F32) for _ in range(4)]
            for e in range(per_tok):
                a = jnp.broadcast_to(act_rep[half, e:e + 1, :], FOLD)
                v = pltpu.bitcast(tbl[rows[e]] & hi_mask, F32)
                accs[e % 4] = accs[e % 4] + a * v
            xn = x_ref[tok] + g2 * ((accs[0] + accs[1]) + (accs[2] + accs[3]))
            if final:
                ms = (jnp.sum(jnp.sum(xn * xn, axis=1, keepdims=True), axis=0, keepdims=True)
                      * (1.0 / (FOLD[0] * FOLD[1])))
                xn = xn * lax.rsqrt(ms + EPS) * fn_ref[...]
            o_ref[tok] = xn

    @pl.when(j == 0)
    def _():
        idx_copy(0, 0).start()
        idx_copy(1, 1).start()
        whole = pltpu.make_async_copy(tbl_hbm.at[pl.ds(part * n_rows, n_rows)], tbl.at[pl.ds(0, n_rows)], tbl_sem)
        whole.start()
        tbl[n_rows] = jnp.zeros(FOLD, jnp.uint32)
        whole.wait()

    more = j + 1 < n_steps
    idx_copy(2 * j, 0).wait()
    consume(0)

    @pl.when(more)
    def _():
        idx_copy(2 * j + 2, 0).start()

    idx_copy(2 * j + 1, 1).wait()
    consume(1)

    @pl.when(more)
    def _():
        idx_copy(2 * j + 3, 1).start()


def _peer_pass(h2, rows, gate, uv, x_in, mods, final_gain, s_out, ctx_rows, part, n_parts, final):
    t = h2.shape[0]
    tb = 2 * PEER_TOKENS
    per_tok = PEER_HEADS * PEER_TOPK
    n_steps = t // tb
    n_rows = uv.shape[0] // n_parts
    assert t % tb == 0 and s_out % tb == 0 and ctx_rows % tb == 0
    ctx_cond = mods.shape[0] - 1

    def mod_index(i):
        tok = i * tb
        return (jnp.where(tok % s_out < ctx_rows, ctx_cond, tok // s_out), 0, 0, 0)

    kernel = functools.partial(_peer_kernel, n_steps=n_steps, part=part, final=final)
    fold_spec = pl.BlockSpec((tb,) + FOLD, lambda i: (i, 0, 0))
    return pl.pallas_call(
        kernel,
        grid=(n_steps,),
        in_specs=[
            pl.BlockSpec(memory_space=pl.ANY),
            pl.BlockSpec(memory_space=pl.ANY),
            fold_spec,
            pl.BlockSpec((tb, per_tok), lambda i: (i, 0)),
            fold_spec,
            pl.BlockSpec((1, N_MOD) + FOLD, mod_index),
            pl.BlockSpec(FOLD, lambda i: (0, 0)),
        ],
        out_specs=fold_spec,
        out_shape=jax.ShapeDtypeStruct((t,) + FOLD, F32),
        scratch_shapes=[
            pltpu.SMEM((2 * PEER_TOKENS * per_tok,), jnp.int32),
            pltpu.VMEM((n_rows + 1,) + FOLD, jnp.uint32),
            pltpu.VMEM((2, per_tok, FOLD[1]), F32),
            pltpu.SemaphoreType.DMA,
            pltpu.SemaphoreType.DMA((2,)),
        ],
        compiler_params=_cparams(1),
        name="peer_experts",
    )(rows, uv, h2, gate, x_in, mods.reshape(mods.shape[0], N_MOD, *FOLD), final_gain.reshape(FOLD))


PEER_TABLE_PARTS = 2


def _peer_experts(h2, idx, gate, uv, x_mid, mods, final_gain, s_out, ctx_rows, final):
    n_rows = uv.shape[0] // PEER_TABLE_PARTS
    x_cur = x_mid
    for part in range(PEER_TABLE_PARTS):
        rows = jnp.where(idx // n_rows == part, idx % n_rows, n_rows)
        x_cur = _peer_pass(h2, rows, gate, uv, x_cur, mods, final_gain, s_out, ctx_rows, part, PEER_TABLE_PARTS,
                           final and part == PEER_TABLE_PARTS - 1)
    return x_cur


def _pack_experts(u, v):
    n_exp = u.shape[0]
    ub = lax.bitcast_convert_type(u.astype(BF16), jnp.uint16).astype(jnp.uint32)
    vb = lax.bitcast_convert_type(v.astype(BF16), jnp.uint16).astype(jnp.uint32)
    return ((vb << 16) | ub).reshape(n_exp, *FOLD)


def _rope_tables(n_ctx, n_lat):
    t = jnp.arange(n_lat)
    inv = ROPE_THETA ** (-jnp.arange(0, AXIS_DIM, 2, dtype=F32) / AXIS_DIM)
    row = (t // GRID_W).astype(F32)[:, None] * inv
    col = (t % GRID_W).astype(F32)[:, None] * inv
    cos = jnp.concatenate([jnp.cos(row), jnp.cos(row), jnp.cos(col), jnp.cos(col)], axis=1)
    sin = jnp.concatenate([-jnp.sin(row), jnp.sin(row), -jnp.sin(col), jnp.sin(col)], axis=1)
    cos = jnp.concatenate([jnp.ones((n_ctx, HEAD_DIM), F32), cos], axis=0)
    sin = jnp.concatenate([jnp.zeros((n_ctx, HEAD_DIM), F32), sin], axis=0)
    return jnp.tile(cos, (1, A_HEADS)), jnp.tile(sin, (1, A_HEADS))


def kernel(x, c, ctx, c_ctx, w_mod, b_mod, norm_mix, norm_ffn, w_in, conv_w, attn_sink, na_rpb, lb_logits, group_norm, w_out, peer_wq, peer_k1, peer_k2, peer_u, peer_v, final_norm):
    b, n_lat, d = x.shape
    n_ctx = ctx.shape[1]
    depth = w_mod.shape[0]
    seq = n_ctx + n_lat
    assert n_ctx % ROW_TILE == 0 and n_lat % ROW_TILE == 0 and n_lat // GRID_W >= NB_ROWS
    assert n_lat >= WIN_BLOCK + 2 * WINDOW and w_in.shape[2] == IN_W
    n_ctx_tiles = n_ctx // ROW_TILE

    n_cond = b + 1
    pad = (-n_cond) % 8
    cvec = jnp.concatenate([c, c_ctx[None, :], jnp.zeros((pad, d), F32)], axis=0)
    mods = _modulation(cvec, w_mod, b_mod)[:, :n_cond].reshape(depth, n_cond, N_MOD, d)

    lb_soft = jax.nn.softmax(lb_logits.astype(F32), axis=0)
    lower_bounds = jnp.cumsum(lb_soft, axis=0) - lb_soft[0:1]
    cos_t, sin_t = _rope_tables(n_ctx, n_lat)

    xc = jnp.concatenate([ctx, x], axis=1)
    out = None
    for l in range(depth):
        last = l == depth - 1
        proj = _in_projection(xc, mods[l], norm_mix[l], w_in[l].astype(BF16), n_ctx_tiles)
        ya = _attention_a(proj, cos_t, sin_t, attn_sink[l], n_ctx, not last)
        yb = _attention_b(proj, _neighbourhood_bias(na_rpb[l]), n_ctx, not last)
        yc = _short_conv(proj, conv_w[l], n_ctx)
        od = _hgrn(proj, lower_bounds[l], n_ctx)
        tile0 = n_ctx_tiles if last else 0
        x_mid, h2, scores_t = _post_mixer(ya, yb, yc, od, proj, xc, mods[l], group_norm[l], w_out[l].astype(BF16),
                                          norm_ffn[l], peer_wq[l].astype(BF16), peer_k1[l].astype(BF16),
                                          peer_k2[l].astype(BF16), n_ctx_tiles, tile0)
        s_out = x_mid.shape[1]
        tokens = b * s_out
        idx_t, gate_t = _routing(scores_t)
        per_tok = PEER_HEADS * PEER_TOPK
        idx = idx_t.reshape(per_tok, tokens).T.reshape(tokens * per_tok)
        gate = gate_t.reshape(per_tok, tokens).T
        uv = _pack_experts(peer_u[l], peer_v[l])
        res = _peer_experts(h2.reshape(tokens, *FOLD), idx, gate, uv, x_mid.reshape(tokens, *FOLD), mods[l],
                            final_norm, s_out, 0 if last else n_ctx, last)
        if last:
            out = res.reshape(b, s_out, d)
        else:
            xc = res.reshape(b, s_out, d)
    return out
```

```python
import functools
import math

import jax
import jax.numpy as jnp
import numpy as np
from jax import lax
from jax.experimental import pallas as pl
from jax.experimental.pallas import tpu as pltpu

F32 = jnp.float32
BF16 = jnp.bfloat16

GRID_W = 64
HEAD_DIM = 64
GROUP_W = 256
A_HEADS = 4
A_KV_HEADS = 2
WINDOW = 128
WIN_BLOCK = 128
B_HEADS = 4
NB_ROWS = 8
NB_COLS = 16
CONV_W = 3
D_HEADS = 4
SCAN_CHUNK = 64
LB_FLOOR = 1e-20
ROPE_THETA = 10000.0
AXIS_DIM = HEAD_DIM // 2
PEER_HEADS = 8
PEER_NKEYS = 128
PEER_TOPK = 16
N_MOD = 6
EPS = 1e-6
MASK_VALUE = -1e30
IN_W = 13 * GROUP_W

ROW_TILE = 256
MOD_COL_TILE = 1536
TOPK_TOKENS = 128
PEER_TOKENS = 8
VMEM_LIMIT = 48 * 1024 * 1024


PEER_VMEM_LIMIT = 60 * 1024 * 1024
PEER_RESIDENT_BYTES = 44 * 1024 * 1024


def _cparams(n_axes, vmem_limit=VMEM_LIMIT):
    return pltpu.CompilerParams(dimension_semantics=("arbitrary",) * n_axes, vmem_limit_bytes=vmem_limit)


def _dot(a, b):
    return jnp.dot(a, b, preferred_element_type=F32)


def _dot_nt(a, b):
    return lax.dot_general(a, b, (((1,), (1,)), ((), ())), preferred_element_type=F32)


def _dot_tn(a, b):
    return lax.dot_general(a, b, (((0,), (0,)), ((), ())), preferred_element_type=F32)


def _rms(x):
    return x * lax.rsqrt(jnp.mean(x * x, axis=-1, keepdims=True) + EPS)


def _silu(x):
    return x / (1.0 + jnp.exp(-x))


def _mod_kernel(c_ref, w_ref, b_ref, o_ref):
    s = _silu(c_ref[...])
    o_ref[0] = _dot(s.astype(BF16), w_ref[0].astype(BF16)) + b_ref[0]


def _modulation(cvec, w_mod, b_mod):
    depth, d, n = w_mod.shape
    r = cvec.shape[0]
    tn = MOD_COL_TILE
    return pl.pallas_call(
        _mod_kernel,
        grid=(depth, n // tn),
        in_specs=[
            pl.BlockSpec((r, d), lambda l, j: (0, 0)),
            pl.BlockSpec((1, d, tn), lambda l, j: (l, 0, j)),
            pl.BlockSpec((1, 1, tn), lambda l, j: (l, 0, j)),
        ],
        out_specs=pl.BlockSpec((1, r, tn), lambda l, j: (l, 0, j)),
        out_shape=jax.ShapeDtypeStruct((depth, r, n), F32),
        compiler_params=_cparams(2),
        name="adaln_modulation",
    )(cvec, w_mod, b_mod.reshape(depth, 1, n))


def _inproj_kernel(x_ref, mod_ref, g_ref, w_ref, o_ref):
    y = _rms(x_ref[0]) * g_ref[...]
    h = y * (1.0 + mod_ref[0, 1:2, :]) + mod_ref[0, 0:1, :]
    o_ref[0] = _dot(h.astype(BF16), w_ref[...])


def _in_projection(xc, mods, gain, w_bf16, n_ctx_tiles):
    b, s, d = xc.shape
    n = w_bf16.shape[1]
    tm = ROW_TILE
    ctx_cond = mods.shape[0] - 1
    return pl.pallas_call(
        _inproj_kernel,
        grid=(b, s // tm),
        in_specs=[
            pl.BlockSpec((1, tm, d), lambda i, j: (i, j, 0)),
            pl.BlockSpec((1, N_MOD, d), lambda i, j: (jnp.where(j < n_ctx_tiles, ctx_cond, i), 0, 0)),
            pl.BlockSpec((1, d), lambda i, j: (0, 0)),
            pl.BlockSpec((d, n), lambda i, j: (0, 0)),
        ],
        out_specs=pl.BlockSpec((1, tm, n), lambda i, j: (i, j, 0)),
        out_shape=jax.ShapeDtypeStruct((b, s, n), F32),
        compiler_params=_cparams(2),
        name="in_projection",
    )(xc, mods, gain.reshape(1, d), w_bf16)


def _rope(x, cos, sin_signed):
    n = x.shape[1]
    lane = lax.broadcasted_iota(jnp.int32, x.shape, 1)
    fwd = pltpu.roll(x, n - AXIS_DIM // 2, 1)
    bwd = pltpu.roll(x, AXIS_DIM // 2, 1)
    swapped = jnp.where((lane % AXIS_DIM) < AXIS_DIM // 2, fwd, bwd)
    return x * cos + swapped * sin_signed


def _attn_a_kernel(sink_ref, q_ref, k_ref, v_ref, cq_ref, sq_ref, ck_ref, sk_ref, o_ref, kh_ref, vh_ref,
                   *, n_ctx, n_lat, ctx_steps):
    t = pl.program_id(1)
    span = WIN_BLOCK + 2 * WINDOW
    g = A_HEADS // A_KV_HEADS

    @pl.when(t == 0)
    def _():
        kr = _rope(k_ref[0], ck_ref[...], sk_ref[...])
        v = v_ref[0]
        for h in range(A_KV_HEADS):
            kh_ref[h] = kr[:, h * HEAD_DIM:(h + 1) * HEAD_DIM].astype(BF16)
            vh_ref[h] = v[:, h * HEAD_DIM:(h + 1) * HEAD_DIM].astype(BF16)

    q = _rope(q_ref[0], cq_ref[...], sq_ref[...]) * (HEAD_DIM ** -0.5)
    row = lax.broadcasted_iota(jnp.int32, (g * WIN_BLOCK, 1), 0)

    def q_pair(hk):
        parts = [q[:, (hk * g + j) * HEAD_DIM:(hk * g + j + 1) * HEAD_DIM] for j in range(g)]
        return jnp.concatenate(parts, axis=0).astype(BF16)

    def sink_col(hk):
        col = jnp.full((g * WIN_BLOCK, 1), sink_ref[hk * g], F32)
        for j in range(1, g):
            col = jnp.where(row >= j * WIN_BLOCK, sink_ref[hk * g + j], col)
        return col

    def assemble(outs):
        pieces = []
        for hk in range(A_KV_HEADS):
            for j in range(g):
                pieces.append(outs[hk][j * WIN_BLOCK:(j + 1) * WIN_BLOCK])
        return jnp.concatenate(pieces, axis=1)

    def latent_block():
        p0 = (t - ctx_steps) * WIN_BLOCK
        ws = jnp.clip(p0 - WINDOW, 0, n_lat - span)
        kstart = pl.multiple_of(n_ctx + ws, WIN_BLOCK)
        qpos = p0 + row % WIN_BLOCK
        kpos = ws + lax.broadcasted_iota(jnp.int32, (1, span), 1)
        mask = jnp.abs(qpos - kpos) <= WINDOW
        outs = []
        for hk in range(A_KV_HEADS):
            q2 = q_pair(hk)
            s_loc = _dot_nt(q2, kh_ref[hk, pl.ds(kstart, span), :])
            s_loc = jnp.where(mask, s_loc, MASK_VALUE)
            s_ctx = _dot_nt(q2, kh_ref[hk, 0:n_ctx, :])
            sk = sink_col(hk)
            m = jnp.maximum(jnp.maximum(jnp.max(s_loc, axis=1, keepdims=True),
                                        jnp.max(s_ctx, axis=1, keepdims=True)), sk)
            p_loc = jnp.exp(s_loc - m)
            p_ctx = jnp.exp(s_ctx - m)
            den = (jnp.sum(p_loc, axis=1, keepdims=True) + jnp.sum(p_ctx, axis=1, keepdims=True)
                   + jnp.exp(sk - m))
            o = (_dot(p_loc.astype(BF16), vh_ref[hk, pl.ds(kstart, span), :])
                 + _dot(p_ctx.astype(BF16), vh_ref[hk, 0:n_ctx, :]))
            outs.append(o / den)
        o_ref[0] = assemble(outs)

    def ctx_block():
        outs = []
        for hk in range(A_KV_HEADS):
            q2 = q_pair(hk)
            s_ctx = _dot_nt(q2, kh_ref[hk, 0:n_ctx, :])
            sk = sink_col(hk)
            m = jnp.maximum(jnp.max(s_ctx, axis=1, keepdims=True), sk)
            p_ctx = jnp.exp(s_ctx - m)
            den = jnp.sum(p_ctx, axis=1, keepdims=True) + jnp.exp(sk - m)
            outs.append(_dot(p_ctx.astype(BF16), vh_ref[hk, 0:n_ctx, :]) / den)
        o_ref[0] = assemble(outs)

    if ctx_steps:
        pl.when(t < ctx_steps)(ctx_block)
        pl.when(t >= ctx_steps)(latent_block)
    else:
        latent_block()


def _attention_a(proj, cos_t, sin_t, sink, n_ctx, with_ctx):
    b, s, _ = proj.shape
    n_lat = s - n_ctx
    ctx_blocks = n_ctx // WIN_BLOCK
    ctx_steps = ctx_blocks if with_ctx else 0
    off = ctx_blocks - ctx_steps
    steps = ctx_steps + n_lat // WIN_BLOCK
    kw = A_KV_HEADS * HEAD_DIM
    kernel = functools.partial(_attn_a_kernel, n_ctx=n_ctx, n_lat=n_lat, ctx_steps=ctx_steps)
    return pl.pallas_call(
        kernel,
        grid=(b, steps),
        in_specs=[
            pl.BlockSpec(memory_space=pltpu.SMEM),
            pl.BlockSpec((1, WIN_BLOCK, GROUP_W), lambda i, t: (i, t + off, 0)),
            pl.BlockSpec((1, s, kw), lambda i, t: (i, 0, 2)),
            pl.BlockSpec((1, s, kw), lambda i, t: (i, 0, 3)),
            pl.BlockSpec((WIN_BLOCK, GROUP_W), lambda i, t: (t + off, 0)),
            pl.BlockSpec((WIN_BLOCK, GROUP_W), lambda i, t: (t + off, 0)),
            pl.BlockSpec((s, kw), lambda i, t: (0, 0)),
            pl.BlockSpec((s, kw), lambda i, t: (0, 0)),
        ],
        out_specs=pl.BlockSpec((1, WIN_BLOCK, GROUP_W), lambda i, t: (i, t, 0)),
        out_shape=jax.ShapeDtypeStruct((b, steps * WIN_BLOCK, GROUP_W), F32),
        scratch_shapes=[pltpu.VMEM((A_KV_HEADS, s, HEAD_DIM), BF16), pltpu.VMEM((A_KV_HEADS, s, HEAD_DIM), BF16)],
        compiler_params=_cparams(2),
        name="window_attention",
    )(sink, proj, proj, proj, cos_t, sin_t, cos_t, sin_t)


def _attn_b_kernel(q_ref, k_ref, v_ref, bias_ref, o_ref, kh_ref, vh_ref, *, n_ctx, n_lat, ctx_steps):
    t = pl.program_id(1)
    rows = n_lat // GRID_W
    nkeys = NB_ROWS * GRID_W

    @pl.when(t == 0)
    def _():
        k = k_ref[0]
        v = v_ref[0]
        for h in range(B_HEADS):
            kh_ref[h] = k[:, h * HEAD_DIM:(h + 1) * HEAD_DIM].astype(BF16)
            vh_ref[h] = v[:, h * HEAD_DIM:(h + 1) * HEAD_DIM].astype(BF16)

    q = q_ref[0] * (HEAD_DIM ** -0.5)

    def latent_block():
        r = t - ctx_steps
        r0 = jnp.clip(r - NB_ROWS // 2, 0, rows - NB_ROWS)
        kstart = pl.multiple_of(n_ctx + r0 * GRID_W, GRID_W)
        outs = []
        for h in range(B_HEADS):
            qh = q[:, h * HEAD_DIM:(h + 1) * HEAD_DIM].astype(BF16)
            s_loc = _dot_nt(qh, kh_ref[h, pl.ds(kstart, nkeys), :]) + bias_ref[0, h]
            s_ctx = _dot_nt(qh, kh_ref[h, 0:n_ctx, :])
            m = jnp.maximum(jnp.max(s_loc, axis=1, keepdims=True), jnp.max(s_ctx, axis=1, keepdims=True))
            p_loc = jnp.exp(s_loc - m)
            p_ctx = jnp.exp(s_ctx - m)
            den = jnp.sum(p_loc, axis=1, keepdims=True) + jnp.sum(p_ctx, axis=1, keepdims=True)
            o = (_dot(p_loc.astype(BF16), vh_ref[h, pl.ds(kstart, nkeys), :])
                 + _dot(p_ctx.astype(BF16), vh_ref[h, 0:n_ctx, :]))
            outs.append(o / den)
        o_ref[0] = jnp.concatenate(outs, axis=1)

    def ctx_block():
        outs = []
        for h in range(B_HEADS):
            qh = q[:, h * HEAD_DIM:(h + 1) * HEAD_DIM].astype(BF16)
            s_ctx = _dot_nt(qh, kh_ref[h, 0:n_ctx, :])
            m = jnp.max(s_ctx, axis=1, keepdims=True)
            p_ctx = jnp.exp(s_ctx - m)
            den = jnp.sum(p_ctx, axis=1, keepdims=True)
            outs.append(_dot(p_ctx.astype(BF16), vh_ref[h, 0:n_ctx, :]) / den)
        o_ref[0] = jnp.concatenate(outs, axis=1)

    if ctx_steps:
        pl.when(t < ctx_steps)(ctx_block)
        pl.when(t >= ctx_steps)(latent_block)
    else:
        latent_block()


def _attention_b(proj, bias, n_ctx, with_ctx):
    b, s, _ = proj.shape
    n_lat = s - n_ctx
    rows = n_lat // GRID_W
    ctx_blocks = n_ctx // GRID_W
    ctx_steps = ctx_blocks if with_ctx else 0
    off = ctx_blocks - ctx_steps
    steps = ctx_steps + rows

    def bias_index(i, t):
        r = jnp.maximum(t - ctx_steps, 0)
        r0 = jnp.clip(r - NB_ROWS // 2, 0, rows - NB_ROWS)
        return (r - r0, 0, 0, 0)

    kernel = functools.partial(_attn_b_kernel, n_ctx=n_ctx, n_lat=n_lat, ctx_steps=ctx_steps)
    return pl.pallas_call(
        kernel,
        grid=(b, steps),
        in_specs=[
            pl.BlockSpec((1, GRID_W, GROUP_W), lambda i, t: (i, t + off, 2)),
            pl.BlockSpec((1, s, GROUP_W), lambda i, t: (i, 0, 3)),
            pl.BlockSpec((1, s, GROUP_W), lambda i, t: (i, 0, 4)),
            pl.BlockSpec((1, B_HEADS, GRID_W, NB_ROWS * GRID_W), bias_index),
        ],
        out_specs=pl.BlockSpec((1, GRID_W, GROUP_W), lambda i, t: (i, t, 0)),
        out_shape=jax.ShapeDtypeStruct((b, steps * GRID_W, GROUP_W), F32),
        scratch_shapes=[pltpu.VMEM((B_HEADS, s, HEAD_DIM), BF16), pltpu.VMEM((B_HEADS, s, HEAD_DIM), BF16)],
        compiler_params=_cparams(2),
        name="neighbourhood_attention",
    )(proj, proj, proj, bias)


def _neighbourhood_bias(rpb):
    qc = np.arange(GRID_W)[:, None]
    kc = np.arange(GRID_W)[None, :]
    win0 = np.clip(qc - NB_COLS // 2, 0, GRID_W - NB_COLS)
    ok = (kc >= win0) & (kc < win0 + NB_COLS)
    d_col = np.clip(kc - qc, -(NB_COLS - 1), NB_COLS - 1) + (NB_COLS - 1)
    onehot = (d_col[None] == np.arange(2 * NB_COLS - 1)[:, None, None]).astype(np.float32)
    by_rel = jnp.stack([rpb[:, NB_ROWS - 1 - rel:2 * NB_ROWS - 1 - rel, :] for rel in range(NB_ROWS)], axis=0)
    tab = jnp.einsum('rhkc,cqj->rhqkj', by_rel.astype(F32), jnp.asarray(onehot), precision=lax.Precision.HIGHEST)
    tab = jnp.where(ok[None, None, :, None, :], tab, MASK_VALUE)
    return tab.reshape(NB_ROWS, B_HEADS, GRID_W, NB_ROWS * GRID_W)


def _conv_kernel(cx_ref, cb_ref, cc_ref, w_ref, o_ref, pad_ref, *, n_ctx, seq):
    edge = 8
    chunk = ROW_TILE
    pad_ref[0:edge, :] = jnp.zeros((edge, GROUP_W), F32)
    pad_ref[edge + seq:2 * edge + seq, :] = jnp.zeros((edge, GROUP_W), F32)
    for c in range(seq // chunk):
        sl = slice(c * chunk, (c + 1) * chunk)
        pad_ref[edge + c * chunk:edge + (c + 1) * chunk, :] = cc_ref[0, sl, :] * cx_ref[0, sl, :]
    w = w_ref[...]
    for c in range(seq // chunk):
        lo = edge + c * chunk
        row = c * chunk + lax.broadcasted_iota(jnp.int32, (chunk, 1), 0)
        prev = pad_ref[lo - 1:lo - 1 + chunk, :]
        cur = pad_ref[lo:lo + chunk, :]
        nxt = pad_ref[lo + 1:lo + 1 + chunk, :]
        prev = jnp.where((row == 0) | (row == n_ctx), 0.0, prev)
        nxt = jnp.where((row == n_ctx - 1) | (row == seq - 1), 0.0, nxt)
        conv = w[0:1] * prev + w[1:2] * cur + w[2:3] * nxt
        o_ref[0, c * chunk:(c + 1) * chunk, :] = cb_ref[0, c * chunk:(c + 1) * chunk, :] * conv


def _short_conv(proj, conv_w, n_ctx):
    b, s, _ = proj.shape
    kernel = functools.partial(_conv_kernel, n_ctx=n_ctx, seq=s)
    return pl.pallas_call(
        kernel,
        grid=(b,),
        in_specs=[
            pl.BlockSpec((1, s, GROUP_W), lambda i: (i, 0, 5)),
            pl.BlockSpec((1, s, GROUP_W), lambda i: (i, 0, 6)),
            pl.BlockSpec((1, s, GROUP_W), lambda i: (i, 0, 7)),
            pl.BlockSpec((CONV_W, GROUP_W), lambda i: (0, 0)),
        ],
        out_specs=pl.BlockSpec((1, s, GROUP_W), lambda i: (i, 0, 0)),
        out_shape=jax.ShapeDtypeStruct((b, s, GROUP_W), F32),
        scratch_shapes=[pltpu.VMEM((s + 16, GROUP_W), F32)],
        compiler_params=_cparams(1),
        name="gated_short_conv",
    )(proj, proj, proj, conv_w)


OFFSET_GROUP = 16


def _hgrn_chunk(d, base, q_ref, z_ref, i_ref, gate_consts, o_ref, st_ref, kst, fst, vst, sst, pbuf, ones_bd, bd_mask):
    ch = SCAN_CHUNK
    sgn = 1 if d == 0 else -1
    lb_floor, one_minus_lb, floor_excess = gate_consts
    rows = pl.ds(base, ch)
    z = z_ref[0, rows, :]
    q = q_ref[0, rows, :]
    v = i_ref[0, rows, :]
    sig = 1.0 / (1.0 + jnp.exp(-z))
    f = lb_floor + one_minus_lb * sig
    k = one_minus_lb * (1.0 - sig) - floor_excess
    kst[d, ch:2 * ch, :] = k
    fst[d, ch:2 * ch, :] = f
    vst[d, ch:2 * ch, :] = v

    def shifted(ref, o):
        return ref[d, ch - sgn * o:2 * ch - sgn * o, :]

    inc = f
    exc = f
    step = 1
    while step < ch:
        sst[d, ch:2 * ch, :] = inc
        inc = inc * shifted(sst, step)
        step *= 2
    step = 1
    while step < ch:
        sst[d, ch:2 * ch, :] = exc
        exc = exc * shifted(sst, -step)
        step *= 2
    sst[d, ch:2 * ch, :] = exc
    exc = shifted(sst, -1)
    total = inc[ch - 1:ch, :] if d == 0 else inc[0:1, :]

    decay = jnp.ones((ch, GROUP_W), F32)
    acc = jnp.zeros((ch, GROUP_W), F32)
    for g0 in range(0, ch, OFFSET_GROUP):
        lo, hi = (g0, ch) if d == 0 else (0, ch - g0)
        n = hi - lo
        if g0 > 0:
            decay = decay[OFFSET_GROUP:] if d == 0 else decay[:n]
        qs = q[lo:hi]

        def window(ref, o):
            return ref[d, ch - sgn * o + lo:ch - sgn * o + hi, :]

        for oo in range(OFFSET_GROUP):
            o = g0 + oo
            if o > 0:
                decay = decay * window(fst, o - 1)
            pbuf[d, oo * n:(oo + 1) * n, :] = (qs * window(kst, o) * decay).astype(BF16)
        head_sums = _dot(pbuf[d, 0:OFFSET_GROUP * n, :], ones_bd)
        part = jnp.zeros((n, GROUP_W), F32)
        for oo in range(OFFSET_GROUP):
            part = part + head_sums[oo * n:(oo + 1) * n, :] * window(vst, g0 + oo)
        pieces = [part]
        if lo:
            pieces.insert(0, jnp.zeros((lo, GROUP_W), F32))
        if ch - hi:
            pieces.append(jnp.zeros((ch - hi, GROUP_W), F32))
        acc = acc + jnp.concatenate(pieces, axis=0)

    st = st_ref[d]
    o_inter = _dot_nt((q * inc).astype(BF16), st.astype(BF16))
    upd = _dot_tn(v.astype(BF16), (k * exc).astype(BF16))
    st_ref[d] = st * total + upd * bd_mask
    o_ref[0, rows, :] = o_ref[0, rows, :] + acc + o_inter


def _hgrn_kernel(q_ref, zf_ref, zb_ref, i_ref, lb_ref, o_ref, st_ref, kst, fst, vst, sst, pbuf, *, n_ctx, seq):
    ch = SCAN_CHUNK
    n_chunks = seq // ch
    ctx_chunks = n_ctx // ch
    lb = lb_ref[...]
    lb_floor = jnp.maximum(lb, LB_FLOOR)
    gate_consts = (lb_floor, 1.0 - lb, lb_floor - lb)
    hi = lax.broadcasted_iota(jnp.int32, (GROUP_W, GROUP_W), 0) // HEAD_DIM
    hj = lax.broadcasted_iota(jnp.int32, (GROUP_W, GROUP_W), 1) // HEAD_DIM
    bd_mask = jnp.where(hi == hj, 1.0, 0.0).astype(F32)
    ones_bd = bd_mask.astype(BF16)

    o_ref[...] = jnp.zeros(o_ref.shape, F32)
    st_ref[...] = jnp.zeros(st_ref.shape, F32)
    kst[...] = jnp.zeros(kst.shape, F32)
    fst[...] = jnp.zeros(fst.shape, F32)
    vst[...] = jnp.zeros(vst.shape, F32)
    sst[...] = jnp.ones(sst.shape, F32)

    def body(s, carry):
        cf = s
        cb = jnp.where(s < ctx_chunks, ctx_chunks - 1 - s, n_chunks - 1 - (s - ctx_chunks))
        for d, c, z_ref in ((0, cf, zf_ref), (1, cb, zb_ref)):
            base = pl.multiple_of(c * ch, ch)
            _hgrn_chunk(d, base, q_ref, z_ref, i_ref, gate_consts, o_ref, st_ref, kst, fst, vst, sst, pbuf,
                        ones_bd, bd_mask)
        return carry

    lax.fori_loop(0, n_chunks, body, 0)


def _hgrn(proj, lower_bound, n_ctx):
    b, s, _ = proj.shape
    ch = SCAN_CHUNK
    kernel = functools.partial(_hgrn_kernel, n_ctx=n_ctx, seq=s)
    col = lambda c: pl.BlockSpec((1, s, GROUP_W), lambda i: (i, 0, c))
    return pl.pallas_call(
        kernel,
        grid=(b,),
        in_specs=[col(8), col(9), col(10), col(11), pl.BlockSpec((1, GROUP_W), lambda i: (0, 0))],
        out_specs=pl.BlockSpec((1, s, GROUP_W), lambda i: (i, 0, 0)),
        out_shape=jax.ShapeDtypeStruct((b, s, GROUP_W), F32),
        scratch_shapes=[
            pltpu.VMEM((2, GROUP_W, GROUP_W), F32),
            pltpu.VMEM((2, 3 * ch, GROUP_W), F32),
            pltpu.VMEM((2, 3 * ch, GROUP_W), F32),
            pltpu.VMEM((2, 3 * ch, GROUP_W), F32),
            pltpu.VMEM((2, 3 * ch, GROUP_W), F32),
            pltpu.VMEM((2, OFFSET_GROUP * ch, GROUP_W), BF16),
        ],
        compiler_params=_cparams(1),
        name="hgrn2_bidirectional",
    )(proj, proj, proj, proj, lower_bound.reshape(1, GROUP_W))


def _post_kernel(ya_ref, yb_ref, yc_ref, od_ref, dg_ref, x_ref, mod_ref, gn_ref, wo_ref, nf_ref, wq_ref,
                 k1_ref, k2_ref, xo_ref, h_ref, sc_ref):
    yd = od_ref[0] * _silu(dg_ref[0])
    gn = gn_ref[...]
    parts = (ya_ref[0], yb_ref[0], yc_ref[0], yd)
    normed = [_rms(p) * gn[:, i * GROUP_W:(i + 1) * GROUP_W] for i, p in enumerate(parts)]
    y = _dot(jnp.concatenate(normed, axis=1).astype(BF16), wo_ref[...])
    xn = x_ref[0] + mod_ref[0, 2:3, :] * y
    xo_ref[0] = xn
    h = _rms(xn) * nf_ref[...] * (1.0 + mod_ref[0, 4:5, :]) + mod_ref[0, 3:4, :]
    h_ref[0] = h
    qv = _dot(h.astype(BF16), wq_ref[...])
    half = PEER_NKEYS
    for g in range(2 * PEER_HEADS):
        keys = k1_ref[...] if g % 2 == 0 else k2_ref[...]
        sc_ref[g] = _dot_nt(keys, qv[:, g * half:(g + 1) * half].astype(BF16))


def _post_mixer(ya, yb, yc, od, proj, xc, mods, gnorm, wo_bf16, nffn, wq_bf16, k1_bf16, k2_bf16, n_ctx_tiles, tile0):
    b, s, d = xc.shape
    tm = ROW_TILE
    nt = s // tm - tile0
    s_out = nt * tm
    ctx_cond = mods.shape[0] - 1
    nq = wq_bf16.shape[1]
    att0 = tile0 - (s - ya.shape[1]) // tm
    att = lambda: pl.BlockSpec((1, tm, GROUP_W), lambda i, j: (i, j + att0, 0))
    grp = lambda: pl.BlockSpec((1, tm, GROUP_W), lambda i, j: (i, j + tile0, 0))
    const = lambda shape: pl.BlockSpec(shape, lambda i, j: (0,) * len(shape))
    return pl.pallas_call(
        _post_kernel,
        grid=(b, nt),
        in_specs=[
            att(), att(), grp(), grp(),
            pl.BlockSpec((1, tm, GROUP_W), lambda i, j: (i, j + tile0, 12)),
            pl.BlockSpec((1, tm, d), lambda i, j: (i, j + tile0, 0)),
            pl.BlockSpec((1, N_MOD, d), lambda i, j: (jnp.where(j + tile0 < n_ctx_tiles, ctx_cond, i), 0, 0)),
            const((1, d)), const((d, d)), const((1, d)), const((d, nq)),
            const((PEER_NKEYS, PEER_NKEYS)), const((PEER_NKEYS, PEER_NKEYS)),
        ],
        out_specs=[
            pl.BlockSpec((1, tm, d), lambda i, j: (i, j, 0)),
            pl.BlockSpec((1, tm, d), lambda i, j: (i, j, 0)),
            pl.BlockSpec((2 * PEER_HEADS, PEER_NKEYS, tm), lambda i, j: (0, 0, i * nt + j)),
        ],
        out_shape=[
            jax.ShapeDtypeStruct((b, s_out, d), F32),
            jax.ShapeDtypeStruct((b, s_out, d), F32),
            jax.ShapeDtypeStruct((2 * PEER_HEADS, PEER_NKEYS, b * s_out), F32),
        ],
        compiler_params=_cparams(2),
        name="post_mixer",
    )(ya, yb, yc, od, proj, xc, mods, gnorm.reshape(1, d), wo_bf16, nffn.reshape(1, d), wq_bf16, k1_bf16, k2_bf16)


def _top16(s, rows=None):
    if rows is None:
        rows = lax.broadcasted_iota(jnp.int32, s.shape, 0).astype(F32)
    vals, ids = [], []
    cur = s
    for _ in range(PEER_TOPK):
        m = jnp.max(cur, axis=0, keepdims=True)
        am = jnp.min(jnp.where(cur == m, rows, jnp.inf), axis=0, keepdims=True)
        vals.append(m)
        ids.append(am)
        cur = jnp.where(rows == am, -jnp.inf, cur)
    return jnp.concatenate(vals, axis=0), jnp.concatenate(ids, axis=0)


def _candidate_rows(a):
    return PEER_TOPK // (a + 1)


def _pick(table, sel):
    out = jnp.zeros(sel.shape, table.dtype)
    for a in range(PEER_TOPK):
        out = out + jnp.where(sel == float(a), table[a:a + 1, :], 0.0)
    return out


def _route_kernel(sc_ref, idx_ref, gate_ref):
    for h in range(PEER_HEADS):
        v1, i1 = _top16(sc_ref[2 * h])
        v2, i2 = _top16(sc_ref[2 * h + 1])
        nb = [_candidate_rows(a) for a in range(PEER_TOPK)]
        cand = jnp.concatenate([v1[a:a + 1, :] + v2[0:nb[a], :] for a in range(PEER_TOPK)], axis=0)
        flat = jnp.concatenate([lax.broadcasted_iota(jnp.int32, (nb[a], v1.shape[1]), 0).astype(F32)
                                + float(a * PEER_TOPK) for a in range(PEER_TOPK)], axis=0)
        top_s, pos = _top16(cand, flat)
        a_sel = jnp.floor(pos * (1.0 / PEER_TOPK))
        b_sel = pos - a_sel * PEER_TOPK
        idx_ref[h] = (_pick(i1, a_sel) * PEER_NKEYS + _pick(i2, b_sel)).astype(jnp.int32)
        e = jnp.exp(top_s - jnp.max(top_s, axis=0, keepdims=True))
        gate_ref[h] = e / jnp.sum(e, axis=0, keepdims=True)


def _routing(scores_t):
    g, nk, t = scores_t.shape
    tt = TOPK_TOKENS
    out_spec = pl.BlockSpec((PEER_HEADS, PEER_TOPK, tt), lambda i: (0, 0, i))
    return pl.pallas_call(
        _route_kernel,
        grid=(t // tt,),
        in_specs=[pl.BlockSpec((g, nk, tt), lambda i: (0, 0, i))],
        out_specs=[out_spec, out_spec],
        out_shape=[jax.ShapeDtypeStruct((PEER_HEADS, PEER_TOPK, t), jnp.int32),
                   jax.ShapeDtypeStruct((PEER_HEADS, PEER_TOPK, t), F32)],
        compiler_params=_cparams(1),
        name="peer_routing",
    )(scores_t)


def _gelu_tanh(x):
    return 0.5 * x * (1.0 + jnp.tanh(math.sqrt(2.0 / math.pi) * (x + 0.044715 * (x * x * x))))


FOLD = (8, 128)


def _sublane_pair_sum(x, y, k):
    sub = lax.broadcasted_iota(jnp.int32, FOLD, 0)
    keep = (sub % (2 * k)) < k
    return jnp.where(keep, x, pltpu.roll(y, k, 0)) + jnp.where(keep, pltpu.roll(x, FOLD[0] - k, 0), y)


def _sublane_sums(p):
    z = [_sublane_pair_sum(p[j], p[j + 4], 4) for j in range(4)]
    w = [_sublane_pair_sum(z[0], z[2], 2), _sublane_pair_sum(z[1], z[3], 2)]
    return _sublane_pair_sum(w[0], w[1], 1)


LAND_SLOTS = PEER_TOKENS * PEER_HEADS * PEER_TOPK
COPY_GROUP = 64
REC = 3 * LAND_SLOTS


def _peer_kernel(rec_hbm, tbl_hbm, h_ref, gate_ref, x_ref, mod_ref, fn_ref, o_ref, rec_smem, tbl, act_rep,
                 tbl_sem, rec_sem, land_sem, *, n_steps, n_res, final):
    j = pl.program_id(0)
    tb = PEER_TOKENS
    per_tok = PEER_HEADS * PEER_TOPK
    sub = FOLD[0]

    def rec_copy(r, region):
        return pltpu.make_async_copy(rec_hbm.at[pl.ds(r * REC, REC)],
                                     rec_smem.at[pl.ds(region * REC, REC)], rec_sem.at[region])

    def issue_next(region, buf):
        n_groups = rec_smem[region * REC + 2 * LAND_SLOTS]

        def body(g, carry):
            for k in range(COPY_GROUP):
                e = rec_smem[region * REC + LAND_SLOTS + g * COPY_GROUP + k]
                pltpu.make_async_copy(tbl_hbm.at[e], tbl.at[n_res + buf * LAND_SLOTS + g * COPY_GROUP + k],
                                      land_sem.at[buf]).start(priority=k % 2)
            return carry
        lax.fori_loop(0, n_groups, body, 0)

    def wait_landing(region, buf):
        n_groups = rec_smem[region * REC + 2 * LAND_SLOTS + 1]

        def body(g, carry):
            grp = tbl.at[pl.ds(n_res + buf * LAND_SLOTS, COPY_GROUP)]
            pltpu.make_async_copy(grp, grp, land_sem.at[buf]).wait()
            return carry
        lax.fori_loop(0, n_groups, body, 0)

    eye = (lax.broadcasted_iota(jnp.int32, (per_tok, per_tok), 0)
           == lax.broadcasted_iota(jnp.int32, (per_tok, per_tok), 1))
    g2 = mod_ref[0, 5]
    hi_mask = jnp.uint32(0xFFFF0000)

    def consume(half):
        for t in range(tb):
            tok = half * tb + t
            hf = h_ref[tok]
            rows = [rec_smem[half * REC + t * per_tok + e] for e in range(per_tok)]
            parts = []
            for g in range(per_tok // sub):
                prods = []
                for k in range(sub):
                    word = tbl[rows[g * sub + k]]
                    prods.append(pltpu.bitcast(word << 16, F32) * hf)
                parts.append(_sublane_sums(prods))
            s = jnp.sum(jnp.concatenate(parts, axis=0), axis=1, keepdims=True)
            gate_col = jnp.sum(jnp.where(eye, gate_ref[tok:tok + 1, :], 0.0), axis=1, keepdims=True)
            act_rep[half] = jnp.broadcast_to(_gelu_tanh(s) * gate_col, (per_tok, FOLD[1]))
            accs = [jnp.zeros(FOLD, F32) for _ in range(4)]
            for e in range(per_tok):
                a = jnp.broadcast_to(act_rep[half, e:e + 1, :], FOLD)
                v = pltpu.bitcast(tbl[rows[e]] & hi_mask, F32)
                accs[e % 4] = accs[e % 4] + a * v
            xn = x_ref[tok] + g2 * ((accs[0] + accs[1]) + (accs[2] + accs[3]))
            if final:
                ms = (jnp.sum(jnp.sum(xn * xn, axis=1, keepdims=True), axis=0, keepdims=True)
                      * (1.0 / (FOLD[0] * FOLD[1])))
                xn = xn * lax.rsqrt(ms + EPS) * fn_ref[...]
            o_ref[tok] = xn

    @pl.when(j == 0)
    def _():
        whole = pltpu.make_async_copy(tbl_hbm.at[pl.ds(0, n_res)], tbl.at[pl.ds(0, n_res)], tbl_sem)
        whole.start()
        first = rec_copy(0, 1)
        first.start()
        first.wait()
        issue_next(1, 0)
        rec_copy(1, 0).start()
        rec_copy(2, 1).start()
        whole.wait()

    more = j + 1 < n_steps
    rec_copy(2 * j + 1, 0).wait()
    issue_next(0, 1)
    wait_landing(0, 0)
    consume(0)

    @pl.when(more)
    def _():
        rec_copy(2 * j + 3, 0).start()

    rec_copy(2 * j + 2, 1).wait()
    issue_next(1, 0)
    wait_landing(1, 1)
    consume(1)

    @pl.when(more)
    def _():
        rec_copy(2 * j + 4, 1).start()


def _peer_records(idx, n_res):
    blocks = idx.reshape(-1, LAND_SLOTS)
    nb = blocks.shape[0]
    away = blocks >= n_res
    rank = jnp.cumsum(away.astype(jnp.int32), axis=1) - 1
    n_groups = (jnp.sum(away.astype(jnp.int32), axis=1) + COPY_GROUP - 1) // COPY_GROUP
    buf = (jnp.arange(nb, dtype=jnp.int32) % 2)[:, None]
    rows = jnp.where(away, n_res + buf * LAND_SLOTS + rank, blocks)
    pos = jnp.arange(LAND_SLOTS, dtype=jnp.int32)[None, :]
    _, copy_list = lax.sort((jnp.where(away, 0, LAND_SLOTS) + pos, blocks), dimension=1, num_keys=1)
    zero_row = jnp.zeros((1, LAND_SLOTS), jnp.int32)
    counts = jnp.zeros((nb + 1, LAND_SLOTS), jnp.int32)
    counts = counts.at[:nb, 0].set(n_groups).at[1:, 1].set(n_groups)
    rec = jnp.concatenate([jnp.concatenate([zero_row, rows], axis=0),
                           jnp.concatenate([copy_list, zero_row], axis=0), counts], axis=1)
    return rec.reshape(-1)


def _peer_experts(h2, idx, gate, uv, x_mid, mods, final_gain, s_out, ctx_rows, final):
    t = h2.shape[0]
    tb = 2 * PEER_TOKENS
    per_tok = PEER_HEADS * PEER_TOPK
    n_steps = t // tb
    n_res = min(uv.shape[0], PEER_RESIDENT_BYTES // (4 * FOLD[0] * FOLD[1]))
    assert t % tb == 0 and s_out % tb == 0 and ctx_rows % tb == 0
    ctx_cond = mods.shape[0] - 1

    def mod_index(i):
        tok = i * tb
        return (jnp.where(tok % s_out < ctx_rows, ctx_cond, tok // s_out), 0, 0, 0)

    kernel = functools.partial(_peer_kernel, n_steps=n_steps, n_res=n_res, final=final)
    fold_spec = pl.BlockSpec((tb,) + FOLD, lambda i: (i, 0, 0))
    return pl.pallas_call(
        kernel,
        grid=(n_steps,),
        in_specs=[
            pl.BlockSpec(memory_space=pl.ANY),
            pl.BlockSpec(memory_space=pl.ANY),
            fold_spec,
            pl.BlockSpec((tb, per_tok), lambda i: (i, 0)),
            fold_spec,
            pl.BlockSpec((1, N_MOD) + FOLD, mod_index),
            pl.BlockSpec(FOLD, lambda i: (0, 0)),
        ],
        out_specs=fold_spec,
        out_shape=jax.ShapeDtypeStruct((t,) + FOLD, F32),
        scratch_shapes=[
            pltpu.SMEM((2 * REC,), jnp.int32),
            pltpu.VMEM((n_res + 2 * LAND_SLOTS,) + FOLD, jnp.uint32),
            pltpu.VMEM((2, per_tok, FOLD[1]), F32),
            pltpu.SemaphoreType.DMA,
            pltpu.SemaphoreType.DMA((2,)),
            pltpu.SemaphoreType.DMA((2,)),
        ],
        compiler_params=_cparams(1, PEER_VMEM_LIMIT),
        name="peer_experts",
    )(_peer_records(idx, n_res), uv, h2, gate, x_mid, mods.reshape(mods.shape[0], N_MOD, *FOLD),
      final_gain.reshape(FOLD))


def _pack_experts(u, v):
    n_exp = u.shape[0]
    ub = lax.bitcast_convert_type(u.astype(BF16), jnp.uint16).astype(jnp.uint32)
    vb = lax.bitcast_convert_type(v.astype(BF16), jnp.uint16).astype(jnp.uint32)
    return ((vb << 16) | ub).reshape(n_exp, *FOLD)


def _rope_tables(n_ctx, n_lat):
    t = jnp.arange(n_lat)
    inv = ROPE_THETA ** (-jnp.arange(0, AXIS_DIM, 2, dtype=F32) / AXIS_DIM)
    row = (t // GRID_W).astype(F32)[:, None] * inv
    col = (t % GRID_W).astype(F32)[:, None] * inv
    cos = jnp.concatenate([jnp.cos(row), jnp.cos(row), jnp.cos(col), jnp.cos(col)], axis=1)
    sin = jnp.concatenate([-jnp.sin(row), jnp.sin(row), -jnp.sin(col), jnp.sin(col)], axis=1)
    cos = jnp.concatenate([jnp.ones((n_ctx, HEAD_DIM), F32), cos], axis=0)
    sin = jnp.concatenate([jnp.zeros((n_ctx, HEAD_DIM), F32), sin], axis=0)
    return jnp.tile(cos, (1, A_HEADS)), jnp.tile(sin, (1, A_HEADS))


def kernel(x, c, ctx, c_ctx, w_mod, b_mod, norm_mix, norm_ffn, w_in, conv_w, attn_sink, na_rpb, lb_logits, group_norm, w_out, peer_wq, peer_k1, peer_k2, peer_u, peer_v, final_norm):
    b, n_lat, d = x.shape
    n_ctx = ctx.shape[1]
    depth = w_mod.shape[0]
    seq = n_ctx + n_lat
    assert n_ctx % ROW_TILE == 0 and n_lat % ROW_TILE == 0 and n_lat // GRID_W >= NB_ROWS
    assert n_lat >= WIN_BLOCK + 2 * WINDOW and w_in.shape[2] == IN_W
    n_ctx_tiles = n_ctx // ROW_TILE

    n_cond = b + 1
    pad = (-n_cond) % 8
    cvec = jnp.concatenate([c, c_ctx[None, :], jnp.zeros((pad, d), F32)], axis=0)
    mods = _modulation(cvec, w_mod, b_mod)[:, :n_cond].reshape(depth, n_cond, N_MOD, d)

    lb_soft = jax.nn.softmax(lb_logits.astype(F32), axis=0)
    lower_bounds = jnp.cumsum(lb_soft, axis=0) - lb_soft[0:1]
    cos_t, sin_t = _rope_tables(n_ctx, n_lat)

    xc = jnp.concatenate([ctx, x], axis=1)
    out = None
    for l in range(depth):
        last = l == depth - 1
        proj = _in_projection(xc, mods[l], norm_mix[l], w_in[l].astype(BF16), n_ctx_tiles)
        ya = _attention_a(proj, cos_t, sin_t, attn_sink[l], n_ctx, not last)
        yb = _attention_b(proj, _neighbourhood_bias(na_rpb[l]), n_ctx, not last)
        yc = _short_conv(proj, conv_w[l], n_ctx)
        od = _hgrn(proj, lower_bounds[l], n_ctx)
        tile0 = n_ctx_tiles if last else 0
        x_mid, h2, scores_t = _post_mixer(ya, yb, yc, od, proj, xc, mods[l], group_norm[l], w_out[l].astype(BF16),
                                          norm_ffn[l], peer_wq[l].astype(BF16), peer_k1[l].astype(BF16),
                                          peer_k2[l].astype(BF16), n_ctx_tiles, tile0)
        s_out = x_mid.shape[1]
        tokens = b * s_out
        idx_t, gate_t = _routing(scores_t)
        per_tok = PEER_HEADS * PEER_TOPK
        idx = idx_t.reshape(per_tok, tokens).T.reshape(tokens * per_tok)
        gate = gate_t.reshape(per_tok, tokens).T
        uv = _pack_experts(peer_u[l], peer_v[l])
        res = _peer_experts(h2.reshape(tokens, *FOLD), idx, gate, uv, x_mid.reshape(tokens, *FOLD), mods[l],
                            final_norm, s_out, 0 if last else n_ctx, last)
        if last:
            out = res.reshape(b, s_out, d)
        else:
            xc = res.reshape(b, s_out, d)
    return out
```

```python
import functools
import math

import jax
import jax.numpy as jnp
import numpy as np
from jax import lax
from jax.experimental import pallas as pl
from jax.experimental.pallas import tpu as pltpu

F32 = jnp.float32
BF16 = jnp.bfloat16

GRID_W = 64
HEAD_DIM = 64
GROUP_W = 256
A_HEADS = 4
A_KV_HEADS = 2
WINDOW = 128
WIN_BLOCK = 128
B_HEADS = 4
NB_ROWS = 8
NB_COLS = 16
CONV_W = 3
D_HEADS = 4
SCAN_CHUNK = 64
LB_FLOOR = 1e-20
ROPE_THETA = 10000.0
AXIS_DIM = HEAD_DIM // 2
PEER_HEADS = 8
PEER_NKEYS = 128
PEER_TOPK = 16
N_MOD = 6
EPS = 1e-6
MASK_VALUE = -1e30
IN_W = 13 * GROUP_W

ROW_TILE = 256
MOD_COL_TILE = 1536
TOPK_TOKENS = 128
PEER_TOKENS = 8
VMEM_LIMIT = 48 * 1024 * 1024


PEER_VMEM_LIMIT = 60 * 1024 * 1024
PEER_RESIDENT_BYTES = 44 * 1024 * 1024


def _cparams(n_axes, vmem_limit=VMEM_LIMIT):
    return pltpu.CompilerParams(dimension_semantics=("arbitrary",) * n_axes, vmem_limit_bytes=vmem_limit)


def _dot(a, b):
    return jnp.dot(a, b, preferred_element_type=F32)


def _dot_nt(a, b):
    return lax.dot_general(a, b, (((1,), (1,)), ((), ())), preferred_element_type=F32)


def _dot_tn(a, b):
    return lax.dot_general(a, b, (((0,), (0,)), ((), ())), preferred_element_type=F32)


def _rms(x):
    return x * lax.rsqrt(jnp.mean(x * x, axis=-1, keepdims=True) + EPS)


def _silu(x):
    return x / (1.0 + jnp.exp(-x))


def _mod_kernel(c_ref, w_ref, b_ref, o_ref):
    s = _silu(c_ref[...])
    o_ref[0] = _dot(s.astype(BF16), w_ref[0].astype(BF16)) + b_ref[0]


def _modulation(cvec, w_mod, b_mod):
    depth, d, n = w_mod.shape
    r = cvec.shape[0]
    tn = MOD_COL_TILE
    return pl.pallas_call(
        _mod_kernel,
        grid=(depth, n // tn),
        in_specs=[
            pl.BlockSpec((r, d), lambda l, j: (0, 0)),
            pl.BlockSpec((1, d, tn), lambda l, j: (l, 0, j)),
            pl.BlockSpec((1, 1, tn), lambda l, j: (l, 0, j)),
        ],
        out_specs=pl.BlockSpec((1, r, tn), lambda l, j: (l, 0, j)),
        out_shape=jax.ShapeDtypeStruct((depth, r, n), F32),
        compiler_params=_cparams(2),
        name="adaln_modulation",
    )(cvec, w_mod, b_mod.reshape(depth, 1, n))


def _inproj_kernel(x_ref, mod_ref, g_ref, w_ref, o_ref):
    y = _rms(x_ref[0]) * g_ref[...]
    h = y * (1.0 + mod_ref[0, 1:2, :]) + mod_ref[0, 0:1, :]
    o_ref[0] = _dot(h.astype(BF16), w_ref[...])


def _in_projection(xc, mods, gain, w_bf16, n_ctx_tiles):
    b, s, d = xc.shape
    n = w_bf16.shape[1]
    tm = ROW_TILE
    ctx_cond = mods.shape[0] - 1
    return pl.pallas_call(
        _inproj_kernel,
        grid=(b, s // tm),
        in_specs=[
            pl.BlockSpec((1, tm, d), lambda i, j: (i, j, 0)),
            pl.BlockSpec((1, N_MOD, d), lambda i, j: (jnp.where(j < n_ctx_tiles, ctx_cond, i), 0, 0)),
            pl.BlockSpec((1, d), lambda i, j: (0, 0)),
            pl.BlockSpec((d, n), lambda i, j: (0, 0)),
        ],
        out_specs=pl.BlockSpec((1, tm, n), lambda i, j: (i, j, 0)),
        out_shape=jax.ShapeDtypeStruct((b, s, n), F32),
        compiler_params=_cparams(2),
        name="in_projection",
    )(xc, mods, gain.reshape(1, d), w_bf16)


def _rope(x, cos, sin_signed):
    n = x.shape[1]
    lane = lax.broadcasted_iota(jnp.int32, x.shape, 1)
    fwd = pltpu.roll(x, n - AXIS_DIM // 2, 1)
    bwd = pltpu.roll(x, AXIS_DIM // 2, 1)
    swapped = jnp.where((lane % AXIS_DIM) < AXIS_DIM // 2, fwd, bwd)
    return x * cos + swapped * sin_signed


def _attn_a_kernel(sink_ref, q_ref, k_ref, v_ref, cq_ref, sq_ref, ck_ref, sk_ref, o_ref, kh_ref, vh_ref,
                   *, n_ctx, n_lat, ctx_steps):
    t = pl.program_id(1)
    span = WIN_BLOCK + 2 * WINDOW
    g = A_HEADS // A_KV_HEADS

    @pl.when(t == 0)
    def _():
        kr = _rope(k_ref[0], ck_ref[...], sk_ref[...])
        v = v_ref[0]
        for h in range(A_KV_HEADS):
            kh_ref[h] = kr[:, h * HEAD_DIM:(h + 1) * HEAD_DIM].astype(BF16)
            vh_ref[h] = v[:, h * HEAD_DIM:(h + 1) * HEAD_DIM].astype(BF16)

    q = _rope(q_ref[0], cq_ref[...], sq_ref[...]) * (HEAD_DIM ** -0.5)
    row = lax.broadcasted_iota(jnp.int32, (g * WIN_BLOCK, 1), 0)

    def q_pair(hk):
        parts = [q[:, (hk * g + j) * HEAD_DIM:(hk * g + j + 1) * HEAD_DIM] for j in range(g)]
        return jnp.concatenate(parts, axis=0).astype(BF16)

    def sink_col(hk):
        col = jnp.full((g * WIN_BLOCK, 1), sink_ref[hk * g], F32)
        for j in range(1, g):
            col = jnp.where(row >= j * WIN_BLOCK, sink_ref[hk * g + j], col)
        return col

    def assemble(outs):
        pieces = []
        for hk in range(A_KV_HEADS):
            for j in range(g):
                pieces.append(outs[hk][j * WIN_BLOCK:(j + 1) * WIN_BLOCK])
        return jnp.concatenate(pieces, axis=1)

    def latent_block():
        p0 = (t - ctx_steps) * WIN_BLOCK
        ws = jnp.clip(p0 - WINDOW, 0, n_lat - span)
        kstart = pl.multiple_of(n_ctx + ws, WIN_BLOCK)
        qpos = p0 + row % WIN_BLOCK
        kpos = ws + lax.broadcasted_iota(jnp.int32, (1, span), 1)
        mask = jnp.abs(qpos - kpos) <= WINDOW
        outs = []
        for hk in range(A_KV_HEADS):
            q2 = q_pair(hk)
            s_loc = _dot_nt(q2, kh_ref[hk, pl.ds(kstart, span), :])
            s_loc = jnp.where(mask, s_loc, MASK_VALUE)
            s_ctx = _dot_nt(q2, kh_ref[hk, 0:n_ctx, :])
            sk = sink_col(hk)
            m = jnp.maximum(jnp.maximum(jnp.max(s_loc, axis=1, keepdims=True),
                                        jnp.max(s_ctx, axis=1, keepdims=True)), sk)
            p_loc = jnp.exp(s_loc - m)
            p_ctx = jnp.exp(s_ctx - m)
            den = (jnp.sum(p_loc, axis=1, keepdims=True) + jnp.sum(p_ctx, axis=1, keepdims=True)
                   + jnp.exp(sk - m))
            o = (_dot(p_loc.astype(BF16), vh_ref[hk, pl.ds(kstart, span), :])
                 + _dot(p_ctx.astype(BF16), vh_ref[hk, 0:n_ctx, :]))
            outs.append(o / den)
        o_ref[0] = assemble(outs)

    def ctx_block():
        outs = []
        for hk in range(A_KV_HEADS):
            q2 = q_pair(hk)
            s_ctx = _dot_nt(q2, kh_ref[hk, 0:n_ctx, :])
            sk = sink_col(hk)
            m = jnp.maximum(jnp.max(s_ctx, axis=1, keepdims=True), sk)
            p_ctx = jnp.exp(s_ctx - m)
            den = jnp.sum(p_ctx, axis=1, keepdims=True) + jnp.exp(sk - m)
            outs.append(_dot(p_ctx.astype(BF16), vh_ref[hk, 0:n_ctx, :]) / den)
        o_ref[0] = assemble(outs)

    if ctx_steps:
        pl.when(t < ctx_steps)(ctx_block)
        pl.when(t >= ctx_steps)(latent_block)
    else:
        latent_block()


def _attention_a(proj, cos_t, sin_t, sink, n_ctx, with_ctx):
    b, s, _ = proj.shape
    n_lat = s - n_ctx
    ctx_blocks = n_ctx // WIN_BLOCK
    ctx_steps = ctx_blocks if with_ctx else 0
    off = ctx_blocks - ctx_steps
    steps = ctx_steps + n_lat // WIN_BLOCK
    kw = A_KV_HEADS * HEAD_DIM
    kernel = functools.partial(_attn_a_kernel, n_ctx=n_ctx, n_lat=n_lat, ctx_steps=ctx_steps)
    return pl.pallas_call(
        kernel,
        grid=(b, steps),
        in_specs=[
            pl.BlockSpec(memory_space=pltpu.SMEM),
            pl.BlockSpec((1, WIN_BLOCK, GROUP_W), lambda i, t: (i, t + off, 0)),
            pl.BlockSpec((1, s, kw), lambda i, t: (i, 0, 2)),
            pl.BlockSpec((1, s, kw), lambda i, t: (i, 0, 3)),
            pl.BlockSpec((WIN_BLOCK, GROUP_W), lambda i, t: (t + off, 0)),
            pl.BlockSpec((WIN_BLOCK, GROUP_W), lambda i, t: (t + off, 0)),
            pl.BlockSpec((s, kw), lambda i, t: (0, 0)),
            pl.BlockSpec((s, kw), lambda i, t: (0, 0)),
        ],
        out_specs=pl.BlockSpec((1, WIN_BLOCK, GROUP_W), lambda i, t: (i, t, 0)),
        out_shape=jax.ShapeDtypeStruct((b, steps * WIN_BLOCK, GROUP_W), F32),
        scratch_shapes=[pltpu.VMEM((A_KV_HEADS, s, HEAD_DIM), BF16), pltpu.VMEM((A_KV_HEADS, s, HEAD_DIM), BF16)],
        compiler_params=_cparams(2),
        name="window_attention",
    )(sink, proj, proj, proj, cos_t, sin_t, cos_t, sin_t)


def _attn_b_kernel(q_ref, k_ref, v_ref, bias_ref, o_ref, kh_ref, vh_ref, *, n_ctx, n_lat, ctx_steps):
    t = pl.program_id(1)
    rows = n_lat // GRID_W
    nkeys = NB_ROWS * GRID_W

    @pl.when(t == 0)
    def _():
        k = k_ref[0]
        v = v_ref[0]
        for h in range(B_HEADS):
            kh_ref[h] = k[:, h * HEAD_DIM:(h + 1) * HEAD_DIM].astype(BF16)
            vh_ref[h] = v[:, h * HEAD_DIM:(h + 1) * HEAD_DIM].astype(BF16)

    q = q_ref[0] * (HEAD_DIM ** -0.5)

    def latent_block():
        r = t - ctx_steps
        r0 = jnp.clip(r - NB_ROWS // 2, 0, rows - NB_ROWS)
        kstart = pl.multiple_of(n_ctx + r0 * GRID_W, GRID_W)
        outs = []
        for h in range(B_HEADS):
            qh = q[:, h * HEAD_DIM:(h + 1) * HEAD_DIM].astype(BF16)
            s_loc = _dot_nt(qh, kh_ref[h, pl.ds(kstart, nkeys), :]) + bias_ref[0, h]
            s_ctx = _dot_nt(qh, kh_ref[h, 0:n_ctx, :])
            m = jnp.maximum(jnp.max(s_loc, axis=1, keepdims=True), jnp.max(s_ctx, axis=1, keepdims=True))
            p_loc = jnp.exp(s_loc - m)
            p_ctx = jnp.exp(s_ctx - m)
            den = jnp.sum(p_loc, axis=1, keepdims=True) + jnp.sum(p_ctx, axis=1, keepdims=True)
            o = (_dot(p_loc.astype(BF16), vh_ref[h, pl.ds(kstart, nkeys), :])
                 + _dot(p_ctx.astype(BF16), vh_ref[h, 0:n_ctx, :]))
            outs.append(o / den)
        o_ref[0] = jnp.concatenate(outs, axis=1)

    def ctx_block():
        outs = []
        for h in range(B_HEADS):
            qh = q[:, h * HEAD_DIM:(h + 1) * HEAD_DIM].astype(BF16)
            s_ctx = _dot_nt(qh, kh_ref[h, 0:n_ctx, :])
            m = jnp.max(s_ctx, axis=1, keepdims=True)
            p_ctx = jnp.exp(s_ctx - m)
            den = jnp.sum(p_ctx, axis=1, keepdims=True)
            outs.append(_dot(p_ctx.astype(BF16), vh_ref[h, 0:n_ctx, :]) / den)
        o_ref[0] = jnp.concatenate(outs, axis=1)

    if ctx_steps:
        pl.when(t < ctx_steps)(ctx_block)
        pl.when(t >= ctx_steps)(latent_block)
    else:
        latent_block()


def _attention_b(proj, bias, n_ctx, with_ctx):
    b, s, _ = proj.shape
    n_lat = s - n_ctx
    rows = n_lat // GRID_W
    ctx_blocks = n_ctx // GRID_W
    ctx_steps = ctx_blocks if with_ctx else 0
    off = ctx_blocks - ctx_steps
    steps = ctx_steps + rows

    def bias_index(i, t):
        r = jnp.maximum(t - ctx_steps, 0)
        r0 = jnp.clip(r - NB_ROWS // 2, 0, rows - NB_ROWS)
        return (r - r0, 0, 0, 0)

    kernel = functools.partial(_attn_b_kernel, n_ctx=n_ctx, n_lat=n_lat, ctx_steps=ctx_steps)
    return pl.pallas_call(
        kernel,
        grid=(b, steps),
        in_specs=[
            pl.BlockSpec((1, GRID_W, GROUP_W), lambda i, t: (i, t + off, 2)),
            pl.BlockSpec((1, s, GROUP_W), lambda i, t: (i, 0, 3)),
            pl.BlockSpec((1, s, GROUP_W), lambda i, t: (i, 0, 4)),
            pl.BlockSpec((1, B_HEADS, GRID_W, NB_ROWS * GRID_W), bias_index),
        ],
        out_specs=pl.BlockSpec((1, GRID_W, GROUP_W), lambda i, t: (i, t, 0)),
        out_shape=jax.ShapeDtypeStruct((b, steps * GRID_W, GROUP_W), F32),
        scratch_shapes=[pltpu.VMEM((B_HEADS, s, HEAD_DIM), BF16), pltpu.VMEM((B_HEADS, s, HEAD_DIM), BF16)],
        compiler_params=_cparams(2),
        name="neighbourhood_attention",
    )(proj, proj, proj, bias)


def _neighbourhood_bias(rpb):
    qc = np.arange(GRID_W)[:, None]
    kc = np.arange(GRID_W)[None, :]
    win0 = np.clip(qc - NB_COLS // 2, 0, GRID_W - NB_COLS)
    ok = (kc >= win0) & (kc < win0 + NB_COLS)
    d_col = np.clip(kc - qc, -(NB_COLS - 1), NB_COLS - 1) + (NB_COLS - 1)
    onehot = (d_col[None] == np.arange(2 * NB_COLS - 1)[:, None, None]).astype(np.float32)
    by_rel = jnp.stack([rpb[:, NB_ROWS - 1 - rel:2 * NB_ROWS - 1 - rel, :] for rel in range(NB_ROWS)], axis=0)
    tab = jnp.einsum('rhkc,cqj->rhqkj', by_rel.astype(F32), jnp.asarray(onehot), precision=lax.Precision.HIGHEST)
    tab = jnp.where(ok[None, None, :, None, :], tab, MASK_VALUE)
    return tab.reshape(NB_ROWS, B_HEADS, GRID_W, NB_ROWS * GRID_W)


def _conv_kernel(cx_ref, cb_ref, cc_ref, w_ref, o_ref, pad_ref, *, n_ctx, seq):
    edge = 8
    chunk = ROW_TILE
    pad_ref[0:edge, :] = jnp.zeros((edge, GROUP_W), F32)
    pad_ref[edge + seq:2 * edge + seq, :] = jnp.zeros((edge, GROUP_W), F32)
    for c in range(seq // chunk):
        sl = slice(c * chunk, (c + 1) * chunk)
        pad_ref[edge + c * chunk:edge + (c + 1) * chunk, :] = cc_ref[0, sl, :] * cx_ref[0, sl, :]
    w = w_ref[...]
    for c in range(seq // chunk):
        lo = edge + c * chunk
        row = c * chunk + lax.broadcasted_iota(jnp.int32, (chunk, 1), 0)
        prev = pad_ref[lo - 1:lo - 1 + chunk, :]
        cur = pad_ref[lo:lo + chunk, :]
        nxt = pad_ref[lo + 1:lo + 1 + chunk, :]
        prev = jnp.where((row == 0) | (row == n_ctx), 0.0, prev)
        nxt = jnp.where((row == n_ctx - 1) | (row == seq - 1), 0.0, nxt)
        conv = w[0:1] * prev + w[1:2] * cur + w[2:3] * nxt
        o_ref[0, c * chunk:(c + 1) * chunk, :] = cb_ref[0, c * chunk:(c + 1) * chunk, :] * conv


def _short_conv(proj, conv_w, n_ctx):
    b, s, _ = proj.shape
    kernel = functools.partial(_conv_kernel, n_ctx=n_ctx, seq=s)
    return pl.pallas_call(
        kernel,
        grid=(b,),
        in_specs=[
            pl.BlockSpec((1, s, GROUP_W), lambda i: (i, 0, 5)),
            pl.BlockSpec((1, s, GROUP_W), lambda i: (i, 0, 6)),
            pl.BlockSpec((1, s, GROUP_W), lambda i: (i, 0, 7)),
            pl.BlockSpec((CONV_W, GROUP_W), lambda i: (0, 0)),
        ],
        out_specs=pl.BlockSpec((1, s, GROUP_W), lambda i: (i, 0, 0)),
        out_shape=jax.ShapeDtypeStruct((b, s, GROUP_W), F32),
        scratch_shapes=[pltpu.VMEM((s + 16, GROUP_W), F32)],
        compiler_params=_cparams(1),
        name="gated_short_conv",
    )(proj, proj, proj, conv_w)


OFFSET_GROUP = 16


def _hgrn_chunk(d, base, q_ref, z_ref, i_ref, gate_consts, o_ref, st_ref, kst, fst, vst, sst, pbuf, ones_bd, bd_mask):
    ch = SCAN_CHUNK
    sgn = 1 if d == 0 else -1
    lb_floor, one_minus_lb, floor_excess = gate_consts
    rows = pl.ds(base, ch)
    z = z_ref[0, rows, :]
    q = q_ref[0, rows, :]
    v = i_ref[0, rows, :]
    sig = 1.0 / (1.0 + jnp.exp(-z))
    f = lb_floor + one_minus_lb * sig
    k = one_minus_lb * (1.0 - sig) - floor_excess
    kst[d, ch:2 * ch, :] = k
    fst[d, ch:2 * ch, :] = f
    vst[d, ch:2 * ch, :] = v

    def shifted(ref, o):
        return ref[d, ch - sgn * o:2 * ch - sgn * o, :]

    inc = f
    exc = f
    step = 1
    while step < ch:
        sst[d, ch:2 * ch, :] = inc
        inc = inc * shifted(sst, step)
        step *= 2
    step = 1
    while step < ch:
        sst[d, ch:2 * ch, :] = exc
        exc = exc * shifted(sst, -step)
        step *= 2
    sst[d, ch:2 * ch, :] = exc
    exc = shifted(sst, -1)
    total = inc[ch - 1:ch, :] if d == 0 else inc[0:1, :]

    decay = jnp.ones((ch, GROUP_W), F32)
    acc = jnp.zeros((ch, GROUP_W), F32)
    for g0 in range(0, ch, OFFSET_GROUP):
        lo, hi = (g0, ch) if d == 0 else (0, ch - g0)
        n = hi - lo
        if g0 > 0:
            decay = decay[OFFSET_GROUP:] if d == 0 else decay[:n]
        qs = q[lo:hi]

        def window(ref, o):
            return ref[d, ch - sgn * o + lo:ch - sgn * o + hi, :]

        for oo in range(OFFSET_GROUP):
            o = g0 + oo
            if o > 0:
                decay = decay * window(fst, o - 1)
            pbuf[d, oo * n:(oo + 1) * n, :] = (qs * window(kst, o) * decay).astype(BF16)
        head_sums = _dot(pbuf[d, 0:OFFSET_GROUP * n, :], ones_bd)
        part = jnp.zeros((n, GROUP_W), F32)
        for oo in range(OFFSET_GROUP):
            part = part + head_sums[oo * n:(oo + 1) * n, :] * window(vst, g0 + oo)
        pieces = [part]
        if lo:
            pieces.insert(0, jnp.zeros((lo, GROUP_W), F32))
        if ch - hi:
            pieces.append(jnp.zeros((ch - hi, GROUP_W), F32))
        acc = acc + jnp.concatenate(pieces, axis=0)

    st = st_ref[d]
    o_inter = _dot_nt((q * inc).astype(BF16), st.astype(BF16))
    upd = _dot_tn(v.astype(BF16), (k * exc).astype(BF16))
    st_ref[d] = st * total + upd * bd_mask
    o_ref[0, rows, :] = o_ref[0, rows, :] + acc + o_inter


def _hgrn_kernel(q_ref, zf_ref, zb_ref, i_ref, lb_ref, o_ref, st_ref, kst, fst, vst, sst, pbuf, *, n_ctx, seq):
    ch = SCAN_CHUNK
    n_chunks = seq // ch
    ctx_chunks = n_ctx // ch
    lb = lb_ref[...]
    lb_floor = jnp.maximum(lb, LB_FLOOR)
    gate_consts = (lb_floor, 1.0 - lb, lb_floor - lb)
    hi = lax.broadcasted_iota(jnp.int32, (GROUP_W, GROUP_W), 0) // HEAD_DIM
    hj = lax.broadcasted_iota(jnp.int32, (GROUP_W, GROUP_W), 1) // HEAD_DIM
    bd_mask = jnp.where(hi == hj, 1.0, 0.0).astype(F32)
    ones_bd = bd_mask.astype(BF16)

    o_ref[...] = jnp.zeros(o_ref.shape, F32)
    st_ref[...] = jnp.zeros(st_ref.shape, F32)
    kst[...] = jnp.zeros(kst.shape, F32)
    fst[...] = jnp.zeros(fst.shape, F32)
    vst[...] = jnp.zeros(vst.shape, F32)
    sst[...] = jnp.ones(sst.shape, F32)

    def body(s, carry):
        cf = s
        cb = jnp.where(s < ctx_chunks, ctx_chunks - 1 - s, n_chunks - 1 - (s - ctx_chunks))
        for d, c, z_ref in ((0, cf, zf_ref), (1, cb, zb_ref)):
            base = pl.multiple_of(c * ch, ch)
            _hgrn_chunk(d, base, q_ref, z_ref, i_ref, gate_consts, o_ref, st_ref, kst, fst, vst, sst, pbuf,
                        ones_bd, bd_mask)
        return carry

    lax.fori_loop(0, n_chunks, body, 0)


def _hgrn(proj, lower_bound, n_ctx):
    b, s, _ = proj.shape
    ch = SCAN_CHUNK
    kernel = functools.partial(_hgrn_kernel, n_ctx=n_ctx, seq=s)
    col = lambda c: pl.BlockSpec((1, s, GROUP_W), lambda i: (i, 0, c))
    return pl.pallas_call(
        kernel,
        grid=(b,),
        in_specs=[col(8), col(9), col(10), col(11), pl.BlockSpec((1, GROUP_W), lambda i: (0, 0))],
        out_specs=pl.BlockSpec((1, s, GROUP_W), lambda i: (i, 0, 0)),
        out_shape=jax.ShapeDtypeStruct((b, s, GROUP_W), F32),
        scratch_shapes=[
            pltpu.VMEM((2, GROUP_W, GROUP_W), F32),
            pltpu.VMEM((2, 3 * ch, GROUP_W), F32),
            pltpu.VMEM((2, 3 * ch, GROUP_W), F32),
            pltpu.VMEM((2, 3 * ch, GROUP_W), F32),
            pltpu.VMEM((2, 3 * ch, GROUP_W), F32),
            pltpu.VMEM((2, OFFSET_GROUP * ch, GROUP_W), BF16),
        ],
        compiler_params=_cparams(1),
        name="hgrn2_bidirectional",
    )(proj, proj, proj, proj, lower_bound.reshape(1, GROUP_W))


def _post_kernel(ya_ref, yb_ref, yc_ref, od_ref, dg_ref, x_ref, mod_ref, gn_ref, wo_ref, nf_ref, wq_ref,
                 k1_ref, k2_ref, xo_ref, h_ref, sc_ref):
    yd = od_ref[0] * _silu(dg_ref[0])
    gn = gn_ref[...]
    parts = (ya_ref[0], yb_ref[0], yc_ref[0], yd)
    normed = [_rms(p) * gn[:, i * GROUP_W:(i + 1) * GROUP_W] for i, p in enumerate(parts)]
    y = _dot(jnp.concatenate(normed, axis=1).astype(BF16), wo_ref[...])
    xn = x_ref[0] + mod_ref[0, 2:3, :] * y
    xo_ref[0] = xn
    h = _rms(xn) * nf_ref[...] * (1.0 + mod_ref[0, 4:5, :]) + mod_ref[0, 3:4, :]
    h_ref[0] = h
    qv = _dot(h.astype(BF16), wq_ref[...])
    half = PEER_NKEYS
    for g in range(2 * PEER_HEADS):
        keys = k1_ref[...] if g % 2 == 0 else k2_ref[...]
        sc_ref[g] = _dot_nt(keys, qv[:, g * half:(g + 1) * half].astype(BF16))


def _post_mixer(ya, yb, yc, od, proj, xc, mods, gnorm, wo_bf16, nffn, wq_bf16, k1_bf16, k2_bf16, n_ctx_tiles, tile0):
    b, s, d = xc.shape
    tm = ROW_TILE
    nt = s // tm - tile0
    s_out = nt * tm
    ctx_cond = mods.shape[0] - 1
    nq = wq_bf16.shape[1]
    att0 = tile0 - (s - ya.shape[1]) // tm
    att = lambda: pl.BlockSpec((1, tm, GROUP_W), lambda i, j: (i, j + att0, 0))
    grp = lambda: pl.BlockSpec((1, tm, GROUP_W), lambda i, j: (i, j + tile0, 0))
    const = lambda shape: pl.BlockSpec(shape, lambda i, j: (0,) * len(shape))
    return pl.pallas_call(
        _post_kernel,
        grid=(b, nt),
        in_specs=[
            att(), att(), grp(), grp(),
            pl.BlockSpec((1, tm, GROUP_W), lambda i, j: (i, j + tile0, 12)),
            pl.BlockSpec((1, tm, d), lambda i, j: (i, j + tile0, 0)),
            pl.BlockSpec((1, N_MOD, d), lambda i, j: (jnp.where(j + tile0 < n_ctx_tiles, ctx_cond, i), 0, 0)),
            const((1, d)), const((d, d)), const((1, d)), const((d, nq)),
            const((PEER_NKEYS, PEER_NKEYS)), const((PEER_NKEYS, PEER_NKEYS)),
        ],
        out_specs=[
            pl.BlockSpec((1, tm, d), lambda i, j: (i, j, 0)),
            pl.BlockSpec((1, tm, d), lambda i, j: (i, j, 0)),
            pl.BlockSpec((2 * PEER_HEADS, PEER_NKEYS, tm), lambda i, j: (0, 0, i * nt + j)),
        ],
        out_shape=[
            jax.ShapeDtypeStruct((b, s_out, d), F32),
            jax.ShapeDtypeStruct((b, s_out, d), F32),
            jax.ShapeDtypeStruct((2 * PEER_HEADS, PEER_NKEYS, b * s_out), F32),
        ],
        compiler_params=_cparams(2),
        name="post_mixer",
    )(ya, yb, yc, od, proj, xc, mods, gnorm.reshape(1, d), wo_bf16, nffn.reshape(1, d), wq_bf16, k1_bf16, k2_bf16)


def _top16(s, rows=None):
    if rows is None:
        rows = lax.broadcasted_iota(jnp.int32, s.shape, 0).astype(F32)
    vals, ids = [], []
    cur = s
    for _ in range(PEER_TOPK):
        m = jnp.max(cur, axis=0, keepdims=True)
        am = jnp.min(jnp.where(cur == m, rows, jnp.inf), axis=0, keepdims=True)
        vals.append(m)
        ids.append(am)
        cur = jnp.where(rows == am, -jnp.inf, cur)
    return jnp.concatenate(vals, axis=0), jnp.concatenate(ids, axis=0)


def _candidate_rows(a):
    return PEER_TOPK // (a + 1)


def _pick(table, sel):
    out = jnp.zeros(sel.shape, table.dtype)
    for a in range(PEER_TOPK):
        out = out + jnp.where(sel == float(a), table[a:a + 1, :], 0.0)
    return out


def _route_kernel(sc_ref, idx_ref, gate_ref):
    ids, gates = [], []
    for h in range(PEER_HEADS):
        v1, i1 = _top16(sc_ref[2 * h])
        v2, i2 = _top16(sc_ref[2 * h + 1])
        nb = [_candidate_rows(a) for a in range(PEER_TOPK)]
        cand = jnp.concatenate([v1[a:a + 1, :] + v2[0:nb[a], :] for a in range(PEER_TOPK)], axis=0)
        flat = jnp.concatenate([lax.broadcasted_iota(jnp.int32, (nb[a], v1.shape[1]), 0).astype(F32)
                                + float(a * PEER_TOPK) for a in range(PEER_TOPK)], axis=0)
        top_s, pos = _top16(cand, flat)
        a_sel = jnp.floor(pos * (1.0 / PEER_TOPK))
        b_sel = pos - a_sel * PEER_TOPK
        ids.append(_pick(i1, a_sel) * PEER_NKEYS + _pick(i2, b_sel))
        e = jnp.exp(top_s - jnp.max(top_s, axis=0, keepdims=True))
        gates.append(e / jnp.sum(e, axis=0, keepdims=True))
    idx_ref[...] = jnp.concatenate(ids, axis=0).T.astype(jnp.int32)
    gate_ref[...] = jnp.concatenate(gates, axis=0).T


def _routing(scores_t):
    g, nk, t = scores_t.shape
    tt = TOPK_TOKENS
    per_tok = PEER_HEADS * PEER_TOPK
    out_spec = pl.BlockSpec((tt, per_tok), lambda i: (i, 0))
    return pl.pallas_call(
        _route_kernel,
        grid=(t // tt,),
        in_specs=[pl.BlockSpec((g, nk, tt), lambda i: (0, 0, i))],
        out_specs=[out_spec, out_spec],
        out_shape=[jax.ShapeDtypeStruct((t, per_tok), jnp.int32),
                   jax.ShapeDtypeStruct((t, per_tok), F32)],
        compiler_params=_cparams(1),
        name="peer_routing",
    )(scores_t)


def _gelu_tanh(x):
    return 0.5 * x * (1.0 + jnp.tanh(math.sqrt(2.0 / math.pi) * (x + 0.044715 * (x * x * x))))


FOLD = (8, 128)


def _sublane_pair_sum(x, y, k):
    sub = lax.broadcasted_iota(jnp.int32, FOLD, 0)
    keep = (sub % (2 * k)) < k
    return jnp.where(keep, x, pltpu.roll(y, k, 0)) + jnp.where(keep, pltpu.roll(x, FOLD[0] - k, 0), y)


def _sublane_sums(p):
    z = [_sublane_pair_sum(p[j], p[j + 4], 4) for j in range(4)]
    w = [_sublane_pair_sum(z[0], z[2], 2), _sublane_pair_sum(z[1], z[3], 2)]
    return _sublane_pair_sum(w[0], w[1], 1)


LAND_SLOTS = PEER_TOKENS * PEER_HEADS * PEER_TOPK
COPY_GROUP = 64
REC = 3 * LAND_SLOTS


def _peer_kernel(rec_hbm, tbl_hbm, h_ref, gate_ref, x_ref, mod_ref, fn_ref, o_ref, rec_smem, tbl, act_rep,
                 tbl_sem, rec_sem, land_sem, *, n_steps, n_res, final):
    j = pl.program_id(0)
    tb = PEER_TOKENS
    per_tok = PEER_HEADS * PEER_TOPK
    sub = FOLD[0]

    def rec_copy(r, region):
        return pltpu.make_async_copy(rec_hbm.at[pl.ds(r * REC, REC)],
                                     rec_smem.at[pl.ds(region * REC, REC)], rec_sem.at[region])

    def issue_next(region, buf):
        n_groups = rec_smem[region * REC + 2 * LAND_SLOTS]

        def body(g, carry):
            for k in range(COPY_GROUP):
                e = rec_smem[region * REC + LAND_SLOTS + g * COPY_GROUP + k]
                pltpu.make_async_copy(tbl_hbm.at[e], tbl.at[n_res + buf * LAND_SLOTS + g * COPY_GROUP + k],
                                      land_sem.at[buf]).start(priority=k % 2)
            return carry
        lax.fori_loop(0, n_groups, body, 0)

    def wait_landing(region, buf):
        n_groups = rec_smem[region * REC + 2 * LAND_SLOTS + 1]

        def body(g, carry):
            grp = tbl.at[pl.ds(n_res + buf * LAND_SLOTS, COPY_GROUP)]
            pltpu.make_async_copy(grp, grp, land_sem.at[buf]).wait()
            return carry
        lax.fori_loop(0, n_groups, body, 0)

    g2 = mod_ref[0, 5]
    hi_mask = jnp.uint32(0xFFFF0000)

    def consume(half):
        for t in range(tb):
            tok = half * tb + t
            hf = h_ref[tok]
            rows = [rec_smem[half * REC + t * per_tok + e] for e in range(per_tok)]
            parts = []
            for g in range(per_tok // sub):
                prods = []
                for k in range(sub):
                    word = tbl[rows[g * sub + k]]
                    prods.append(pltpu.bitcast(word << 16, F32) * hf)
                parts.append(_sublane_sums(prods))
            s_row = jnp.sum(jnp.concatenate(parts, axis=0).T, axis=0, keepdims=True)
            act_row = _gelu_tanh(s_row) * gate_ref[tok:tok + 1, :]
            act_rep[tok] = jnp.broadcast_to(act_row, (FOLD[1], per_tok)).T
            accs = [jnp.zeros(FOLD, F32) for _ in range(4)]
            for e in range(per_tok):
                a = jnp.broadcast_to(act_rep[tok, e:e + 1, :], FOLD)
                v = pltpu.bitcast(tbl[rows[e]] & hi_mask, F32)
                accs[e % 4] = accs[e % 4] + a * v
            xn = x_ref[tok] + g2 * ((accs[0] + accs[1]) + (accs[2] + accs[3]))
            if final:
                ms = (jnp.sum(jnp.sum(xn * xn, axis=1, keepdims=True), axis=0, keepdims=True)
                      * (1.0 / (FOLD[0] * FOLD[1])))
                xn = xn * lax.rsqrt(ms + EPS) * fn_ref[...]
            o_ref[tok] = xn

    @pl.when(j == 0)
    def _():
        whole = pltpu.make_async_copy(tbl_hbm.at[pl.ds(0, n_res)], tbl.at[pl.ds(0, n_res)], tbl_sem)
        whole.start()
        first = rec_copy(0, 1)
        first.start()
        first.wait()
        issue_next(1, 0)
        rec_copy(1, 0).start()
        rec_copy(2, 1).start()
        whole.wait()

    more = j + 1 < n_steps
    rec_copy(2 * j + 1, 0).wait()
    issue_next(0, 1)
    wait_landing(0, 0)
    consume(0)

    @pl.when(more)
    def _():
        rec_copy(2 * j + 3, 0).start()

    rec_copy(2 * j + 2, 1).wait()
    issue_next(1, 0)
    wait_landing(1, 1)
    consume(1)

    @pl.when(more)
    def _():
        rec_copy(2 * j + 4, 1).start()


def _peer_records(idx, n_res):
    blocks = idx.reshape(-1, LAND_SLOTS)
    nb = blocks.shape[0]
    away = blocks >= n_res
    rank = jnp.cumsum(away.astype(jnp.int32), axis=1) - 1
    n_groups = (jnp.sum(away.astype(jnp.int32), axis=1) + COPY_GROUP - 1) // COPY_GROUP
    buf = (jnp.arange(nb, dtype=jnp.int32) % 2)[:, None]
    rows = jnp.where(away, n_res + buf * LAND_SLOTS + rank, blocks)
    pos = jnp.arange(LAND_SLOTS, dtype=jnp.int32)[None, :]
    _, copy_list = lax.sort((jnp.where(away, 0, LAND_SLOTS) + pos, blocks), dimension=1, num_keys=1)
    zero_row = jnp.zeros((1, LAND_SLOTS), jnp.int32)
    counts = jnp.zeros((nb + 1, LAND_SLOTS), jnp.int32)
    counts = counts.at[:nb, 0].set(n_groups).at[1:, 1].set(n_groups)
    rec = jnp.concatenate([jnp.concatenate([zero_row, rows], axis=0),
                           jnp.concatenate([copy_list, zero_row], axis=0), counts], axis=1)
    return rec.reshape(-1)


def _peer_experts(h2, idx, gate, uv, x_mid, mods, final_gain, s_out, ctx_rows, final):
    t = h2.shape[0]
    tb = 2 * PEER_TOKENS
    per_tok = PEER_HEADS * PEER_TOPK
    n_steps = t // tb
    n_res = min(uv.shape[0], PEER_RESIDENT_BYTES // (4 * FOLD[0] * FOLD[1]))
    assert t % tb == 0 and s_out % tb == 0 and ctx_rows % tb == 0
    ctx_cond = mods.shape[0] - 1

    def mod_index(i):
        tok = i * tb
        return (jnp.where(tok % s_out < ctx_rows, ctx_cond, tok // s_out), 0, 0, 0)

    kernel = functools.partial(_peer_kernel, n_steps=n_steps, n_res=n_res, final=final)
    fold_spec = pl.BlockSpec((tb,) + FOLD, lambda i: (i, 0, 0))
    return pl.pallas_call(
        kernel,
        grid=(n_steps,),
        in_specs=[
            pl.BlockSpec(memory_space=pl.ANY),
            pl.BlockSpec(memory_space=pl.ANY),
            fold_spec,
            pl.BlockSpec((tb, per_tok), lambda i: (i, 0)),
            fold_spec,
            pl.BlockSpec((1, N_MOD) + FOLD, mod_index),
            pl.BlockSpec(FOLD, lambda i: (0, 0)),
        ],
        out_specs=fold_spec,
        out_shape=jax.ShapeDtypeStruct((t,) + FOLD, F32),
        scratch_shapes=[
            pltpu.SMEM((2 * REC,), jnp.int32),
            pltpu.VMEM((n_res + 2 * LAND_SLOTS,) + FOLD, jnp.uint32),
            pltpu.VMEM((2 * PEER_TOKENS, per_tok, FOLD[1]), F32),
            pltpu.SemaphoreType.DMA,
            pltpu.SemaphoreType.DMA((2,)),
            pltpu.SemaphoreType.DMA((2,)),
        ],
        compiler_params=_cparams(1, PEER_VMEM_LIMIT),
        name="peer_experts",
    )(_peer_records(idx, n_res), uv, h2, gate, x_mid, mods.reshape(mods.shape[0], N_MOD, *FOLD),
      final_gain.reshape(FOLD))


def _pack_experts(u, v):
    n_exp = u.shape[0]
    ub = lax.bitcast_convert_type(u.astype(BF16), jnp.uint16).astype(jnp.uint32)
    vb = lax.bitcast_convert_type(v.astype(BF16), jnp.uint16).astype(jnp.uint32)
    return ((vb << 16) | ub).reshape(n_exp, *FOLD)


def _rope_tables(n_ctx, n_lat):
    t = jnp.arange(n_lat)
    inv = ROPE_THETA ** (-jnp.arange(0, AXIS_DIM, 2, dtype=F32) / AXIS_DIM)
    row = (t // GRID_W).astype(F32)[:, None] * inv
    col = (t % GRID_W).astype(F32)[:, None] * inv
    cos = jnp.concatenate([jnp.cos(row), jnp.cos(row), jnp.cos(col), jnp.cos(col)], axis=1)
    sin = jnp.concatenate([-jnp.sin(row), jnp.sin(row), -jnp.sin(col), jnp.sin(col)], axis=1)
    cos = jnp.concatenate([jnp.ones((n_ctx, HEAD_DIM), F32), cos], axis=0)
    sin = jnp.concatenate([jnp.zeros((n_ctx, HEAD_DIM), F32), sin], axis=0)
    return jnp.tile(cos, (1, A_HEADS)), jnp.tile(sin, (1, A_HEADS))


def kernel(x, c, ctx, c_ctx, w_mod, b_mod, norm_mix, norm_ffn, w_in, conv_w, attn_sink, na_rpb, lb_logits, group_norm, w_out, peer_wq, peer_k1, peer_k2, peer_u, peer_v, final_norm):
    b, n_lat, d = x.shape
    n_ctx = ctx.shape[1]
    depth = w_mod.shape[0]
    seq = n_ctx + n_lat
    assert n_ctx % ROW_TILE == 0 and n_lat % ROW_TILE == 0 and n_lat // GRID_W >= NB_ROWS
    assert n_lat >= WIN_BLOCK + 2 * WINDOW and w_in.shape[2] == IN_W
    n_ctx_tiles = n_ctx // ROW_TILE

    n_cond = b + 1
    pad = (-n_cond) % 8
    cvec = jnp.concatenate([c, c_ctx[None, :], jnp.zeros((pad, d), F32)], axis=0)
    mods = _modulation(cvec, w_mod, b_mod)[:, :n_cond].reshape(depth, n_cond, N_MOD, d)

    lb_soft = jax.nn.softmax(lb_logits.astype(F32), axis=0)
    lower_bounds = jnp.cumsum(lb_soft, axis=0) - lb_soft[0:1]
    cos_t, sin_t = _rope_tables(n_ctx, n_lat)

    xc = jnp.concatenate([ctx, x], axis=1)
    out = None
    for l in range(depth):
        last = l == depth - 1
        proj = _in_projection(xc, mods[l], norm_mix[l], w_in[l].astype(BF16), n_ctx_tiles)
        ya = _attention_a(proj, cos_t, sin_t, attn_sink[l], n_ctx, not last)
        yb = _attention_b(proj, _neighbourhood_bias(na_rpb[l]), n_ctx, not last)
        yc = _short_conv(proj, conv_w[l], n_ctx)
        od = _hgrn(proj, lower_bounds[l], n_ctx)
        tile0 = n_ctx_tiles if last else 0
        x_mid, h2, scores_t = _post_mixer(ya, yb, yc, od, proj, xc, mods[l], group_norm[l], w_out[l].astype(BF16),
                                          norm_ffn[l], peer_wq[l].astype(BF16), peer_k1[l].astype(BF16),
                                          peer_k2[l].astype(BF16), n_ctx_tiles, tile0)
        s_out = x_mid.shape[1]
        tokens = b * s_out
        idx, gate = _routing(scores_t)
        idx = idx.reshape(-1)
        uv = _pack_experts(peer_u[l], peer_v[l])
        res = _peer_experts(h2.reshape(tokens, *FOLD), idx, gate, uv, x_mid.reshape(tokens, *FOLD), mods[l],
                            final_norm, s_out, 0 if last else n_ctx, last)
        if last:
            out = res.reshape(b, s_out, d)
        else:
            xc = res.reshape(b, s_out, d)
    return out
```

```python
import functools
import math

import jax
import jax.numpy as jnp
import numpy as np
from jax import lax
from jax.experimental import pallas as pl
from jax.experimental.pallas import tpu as pltpu

F32 = jnp.float32
BF16 = jnp.bfloat16

GRID_W = 64
HEAD_DIM = 64
GROUP_W = 256
A_HEADS = 4
A_KV_HEADS = 2
WINDOW = 128
WIN_BLOCK = 128
B_HEADS = 4
NB_ROWS = 8
NB_COLS = 16
CONV_W = 3
D_HEADS = 4
SCAN_CHUNK = 64
LB_FLOOR = 1e-20
ROPE_THETA = 10000.0
AXIS_DIM = HEAD_DIM // 2
PEER_HEADS = 8
PEER_NKEYS = 128
PEER_TOPK = 16
N_MOD = 6
EPS = 1e-6
MASK_VALUE = -1e30
IN_W = 13 * GROUP_W

ROW_TILE = 256
MOD_COL_TILE = 1536
TOPK_TOKENS = 128
PEER_TOKENS = 8
VMEM_LIMIT = 48 * 1024 * 1024


PEER_VMEM_LIMIT = 60 * 1024 * 1024
PEER_RESIDENT_BYTES = 44 * 1024 * 1024


def _cparams(n_axes, vmem_limit=VMEM_LIMIT):
    return pltpu.CompilerParams(dimension_semantics=("arbitrary",) * n_axes, vmem_limit_bytes=vmem_limit)


def _dot(a, b):
    return jnp.dot(a, b, preferred_element_type=F32)


def _dot_nt(a, b):
    return lax.dot_general(a, b, (((1,), (1,)), ((), ())), preferred_element_type=F32)


def _dot_tn(a, b):
    return lax.dot_general(a, b, (((0,), (0,)), ((), ())), preferred_element_type=F32)


def _rms(x):
    return x * lax.rsqrt(jnp.mean(x * x, axis=-1, keepdims=True) + EPS)


def _silu(x):
    return x / (1.0 + jnp.exp(-x))


def _mod_kernel(c_ref, w_ref, b_ref, o_ref):
    s = _silu(c_ref[...])
    o_ref[0] = _dot(s.astype(BF16), w_ref[0].astype(BF16)) + b_ref[0]


def _modulation(cvec, w_mod, b_mod):
    depth, d, n = w_mod.shape
    r = cvec.shape[0]
    tn = MOD_COL_TILE
    return pl.pallas_call(
        _mod_kernel,
        grid=(depth, n // tn),
        in_specs=[
            pl.BlockSpec((r, d), lambda l, j: (0, 0)),
            pl.BlockSpec((1, d, tn), lambda l, j: (l, 0, j)),
            pl.BlockSpec((1, 1, tn), lambda l, j: (l, 0, j)),
        ],
        out_specs=pl.BlockSpec((1, r, tn), lambda l, j: (l, 0, j)),
        out_shape=jax.ShapeDtypeStruct((depth, r, n), F32),
        compiler_params=_cparams(2),
        name="adaln_modulation",
    )(cvec, w_mod, b_mod.reshape(depth, 1, n))


def _inproj_kernel(x_ref, mod_ref, g_ref, w_ref, o_ref):
    y = _rms(x_ref[0]) * g_ref[...]
    h = y * (1.0 + mod_ref[0, 1:2, :]) + mod_ref[0, 0:1, :]
    o_ref[0] = _dot(h.astype(BF16), w_ref[...])


def _in_projection(xc, mods, gain, w_bf16, n_ctx_tiles):
    b, s, d = xc.shape
    n = w_bf16.shape[1]
    tm = ROW_TILE
    ctx_cond = mods.shape[0] - 1
    return pl.pallas_call(
        _inproj_kernel,
        grid=(b, s // tm),
        in_specs=[
            pl.BlockSpec((1, tm, d), lambda i, j: (i, j, 0)),
            pl.BlockSpec((1, N_MOD, d), lambda i, j: (jnp.where(j < n_ctx_tiles, ctx_cond, i), 0, 0)),
            pl.BlockSpec((1, d), lambda i, j: (0, 0)),
            pl.BlockSpec((d, n), lambda i, j: (0, 0)),
        ],
        out_specs=pl.BlockSpec((1, tm, n), lambda i, j: (i, j, 0)),
        out_shape=jax.ShapeDtypeStruct((b, s, n), F32),
        compiler_params=_cparams(2),
        name="in_projection",
    )(xc, mods, gain.reshape(1, d), w_bf16)


def _rope(x, cos, sin_signed):
    n = x.shape[1]
    lane = lax.broadcasted_iota(jnp.int32, x.shape, 1)
    fwd = pltpu.roll(x, n - AXIS_DIM // 2, 1)
    bwd = pltpu.roll(x, AXIS_DIM // 2, 1)
    swapped = jnp.where((lane % AXIS_DIM) < AXIS_DIM // 2, fwd, bwd)
    return x * cos + swapped * sin_signed


def _attn_a_kernel(sink_ref, q_ref, k_ref, v_ref, cq_ref, sq_ref, ck_ref, sk_ref, o_ref, kh_ref, vh_ref,
                   *, n_ctx, n_lat, ctx_steps):
    t = pl.program_id(1)
    span = WIN_BLOCK + 2 * WINDOW
    g = A_HEADS // A_KV_HEADS

    @pl.when(t == 0)
    def _():
        kr = _rope(k_ref[0], ck_ref[...], sk_ref[...])
        v = v_ref[0]
        for h in range(A_KV_HEADS):
            kh_ref[h] = kr[:, h * HEAD_DIM:(h + 1) * HEAD_DIM].astype(BF16)
            vh_ref[h] = v[:, h * HEAD_DIM:(h + 1) * HEAD_DIM].astype(BF16)

    q = _rope(q_ref[0], cq_ref[...], sq_ref[...]) * (HEAD_DIM ** -0.5)
    row = lax.broadcasted_iota(jnp.int32, (g * WIN_BLOCK, 1), 0)

    def q_pair(hk):
        parts = [q[:, (hk * g + j) * HEAD_DIM:(hk * g + j + 1) * HEAD_DIM] for j in range(g)]
        return jnp.concatenate(parts, axis=0).astype(BF16)

    def sink_col(hk):
        col = jnp.full((g * WIN_BLOCK, 1), sink_ref[hk * g], F32)
        for j in range(1, g):
            col = jnp.where(row >= j * WIN_BLOCK, sink_ref[hk * g + j], col)
        return col

    def assemble(outs):
        pieces = []
        for hk in range(A_KV_HEADS):
            for j in range(g):
                pieces.append(outs[hk][j * WIN_BLOCK:(j + 1) * WIN_BLOCK])
        return jnp.concatenate(pieces, axis=1)

    def latent_block():
        p0 = (t - ctx_steps) * WIN_BLOCK
        ws = jnp.clip(p0 - WINDOW, 0, n_lat - span)
        kstart = pl.multiple_of(n_ctx + ws, WIN_BLOCK)
        qpos = p0 + row % WIN_BLOCK
        kpos = ws + lax.broadcasted_iota(jnp.int32, (1, span), 1)
        mask = jnp.abs(qpos - kpos) <= WINDOW
        outs = []
        for hk in range(A_KV_HEADS):
            q2 = q_pair(hk)
            s_loc = _dot_nt(q2, kh_ref[hk, pl.ds(kstart, span), :])
            s_loc = jnp.where(mask, s_loc, MASK_VALUE)
            s_ctx = _dot_nt(q2, kh_ref[hk, 0:n_ctx, :])
            sk = sink_col(hk)
            m = jnp.maximum(jnp.maximum(jnp.max(s_loc, axis=1, keepdims=True),
                                        jnp.max(s_ctx, axis=1, keepdims=True)), sk)
            p_loc = jnp.exp(s_loc - m)
            p_ctx = jnp.exp(s_ctx - m)
            den = (jnp.sum(p_loc, axis=1, keepdims=True) + jnp.sum(p_ctx, axis=1, keepdims=True)
                   + jnp.exp(sk - m))
            o = (_dot(p_loc.astype(BF16), vh_ref[hk, pl.ds(kstart, span), :])
                 + _dot(p_ctx.astype(BF16), vh_ref[hk, 0:n_ctx, :]))
            outs.append(o / den)
        o_ref[0] = assemble(outs)

    def ctx_block():
        outs = []
        for hk in range(A_KV_HEADS):
            q2 = q_pair(hk)
            s_ctx = _dot_nt(q2, kh_ref[hk, 0:n_ctx, :])
            sk = sink_col(hk)
            m = jnp.maximum(jnp.max(s_ctx, axis=1, keepdims=True), sk)
            p_ctx = jnp.exp(s_ctx - m)
            den = jnp.sum(p_ctx, axis=1, keepdims=True) + jnp.exp(sk - m)
            outs.append(_dot(p_ctx.astype(BF16), vh_ref[hk, 0:n_ctx, :]) / den)
        o_ref[0] = assemble(outs)

    if ctx_steps:
        pl.when(t < ctx_steps)(ctx_block)
        pl.when(t >= ctx_steps)(latent_block)
    else:
        latent_block()


def _attention_a(proj, cos_t, sin_t, sink, n_ctx, with_ctx):
    b, s, _ = proj.shape
    n_lat = s - n_ctx
    ctx_blocks = n_ctx // WIN_BLOCK
    ctx_steps = ctx_blocks if with_ctx else 0
    off = ctx_blocks - ctx_steps
    steps = ctx_steps + n_lat // WIN_BLOCK
    kw = A_KV_HEADS * HEAD_DIM
    kernel = functools.partial(_attn_a_kernel, n_ctx=n_ctx, n_lat=n_lat, ctx_steps=ctx_steps)
    return pl.pallas_call(
        kernel,
        grid=(b, steps),
        in_specs=[
            pl.BlockSpec(memory_space=pltpu.SMEM),
            pl.BlockSpec((1, WIN_BLOCK, GROUP_W), lambda i, t: (i, t + off, 0)),
            pl.BlockSpec((1, s, kw), lambda i, t: (i, 0, 2)),
            pl.BlockSpec((1, s, kw), lambda i, t: (i, 0, 3)),
            pl.BlockSpec((WIN_BLOCK, GROUP_W), lambda i, t: (t + off, 0)),
            pl.BlockSpec((WIN_BLOCK, GROUP_W), lambda i, t: (t + off, 0)),
            pl.BlockSpec((s, kw), lambda i, t: (0, 0)),
            pl.BlockSpec((s, kw), lambda i, t: (0, 0)),
        ],
        out_specs=pl.BlockSpec((1, WIN_BLOCK, GROUP_W), lambda i, t: (i, t, 0)),
        out_shape=jax.ShapeDtypeStruct((b, steps * WIN_BLOCK, GROUP_W), F32),
        scratch_shapes=[pltpu.VMEM((A_KV_HEADS, s, HEAD_DIM), BF16), pltpu.VMEM((A_KV_HEADS, s, HEAD_DIM), BF16)],
        compiler_params=_cparams(2),
        name="window_attention",
    )(sink, proj, proj, proj, cos_t, sin_t, cos_t, sin_t)


def _attn_b_kernel(q_ref, k_ref, v_ref, bias_ref, o_ref, kh_ref, vh_ref, *, n_ctx, n_lat, ctx_steps):
    t = pl.program_id(1)
    rows = n_lat // GRID_W
    nkeys = NB_ROWS * GRID_W

    @pl.when(t == 0)
    def _():
        k = k_ref[0]
        v = v_ref[0]
        for h in range(B_HEADS):
            kh_ref[h] = k[:, h * HEAD_DIM:(h + 1) * HEAD_DIM].astype(BF16)
            vh_ref[h] = v[:, h * HEAD_DIM:(h + 1) * HEAD_DIM].astype(BF16)

    q = q_ref[0] * (HEAD_DIM ** -0.5)

    def latent_block():
        r = t - ctx_steps
        r0 = jnp.clip(r - NB_ROWS // 2, 0, rows - NB_ROWS)
        kstart = pl.multiple_of(n_ctx + r0 * GRID_W, GRID_W)
        outs = []
        for h in range(B_HEADS):
            qh = q[:, h * HEAD_DIM:(h + 1) * HEAD_DIM].astype(BF16)
            s_loc = _dot_nt(qh, kh_ref[h, pl.ds(kstart, nkeys), :]) + bias_ref[0, h]
            s_ctx = _dot_nt(qh, kh_ref[h, 0:n_ctx, :])
            m = jnp.maximum(jnp.max(s_loc, axis=1, keepdims=True), jnp.max(s_ctx, axis=1, keepdims=True))
            p_loc = jnp.exp(s_loc - m)
            p_ctx = jnp.exp(s_ctx - m)
            den = jnp.sum(p_loc, axis=1, keepdims=True) + jnp.sum(p_ctx, axis=1, keepdims=True)
            o = (_dot(p_loc.astype(BF16), vh_ref[h, pl.ds(kstart, nkeys), :])
                 + _dot(p_ctx.astype(BF16), vh_ref[h, 0:n_ctx, :]))
            outs.append(o / den)
        o_ref[0] = jnp.concatenate(outs, axis=1)

    def ctx_block():
        outs = []
        for h in range(B_HEADS):
            qh = q[:, h * HEAD_DIM:(h + 1) * HEAD_DIM].astype(BF16)
            s_ctx = _dot_nt(qh, kh_ref[h, 0:n_ctx, :])
            m = jnp.max(s_ctx, axis=1, keepdims=True)
            p_ctx = jnp.exp(s_ctx - m)
            den = jnp.sum(p_ctx, axis=1, keepdims=True)
            outs.append(_dot(p_ctx.astype(BF16), vh_ref[h, 0:n_ctx, :]) / den)
        o_ref[0] = jnp.concatenate(outs, axis=1)

    if ctx_steps:
        pl.when(t < ctx_steps)(ctx_block)
        pl.when(t >= ctx_steps)(latent_block)
    else:
        latent_block()


def _attention_b(proj, bias, n_ctx, with_ctx):
    b, s, _ = proj.shape
    n_lat = s - n_ctx
    rows = n_lat // GRID_W
    ctx_blocks = n_ctx // GRID_W
    ctx_steps = ctx_blocks if with_ctx else 0
    off = ctx_blocks - ctx_steps
    steps = ctx_steps + rows

    def bias_index(i, t):
        r = jnp.maximum(t - ctx_steps, 0)
        r0 = jnp.clip(r - NB_ROWS // 2, 0, rows - NB_ROWS)
        return (r - r0, 0, 0, 0)

    kernel = functools.partial(_attn_b_kernel, n_ctx=n_ctx, n_lat=n_lat, ctx_steps=ctx_steps)
    return pl.pallas_call(
        kernel,
        grid=(b, steps),
        in_specs=[
            pl.BlockSpec((1, GRID_W, GROUP_W), lambda i, t: (i, t + off, 2)),
            pl.BlockSpec((1, s, GROUP_W), lambda i, t: (i, 0, 3)),
            pl.BlockSpec((1, s, GROUP_W), lambda i, t: (i, 0, 4)),
            pl.BlockSpec((1, B_HEADS, GRID_W, NB_ROWS * GRID_W), bias_index),
        ],
        out_specs=pl.BlockSpec((1, GRID_W, GROUP_W), lambda i, t: (i, t, 0)),
        out_shape=jax.ShapeDtypeStruct((b, steps * GRID_W, GROUP_W), F32),
        scratch_shapes=[pltpu.VMEM((B_HEADS, s, HEAD_DIM), BF16), pltpu.VMEM((B_HEADS, s, HEAD_DIM), BF16)],
        compiler_params=_cparams(2),
        name="neighbourhood_attention",
    )(proj, proj, proj, bias)


def _neighbourhood_bias(rpb):
    qc = np.arange(GRID_W)[:, None]
    kc = np.arange(GRID_W)[None, :]
    win0 = np.clip(qc - NB_COLS // 2, 0, GRID_W - NB_COLS)
    ok = (kc >= win0) & (kc < win0 + NB_COLS)
    d_col = np.clip(kc - qc, -(NB_COLS - 1), NB_COLS - 1) + (NB_COLS - 1)
    onehot = (d_col[None] == np.arange(2 * NB_COLS - 1)[:, None, None]).astype(np.float32)
    by_rel = jnp.stack([rpb[:, NB_ROWS - 1 - rel:2 * NB_ROWS - 1 - rel, :] for rel in range(NB_ROWS)], axis=0)
    tab = jnp.einsum('rhkc,cqj->rhqkj', by_rel.astype(F32), jnp.asarray(onehot), precision=lax.Precision.HIGHEST)
    tab = jnp.where(ok[None, None, :, None, :], tab, MASK_VALUE)
    return tab.reshape(NB_ROWS, B_HEADS, GRID_W, NB_ROWS * GRID_W)


def _conv_kernel(cx_ref, cb_ref, cc_ref, w_ref, o_ref, pad_ref, *, n_ctx, seq):
    edge = 8
    chunk = ROW_TILE
    pad_ref[0:edge, :] = jnp.zeros((edge, GROUP_W), F32)
    pad_ref[edge + seq:2 * edge + seq, :] = jnp.zeros((edge, GROUP_W), F32)
    for c in range(seq // chunk):
        sl = slice(c * chunk, (c + 1) * chunk)
        pad_ref[edge + c * chunk:edge + (c + 1) * chunk, :] = cc_ref[0, sl, :] * cx_ref[0, sl, :]
    w = w_ref[...]
    for c in range(seq // chunk):
        lo = edge + c * chunk
        row = c * chunk + lax.broadcasted_iota(jnp.int32, (chunk, 1), 0)
        prev = pad_ref[lo - 1:lo - 1 + chunk, :]
        cur = pad_ref[lo:lo + chunk, :]
        nxt = pad_ref[lo + 1:lo + 1 + chunk, :]
        prev = jnp.where((row == 0) | (row == n_ctx), 0.0, prev)
        nxt = jnp.where((row == n_ctx - 1) | (row == seq - 1), 0.0, nxt)
        conv = w[0:1] * prev + w[1:2] * cur + w[2:3] * nxt
        o_ref[0, c * chunk:(c + 1) * chunk, :] = cb_ref[0, c * chunk:(c + 1) * chunk, :] * conv


def _short_conv(proj, conv_w, n_ctx):
    b, s, _ = proj.shape
    kernel = functools.partial(_conv_kernel, n_ctx=n_ctx, seq=s)
    return pl.pallas_call(
        kernel,
        grid=(b,),
        in_specs=[
            pl.BlockSpec((1, s, GROUP_W), lambda i: (i, 0, 5)),
            pl.BlockSpec((1, s, GROUP_W), lambda i: (i, 0, 6)),
            pl.BlockSpec((1, s, GROUP_W), lambda i: (i, 0, 7)),
            pl.BlockSpec((CONV_W, GROUP_W), lambda i: (0, 0)),
        ],
        out_specs=pl.BlockSpec((1, s, GROUP_W), lambda i: (i, 0, 0)),
        out_shape=jax.ShapeDtypeStruct((b, s, GROUP_W), F32),
        scratch_shapes=[pltpu.VMEM((s + 16, GROUP_W), F32)],
        compiler_params=_cparams(1),
        name="gated_short_conv",
    )(proj, proj, proj, conv_w)


OFFSET_GROUP = 16


def _hgrn_chunk(d, base, q_ref, z_ref, i_ref, gate_consts, o_ref, st_ref, kst, fst, vst, sst, pbuf, ones_bd, bd_mask):
    ch = SCAN_CHUNK
    sgn = 1 if d == 0 else -1
    lb_floor, one_minus_lb, floor_excess = gate_consts
    rows = pl.ds(base, ch)
    z = z_ref[0, rows, :]
    q = q_ref[0, rows, :]
    v = i_ref[0, rows, :]
    sig = 1.0 / (1.0 + jnp.exp(-z))
    f = lb_floor + one_minus_lb * sig
    k = one_minus_lb * (1.0 - sig) - floor_excess
    kst[d, ch:2 * ch, :] = k
    fst[d, ch:2 * ch, :] = f
    vst[d, ch:2 * ch, :] = v

    def shifted(ref, o):
        return ref[d, ch - sgn * o:2 * ch - sgn * o, :]

    inc = f
    exc = f
    step = 1
    while step < ch:
        sst[d, ch:2 * ch, :] = inc
        inc = inc * shifted(sst, step)
        step *= 2
    step = 1
    while step < ch:
        sst[d, ch:2 * ch, :] = exc
        exc = exc * shifted(sst, -step)
        step *= 2
    sst[d, ch:2 * ch, :] = exc
    exc = shifted(sst, -1)
    total = inc[ch - 1:ch, :] if d == 0 else inc[0:1, :]

    decay = jnp.ones((ch, GROUP_W), F32)
    acc = jnp.zeros((ch, GROUP_W), F32)
    for g0 in range(0, ch, OFFSET_GROUP):
        lo, hi = (g0, ch) if d == 0 else (0, ch - g0)
        n = hi - lo
        if g0 > 0:
            decay = decay[OFFSET_GROUP:] if d == 0 else decay[:n]
        qs = q[lo:hi]

        def window(ref, o):
            return ref[d, ch - sgn * o + lo:ch - sgn * o + hi, :]

        for oo in range(OFFSET_GROUP):
            o = g0 + oo
            if o > 0:
                decay = decay * window(fst, o - 1)
            pbuf[d, oo * n:(oo + 1) * n, :] = (qs * window(kst, o) * decay).astype(BF16)
        head_sums = _dot(pbuf[d, 0:OFFSET_GROUP * n, :], ones_bd)
        part = jnp.zeros((n, GROUP_W), F32)
        for oo in range(OFFSET_GROUP):
            part = part + head_sums[oo * n:(oo + 1) * n, :] * window(vst, g0 + oo)
        pieces = [part]
        if lo:
            pieces.insert(0, jnp.zeros((lo, GROUP_W), F32))
        if ch - hi:
            pieces.append(jnp.zeros((ch - hi, GROUP_W), F32))
        acc = acc + jnp.concatenate(pieces, axis=0)

    st = st_ref[d]
    o_inter = _dot_nt((q * inc).astype(BF16), st.astype(BF16))
    upd = _dot_tn(v.astype(BF16), (k * exc).astype(BF16))
    st_ref[d] = st * total + upd * bd_mask
    o_ref[0, rows, :] = o_ref[0, rows, :] + acc + o_inter


def _hgrn_kernel(q_ref, zf_ref, zb_ref, i_ref, lb_ref, o_ref, st_ref, kst, fst, vst, sst, pbuf, *, n_ctx, seq):
    ch = SCAN_CHUNK
    n_chunks = seq // ch
    ctx_chunks = n_ctx // ch
    lb = lb_ref[...]
    lb_floor = jnp.maximum(lb, LB_FLOOR)
    gate_consts = (lb_floor, 1.0 - lb, lb_floor - lb)
    hi = lax.broadcasted_iota(jnp.int32, (GROUP_W, GROUP_W), 0) // HEAD_DIM
    hj = lax.broadcasted_iota(jnp.int32, (GROUP_W, GROUP_W), 1) // HEAD_DIM
    bd_mask = jnp.where(hi == hj, 1.0, 0.0).astype(F32)
    ones_bd = bd_mask.astype(BF16)

    o_ref[...] = jnp.zeros(o_ref.shape, F32)
    st_ref[...] = jnp.zeros(st_ref.shape, F32)
    kst[...] = jnp.zeros(kst.shape, F32)
    fst[...] = jnp.zeros(fst.shape, F32)
    vst[...] = jnp.zeros(vst.shape, F32)
    sst[...] = jnp.ones(sst.shape, F32)

    def body(s, carry):
        cf = s
        cb = jnp.where(s < ctx_chunks, ctx_chunks - 1 - s, n_chunks - 1 - (s - ctx_chunks))
        for d, c, z_ref in ((0, cf, zf_ref), (1, cb, zb_ref)):
            base = pl.multiple_of(c * ch, ch)
            _hgrn_chunk(d, base, q_ref, z_ref, i_ref, gate_consts, o_ref, st_ref, kst, fst, vst, sst, pbuf,
                        ones_bd, bd_mask)
        return carry

    lax.fori_loop(0, n_chunks, body, 0)


def _hgrn(proj, lower_bound, n_ctx):
    b, s, _ = proj.shape
    ch = SCAN_CHUNK
    kernel = functools.partial(_hgrn_kernel, n_ctx=n_ctx, seq=s)
    col = lambda c: pl.BlockSpec((1, s, GROUP_W), lambda i: (i, 0, c))
    return pl.pallas_call(
        kernel,
        grid=(b,),
        in_specs=[col(8), col(9), col(10), col(11), pl.BlockSpec((1, GROUP_W), lambda i: (0, 0))],
        out_specs=pl.BlockSpec((1, s, GROUP_W), lambda i: (i, 0, 0)),
        out_shape=jax.ShapeDtypeStruct((b, s, GROUP_W), F32),
        scratch_shapes=[
            pltpu.VMEM((2, GROUP_W, GROUP_W), F32),
            pltpu.VMEM((2, 3 * ch, GROUP_W), F32),
            pltpu.VMEM((2, 3 * ch, GROUP_W), F32),
            pltpu.VMEM((2, 3 * ch, GROUP_W), F32),
            pltpu.VMEM((2, 3 * ch, GROUP_W), F32),
            pltpu.VMEM((2, OFFSET_GROUP * ch, GROUP_W), BF16),
        ],
        compiler_params=_cparams(1),
        name="hgrn2_bidirectional",
    )(proj, proj, proj, proj, lower_bound.reshape(1, GROUP_W))


def _post_kernel(ya_ref, yb_ref, yc_ref, od_ref, dg_ref, x_ref, mod_ref, gn_ref, wo_ref, nf_ref, wq_ref,
                 k1_ref, k2_ref, xo_ref, h_ref, sc_ref):
    yd = od_ref[0] * _silu(dg_ref[0])
    gn = gn_ref[...]
    parts = (ya_ref[0], yb_ref[0], yc_ref[0], yd)
    normed = [_rms(p) * gn[:, i * GROUP_W:(i + 1) * GROUP_W] for i, p in enumerate(parts)]
    y = _dot(jnp.concatenate(normed, axis=1).astype(BF16), wo_ref[...])
    xn = x_ref[0] + mod_ref[0, 2:3, :] * y
    xo_ref[0] = xn
    h = _rms(xn) * nf_ref[...] * (1.0 + mod_ref[0, 4:5, :]) + mod_ref[0, 3:4, :]
    h_ref[0] = h
    qv = _dot(h.astype(BF16), wq_ref[...])
    half = PEER_NKEYS
    for g in range(2 * PEER_HEADS):
        keys = k1_ref[...] if g % 2 == 0 else k2_ref[...]
        sc_ref[g] = _dot_nt(keys, qv[:, g * half:(g + 1) * half].astype(BF16))


def _post_mixer(ya, yb, yc, od, proj, xc, mods, gnorm, wo_bf16, nffn, wq_bf16, k1_bf16, k2_bf16, n_ctx_tiles, tile0):
    b, s, d = xc.shape
    tm = ROW_TILE
    nt = s // tm - tile0
    s_out = nt * tm
    ctx_cond = mods.shape[0] - 1
    nq = wq_bf16.shape[1]
    att0 = tile0 - (s - ya.shape[1]) // tm
    att = lambda: pl.BlockSpec((1, tm, GROUP_W), lambda i, j: (i, j + att0, 0))
    grp = lambda: pl.BlockSpec((1, tm, GROUP_W), lambda i, j: (i, j + tile0, 0))
    const = lambda shape: pl.BlockSpec(shape, lambda i, j: (0,) * len(shape))
    return pl.pallas_call(
        _post_kernel,
        grid=(b, nt),
        in_specs=[
            att(), att(), grp(), grp(),
            pl.BlockSpec((1, tm, GROUP_W), lambda i, j: (i, j + tile0, 12)),
            pl.BlockSpec((1, tm, d), lambda i, j: (i, j + tile0, 0)),
            pl.BlockSpec((1, N_MOD, d), lambda i, j: (jnp.where(j + tile0 < n_ctx_tiles, ctx_cond, i), 0, 0)),
            const((1, d)), const((d, d)), const((1, d)), const((d, nq)),
            const((PEER_NKEYS, PEER_NKEYS)), const((PEER_NKEYS, PEER_NKEYS)),
        ],
        out_specs=[
            pl.BlockSpec((1, tm, d), lambda i, j: (i, j, 0)),
            pl.BlockSpec((1, tm, d), lambda i, j: (i, j, 0)),
            pl.BlockSpec((2 * PEER_HEADS, PEER_NKEYS, tm), lambda i, j: (0, 0, i * nt + j)),
        ],
        out_shape=[
            jax.ShapeDtypeStruct((b, s_out, d), F32),
            jax.ShapeDtypeStruct((b, s_out, d), F32),
            jax.ShapeDtypeStruct((2 * PEER_HEADS, PEER_NKEYS, b * s_out), F32),
        ],
        compiler_params=_cparams(2),
        name="post_mixer",
    )(ya, yb, yc, od, proj, xc, mods, gnorm.reshape(1, d), wo_bf16, nffn.reshape(1, d), wq_bf16, k1_bf16, k2_bf16)


def _top16(s, rows=None):
    if rows is None:
        rows = lax.broadcasted_iota(jnp.int32, s.shape, 0).astype(F32)
    vals, ids = [], []
    cur = s
    for _ in range(PEER_TOPK):
        m = jnp.max(cur, axis=0, keepdims=True)
        am = jnp.min(jnp.where(cur == m, rows, jnp.inf), axis=0, keepdims=True)
        vals.append(m)
        ids.append(am)
        cur = jnp.where(rows == am, -jnp.inf, cur)
    return jnp.concatenate(vals, axis=0), jnp.concatenate(ids, axis=0)


def _candidate_rows(a):
    return PEER_TOPK // (a + 1)


def _pick(table, sel):
    out = jnp.zeros(sel.shape, table.dtype)
    for a in range(PEER_TOPK):
        out = out + jnp.where(sel == float(a), table[a:a + 1, :], 0.0)
    return out


def _route_kernel(sc_ref, idx_ref, gate_ref):
    ids, gates = [], []
    for h in range(PEER_HEADS):
        v1, i1 = _top16(sc_ref[2 * h])
        v2, i2 = _top16(sc_ref[2 * h + 1])
        nb = [_candidate_rows(a) for a in range(PEER_TOPK)]
        cand = jnp.concatenate([v1[a:a + 1, :] + v2[0:nb[a], :] for a in range(PEER_TOPK)], axis=0)
        flat = jnp.concatenate([lax.broadcasted_iota(jnp.int32, (nb[a], v1.shape[1]), 0).astype(F32)
                                + float(a * PEER_TOPK) for a in range(PEER_TOPK)], axis=0)
        top_s, pos = _top16(cand, flat)
        a_sel = jnp.floor(pos * (1.0 / PEER_TOPK))
        b_sel = pos - a_sel * PEER_TOPK
        ids.append(_pick(i1, a_sel) * PEER_NKEYS + _pick(i2, b_sel))
        e = jnp.exp(top_s - jnp.max(top_s, axis=0, keepdims=True))
        gates.append(e / jnp.sum(e, axis=0, keepdims=True))
    idx_ref[...] = jnp.concatenate(ids, axis=0).T.astype(jnp.int32)
    gate_ref[...] = jnp.concatenate(gates, axis=0).T


def _routing(scores_t):
    g, nk, t = scores_t.shape
    tt = TOPK_TOKENS
    per_tok = PEER_HEADS * PEER_TOPK
    out_spec = pl.BlockSpec((tt, per_tok), lambda i: (i, 0))
    return pl.pallas_call(
        _route_kernel,
        grid=(t // tt,),
        in_specs=[pl.BlockSpec((g, nk, tt), lambda i: (0, 0, i))],
        out_specs=[out_spec, out_spec],
        out_shape=[jax.ShapeDtypeStruct((t, per_tok), jnp.int32),
                   jax.ShapeDtypeStruct((t, per_tok), F32)],
        compiler_params=_cparams(1),
        name="peer_routing",
    )(scores_t)


def _gelu_tanh(x):
    return 0.5 * x * (1.0 + jnp.tanh(math.sqrt(2.0 / math.pi) * (x + 0.044715 * (x * x * x))))


FOLD = (8, 128)


def _sublane_pair_sum(x, y, k):
    sub = lax.broadcasted_iota(jnp.int32, FOLD, 0)
    keep = (sub % (2 * k)) < k
    return jnp.where(keep, x, pltpu.roll(y, k, 0)) + jnp.where(keep, pltpu.roll(x, FOLD[0] - k, 0), y)


def _sublane_sums(p):
    z = [_sublane_pair_sum(p[j], p[j + 4], 4) for j in range(4)]
    w = [_sublane_pair_sum(z[0], z[2], 2), _sublane_pair_sum(z[1], z[3], 2)]
    return _sublane_pair_sum(w[0], w[1], 1)


LAND_SLOTS = PEER_TOKENS * PEER_HEADS * PEER_TOPK
COPY_GROUP = 64
REC = 3 * LAND_SLOTS


def _peer_kernel(rec_hbm, tbl_hbm, h_ref, gate_ref, x_ref, mod_ref, fn_ref, o_ref, rec_smem, tbl, act_rep,
                 tbl_sem, rec_sem, land_sem, *, n_steps, n_res, final):
    j = pl.program_id(0)
    tb = PEER_TOKENS
    per_tok = PEER_HEADS * PEER_TOPK
    sub = FOLD[0]

    def rec_copy(r, region):
        return pltpu.make_async_copy(rec_hbm.at[pl.ds(r * REC, REC)],
                                     rec_smem.at[pl.ds(region * REC, REC)], rec_sem.at[region])

    def issue_next(region, buf):
        n_groups = rec_smem[region * REC + 2 * LAND_SLOTS]

        def body(g, carry):
            for k in range(COPY_GROUP):
                e = rec_smem[region * REC + LAND_SLOTS + g * COPY_GROUP + k]
                pltpu.make_async_copy(tbl_hbm.at[e], tbl.at[n_res + buf * LAND_SLOTS + g * COPY_GROUP + k],
                                      land_sem.at[buf]).start(priority=k % 2)
            return carry
        lax.fori_loop(0, n_groups, body, 0)

    def wait_landing(region, buf):
        n_groups = rec_smem[region * REC + 2 * LAND_SLOTS + 1]

        def body(g, carry):
            grp = tbl.at[pl.ds(n_res + buf * LAND_SLOTS, COPY_GROUP)]
            pltpu.make_async_copy(grp, grp, land_sem.at[buf]).wait()
            return carry
        lax.fori_loop(0, n_groups, body, 0)

    g2 = mod_ref[0, 5]
    hi_mask = jnp.uint32(0xFFFF0000)

    def consume(half):
        for t in range(tb):
            tok = half * tb + t
            hf = h_ref[tok]
            rows = [rec_smem[half * REC + t * per_tok + e] for e in range(per_tok)]
            parts = []
            for g in range(per_tok // sub):
                prods = []
                for k in range(sub):
                    word = tbl[rows[g * sub + k]]
                    prods.append(pltpu.bitcast(word << 16, F32) * hf)
                parts.append(_sublane_sums(prods))
            s_row = jnp.sum(jnp.concatenate(parts, axis=0).T, axis=0, keepdims=True)
            act_row = _gelu_tanh(s_row) * gate_ref[tok:tok + 1, :]
            act_rep[tok] = jnp.broadcast_to(act_row, (FOLD[1], per_tok)).T
            accs = [jnp.zeros(FOLD, F32) for _ in range(4)]
            for e in range(per_tok):
                a = jnp.broadcast_to(act_rep[tok, e:e + 1, :], FOLD)
                v = pltpu.bitcast(tbl[rows[e]] & hi_mask, F32)
                accs[e % 4] = accs[e % 4] + a * v
            xn = x_ref[tok] + g2 * ((accs[0] + accs[1]) + (accs[2] + accs[3]))
            if final:
                ms = (jnp.sum(jnp.sum(xn * xn, axis=1, keepdims=True), axis=0, keepdims=True)
                      * (1.0 / (FOLD[0] * FOLD[1])))
                xn = xn * lax.rsqrt(ms + EPS) * fn_ref[...]
            o_ref[tok] = xn

    @pl.when(j == 0)
    def _():
        whole = pltpu.make_async_copy(tbl_hbm.at[pl.ds(0, n_res)], tbl.at[pl.ds(0, n_res)], tbl_sem)
        whole.start()
        first = rec_copy(0, 1)
        first.start()
        first.wait()
        issue_next(1, 0)
        rec_copy(1, 0).start()
        rec_copy(2, 1).start()
        whole.wait()

    more = j + 1 < n_steps
    rec_copy(2 * j + 1, 0).wait()
    issue_next(0, 1)
    wait_landing(0, 0)
    consume(0)

    @pl.when(more)
    def _():
        rec_copy(2 * j + 3, 0).start()

    rec_copy(2 * j + 2, 1).wait()
    issue_next(1, 0)
    wait_landing(1, 1)
    consume(1)

    @pl.when(more)
    def _():
        rec_copy(2 * j + 4, 1).start()


def _peer_records(idx, n_res, n_exp):
    blocks = idx.reshape(-1, LAND_SLOTS)
    nb = blocks.shape[0]
    away = blocks >= n_res
    rank = jnp.cumsum(away.astype(jnp.int32), axis=1) - 1
    n_groups = (jnp.sum(away.astype(jnp.int32), axis=1) + COPY_GROUP - 1) // COPY_GROUP
    buf = (jnp.arange(nb, dtype=jnp.int32) % 2)[:, None]
    rows = jnp.where(away, n_res + buf * LAND_SLOTS + rank, blocks)
    id_bits = (n_exp - 1).bit_length()
    pos_bits = (LAND_SLOTS - 1).bit_length()
    assert id_bits + pos_bits < 31
    pos = jnp.arange(LAND_SLOTS, dtype=jnp.int32)[None, :]
    packed = (jnp.where(away, 0, 1 << (id_bits + pos_bits)) | (pos << id_bits) | blocks).astype(jnp.int32)
    copy_list = lax.sort(packed, dimension=1) & ((1 << id_bits) - 1)
    zero_row = jnp.zeros((1, LAND_SLOTS), jnp.int32)
    counts = jnp.zeros((nb + 1, LAND_SLOTS), jnp.int32)
    counts = counts.at[:nb, 0].set(n_groups).at[1:, 1].set(n_groups)
    rec = jnp.concatenate([jnp.concatenate([zero_row, rows], axis=0),
                           jnp.concatenate([copy_list, zero_row], axis=0), counts], axis=1)
    return rec.reshape(-1)


def _peer_experts(h2, idx, gate, uv, x_mid, mods, final_gain, s_out, ctx_rows, final):
    t = h2.shape[0]
    tb = 2 * PEER_TOKENS
    per_tok = PEER_HEADS * PEER_TOPK
    n_steps = t // tb
    n_res = min(uv.shape[0], PEER_RESIDENT_BYTES // (4 * FOLD[0] * FOLD[1]))
    assert t % tb == 0 and s_out % tb == 0 and ctx_rows % tb == 0
    ctx_cond = mods.shape[0] - 1

    def mod_index(i):
        tok = i * tb
        return (jnp.where(tok % s_out < ctx_rows, ctx_cond, tok // s_out), 0, 0, 0)

    kernel = functools.partial(_peer_kernel, n_steps=n_steps, n_res=n_res, final=final)
    fold_spec = pl.BlockSpec((tb,) + FOLD, lambda i: (i, 0, 0))
    return pl.pallas_call(
        kernel,
        grid=(n_steps,),
        in_specs=[
            pl.BlockSpec(memory_space=pl.ANY),
            pl.BlockSpec(memory_space=pl.ANY),
            fold_spec,
            pl.BlockSpec((tb, per_tok), lambda i: (i, 0)),
            fold_spec,
            pl.BlockSpec((1, N_MOD) + FOLD, mod_index),
            pl.BlockSpec(FOLD, lambda i: (0, 0)),
        ],
        out_specs=fold_spec,
        out_shape=jax.ShapeDtypeStruct((t,) + FOLD, F32),
        scratch_shapes=[
            pltpu.SMEM((2 * REC,), jnp.int32),
            pltpu.VMEM((n_res + 2 * LAND_SLOTS,) + FOLD, jnp.uint32),
            pltpu.VMEM((2 * PEER_TOKENS, per_tok, FOLD[1]), F32),
            pltpu.SemaphoreType.DMA,
            pltpu.SemaphoreType.DMA((2,)),
            pltpu.SemaphoreType.DMA((2,)),
        ],
        compiler_params=_cparams(1, PEER_VMEM_LIMIT),
        name="peer_experts",
    )(_peer_records(idx, n_res, uv.shape[0]), uv, h2, gate, x_mid, mods.reshape(mods.shape[0], N_MOD, *FOLD),
      final_gain.reshape(FOLD))


def _pack_experts(u, v):
    n_exp = u.shape[0]
    ub = lax.bitcast_convert_type(u.astype(BF16), jnp.uint16).astype(jnp.uint32)
    vb = lax.bitcast_convert_type(v.astype(BF16), jnp.uint16).astype(jnp.uint32)
    return ((vb << 16) | ub).reshape(n_exp, *FOLD)


def _rope_tables(n_ctx, n_lat):
    t = jnp.arange(n_lat)
    inv = ROPE_THETA ** (-jnp.arange(0, AXIS_DIM, 2, dtype=F32) / AXIS_DIM)
    row = (t // GRID_W).astype(F32)[:, None] * inv
    col = (t % GRID_W).astype(F32)[:, None] * inv
    cos = jnp.concatenate([jnp.cos(row), jnp.cos(row), jnp.cos(col), jnp.cos(col)], axis=1)
    sin = jnp.concatenate([-jnp.sin(row), jnp.sin(row), -jnp.sin(col), jnp.sin(col)], axis=1)
    cos = jnp.concatenate([jnp.ones((n_ctx, HEAD_DIM), F32), cos], axis=0)
    sin = jnp.concatenate([jnp.zeros((n_ctx, HEAD_DIM), F32), sin], axis=0)
    return jnp.tile(cos, (1, A_HEADS)), jnp.tile(sin, (1, A_HEADS))


def kernel(x, c, ctx, c_ctx, w_mod, b_mod, norm_mix, norm_ffn, w_in, conv_w, attn_sink, na_rpb, lb_logits, group_norm, w_out, peer_wq, peer_k1, peer_k2, peer_u, peer_v, final_norm):
    b, n_lat, d = x.shape
    n_ctx = ctx.shape[1]
    depth = w_mod.shape[0]
    seq = n_ctx + n_lat
    assert n_ctx % ROW_TILE == 0 and n_lat % ROW_TILE == 0 and n_lat // GRID_W >= NB_ROWS
    assert n_lat >= WIN_BLOCK + 2 * WINDOW and w_in.shape[2] == IN_W
    n_ctx_tiles = n_ctx // ROW_TILE

    n_cond = b + 1
    pad = (-n_cond) % 8
    cvec = jnp.concatenate([c, c_ctx[None, :], jnp.zeros((pad, d), F32)], axis=0)
    mods = _modulation(cvec, w_mod, b_mod)[:, :n_cond].reshape(depth, n_cond, N_MOD, d)

    lb_soft = jax.nn.softmax(lb_logits.astype(F32), axis=0)
    lower_bounds = jnp.cumsum(lb_soft, axis=0) - lb_soft[0:1]
    cos_t, sin_t = _rope_tables(n_ctx, n_lat)

    xc = jnp.concatenate([ctx, x], axis=1)
    out = None
    for l in range(depth):
        last = l == depth - 1
        proj = _in_projection(xc, mods[l], norm_mix[l], w_in[l].astype(BF16), n_ctx_tiles)
        ya = _attention_a(proj, cos_t, sin_t, attn_sink[l], n_ctx, not last)
        yb = _attention_b(proj, _neighbourhood_bias(na_rpb[l]), n_ctx, not last)
        yc = _short_conv(proj, conv_w[l], n_ctx)
        od = _hgrn(proj, lower_bounds[l], n_ctx)
        tile0 = n_ctx_tiles if last else 0
        x_mid, h2, scores_t = _post_mixer(ya, yb, yc, od, proj, xc, mods[l], group_norm[l], w_out[l].astype(BF16),
                                          norm_ffn[l], peer_wq[l].astype(BF16), peer_k1[l].astype(BF16),
                                          peer_k2[l].astype(BF16), n_ctx_tiles, tile0)
        s_out = x_mid.shape[1]
        tokens = b * s_out
        idx, gate = _routing(scores_t)
        idx = idx.reshape(-1)
        uv = _pack_experts(peer_u[l], peer_v[l])
        res = _peer_experts(h2.reshape(tokens, *FOLD), idx, gate, uv, x_mid.reshape(tokens, *FOLD), mods[l],
                            final_norm, s_out, 0 if last else n_ctx, last)
        if last:
            out = res.reshape(b, s_out, d)
        else:
            xc = res.reshape(b, s_out, d)
    return out
```

```python
import functools
import math

import jax
import jax.numpy as jnp
import numpy as np
from jax import lax
from jax.experimental import pallas as pl
from jax.experimental.pallas import tpu as pltpu

F32 = jnp.float32
BF16 = jnp.bfloat16

GRID_W = 64
HEAD_DIM = 64
GROUP_W = 256
A_HEADS = 4
A_KV_HEADS = 2
WINDOW = 128
WIN_BLOCK = 128
B_HEADS = 4
NB_ROWS = 8
NB_COLS = 16
CONV_W = 3
D_HEADS = 4
SCAN_CHUNK = 64
LB_FLOOR = 1e-20
ROPE_THETA = 10000.0
AXIS_DIM = HEAD_DIM // 2
PEER_HEADS = 8
PEER_NKEYS = 128
PEER_TOPK = 16
N_MOD = 6
EPS = 1e-6
MASK_VALUE = -1e30
IN_W = 13 * GROUP_W

ROW_TILE = 256
MOD_COL_TILE = 1536
TOPK_TOKENS = 128
PEER_TOKENS = 8
VMEM_LIMIT = 48 * 1024 * 1024


PEER_VMEM_LIMIT = 61 * 1024 * 1024
PEER_RESIDENT_BYTES = 47 * 1024 * 1024


def _cparams(n_axes, vmem_limit=VMEM_LIMIT):
    return pltpu.CompilerParams(dimension_semantics=("arbitrary",) * n_axes, vmem_limit_bytes=vmem_limit)


def _dot(a, b):
    return jnp.dot(a, b, preferred_element_type=F32)


def _dot_nt(a, b):
    return lax.dot_general(a, b, (((1,), (1,)), ((), ())), preferred_element_type=F32)


def _dot_tn(a, b):
    return lax.dot_general(a, b, (((0,), (0,)), ((), ())), preferred_element_type=F32)


def _rms(x):
    return x * lax.rsqrt(jnp.mean(x * x, axis=-1, keepdims=True) + EPS)


def _silu(x):
    return x / (1.0 + jnp.exp(-x))


def _mod_kernel(c_ref, w_ref, b_ref, o_ref):
    s = _silu(c_ref[...])
    o_ref[0] = _dot(s.astype(BF16), w_ref[0].astype(BF16)) + b_ref[0]


def _modulation(cvec, w_mod, b_mod):
    depth, d, n = w_mod.shape
    r = cvec.shape[0]
    tn = MOD_COL_TILE
    return pl.pallas_call(
        _mod_kernel,
        grid=(depth, n // tn),
        in_specs=[
            pl.BlockSpec((r, d), lambda l, j: (0, 0)),
            pl.BlockSpec((1, d, tn), lambda l, j: (l, 0, j)),
            pl.BlockSpec((1, 1, tn), lambda l, j: (l, 0, j)),
        ],
        out_specs=pl.BlockSpec((1, r, tn), lambda l, j: (l, 0, j)),
        out_shape=jax.ShapeDtypeStruct((depth, r, n), F32),
        compiler_params=_cparams(2),
        name="adaln_modulation",
    )(cvec, w_mod, b_mod.reshape(depth, 1, n))


def _inproj_kernel(x_ref, mod_ref, g_ref, w_ref, o_ref):
    y = _rms(x_ref[0]) * g_ref[...]
    h = y * (1.0 + mod_ref[0, 1:2, :]) + mod_ref[0, 0:1, :]
    o_ref[0] = _dot(h.astype(BF16), w_ref[...])


def _in_projection(xc, mods, gain, w_bf16, n_ctx_tiles):
    b, s, d = xc.shape
    n = w_bf16.shape[1]
    tm = ROW_TILE
    ctx_cond = mods.shape[0] - 1
    return pl.pallas_call(
        _inproj_kernel,
        grid=(b, s // tm),
        in_specs=[
            pl.BlockSpec((1, tm, d), lambda i, j: (i, j, 0)),
            pl.BlockSpec((1, N_MOD, d), lambda i, j: (jnp.where(j < n_ctx_tiles, ctx_cond, i), 0, 0)),
            pl.BlockSpec((1, d), lambda i, j: (0, 0)),
            pl.BlockSpec((d, n), lambda i, j: (0, 0)),
        ],
        out_specs=pl.BlockSpec((1, tm, n), lambda i, j: (i, j, 0)),
        out_shape=jax.ShapeDtypeStruct((b, s, n), F32),
        compiler_params=_cparams(2),
        name="in_projection",
    )(xc, mods, gain.reshape(1, d), w_bf16)


def _rope(x, cos, sin_signed):
    n = x.shape[1]
    lane = lax.broadcasted_iota(jnp.int32, x.shape, 1)
    fwd = pltpu.roll(x, n - AXIS_DIM // 2, 1)
    bwd = pltpu.roll(x, AXIS_DIM // 2, 1)
    swapped = jnp.where((lane % AXIS_DIM) < AXIS_DIM // 2, fwd, bwd)
    return x * cos + swapped * sin_signed


def _attn_a_kernel(sink_ref, q_ref, k_ref, v_ref, cq_ref, sq_ref, ck_ref, sk_ref, o_ref, kh_ref, vh_ref,
                   *, n_ctx, n_lat, ctx_steps):
    t = pl.program_id(1)
    span = WIN_BLOCK + 2 * WINDOW
    g = A_HEADS // A_KV_HEADS

    @pl.when(t == 0)
    def _():
        kr = _rope(k_ref[0], ck_ref[...], sk_ref[...])
        v = v_ref[0]
        for h in range(A_KV_HEADS):
            kh_ref[h] = kr[:, h * HEAD_DIM:(h + 1) * HEAD_DIM].astype(BF16)
            vh_ref[h] = v[:, h * HEAD_DIM:(h + 1) * HEAD_DIM].astype(BF16)

    q = _rope(q_ref[0], cq_ref[...], sq_ref[...]) * (HEAD_DIM ** -0.5)
    row = lax.broadcasted_iota(jnp.int32, (g * WIN_BLOCK, 1), 0)

    def q_pair(hk):
        parts = [q[:, (hk * g + j) * HEAD_DIM:(hk * g + j + 1) * HEAD_DIM] for j in range(g)]
        return jnp.concatenate(parts, axis=0).astype(BF16)

    def sink_col(hk):
        col = jnp.full((g * WIN_BLOCK, 1), sink_ref[hk * g], F32)
        for j in range(1, g):
            col = jnp.where(row >= j * WIN_BLOCK, sink_ref[hk * g + j], col)
        return col

    def assemble(outs):
        pieces = []
        for hk in range(A_KV_HEADS):
            for j in range(g):
                pieces.append(outs[hk][j * WIN_BLOCK:(j + 1) * WIN_BLOCK])
        return jnp.concatenate(pieces, axis=1)

    def latent_block():
        p0 = (t - ctx_steps) * WIN_BLOCK
        ws = jnp.clip(p0 - WINDOW, 0, n_lat - span)
        kstart = pl.multiple_of(n_ctx + ws, WIN_BLOCK)
        qpos = p0 + row % WIN_BLOCK
        kpos = ws + lax.broadcasted_iota(jnp.int32, (1, span), 1)
        mask = jnp.abs(qpos - kpos) <= WINDOW
        outs = []
        for hk in range(A_KV_HEADS):
            q2 = q_pair(hk)
            s_loc = _dot_nt(q2, kh_ref[hk, pl.ds(kstart, span), :])
            s_loc = jnp.where(mask, s_loc, MASK_VALUE)
            s_ctx = _dot_nt(q2, kh_ref[hk, 0:n_ctx, :])
            sk = sink_col(hk)
            m = jnp.maximum(jnp.maximum(jnp.max(s_loc, axis=1, keepdims=True),
                                        jnp.max(s_ctx, axis=1, keepdims=True)), sk)
            p_loc = jnp.exp(s_loc - m)
            p_ctx = jnp.exp(s_ctx - m)
            den = (jnp.sum(p_loc, axis=1, keepdims=True) + jnp.sum(p_ctx, axis=1, keepdims=True)
                   + jnp.exp(sk - m))
            o = (_dot(p_loc.astype(BF16), vh_ref[hk, pl.ds(kstart, span), :])
                 + _dot(p_ctx.astype(BF16), vh_ref[hk, 0:n_ctx, :]))
            outs.append(o / den)
        o_ref[0] = assemble(outs)

    def ctx_block():
        outs = []
        for hk in range(A_KV_HEADS):
            q2 = q_pair(hk)
            s_ctx = _dot_nt(q2, kh_ref[hk, 0:n_ctx, :])
            sk = sink_col(hk)
            m = jnp.maximum(jnp.max(s_ctx, axis=1, keepdims=True), sk)
            p_ctx = jnp.exp(s_ctx - m)
            den = jnp.sum(p_ctx, axis=1, keepdims=True) + jnp.exp(sk - m)
            outs.append(_dot(p_ctx.astype(BF16), vh_ref[hk, 0:n_ctx, :]) / den)
        o_ref[0] = assemble(outs)

    if ctx_steps:
        pl.when(t < ctx_steps)(ctx_block)
        pl.when(t >= ctx_steps)(latent_block)
    else:
        latent_block()


def _attention_a(proj, cos_t, sin_t, sink, n_ctx, with_ctx):
    b, s, _ = proj.shape
    n_lat = s - n_ctx
    ctx_blocks = n_ctx // WIN_BLOCK
    ctx_steps = ctx_blocks if with_ctx else 0
    off = ctx_blocks - ctx_steps
    steps = ctx_steps + n_lat // WIN_BLOCK
    kw = A_KV_HEADS * HEAD_DIM
    kernel = functools.partial(_attn_a_kernel, n_ctx=n_ctx, n_lat=n_lat, ctx_steps=ctx_steps)
    return pl.pallas_call(
        kernel,
        grid=(b, steps),
        in_specs=[
            pl.BlockSpec(memory_space=pltpu.SMEM),
            pl.BlockSpec((1, WIN_BLOCK, GROUP_W), lambda i, t: (i, t + off, 0)),
            pl.BlockSpec((1, s, kw), lambda i, t: (i, 0, 2)),
            pl.BlockSpec((1, s, kw), lambda i, t: (i, 0, 3)),
            pl.BlockSpec((WIN_BLOCK, GROUP_W), lambda i, t: (t + off, 0)),
            pl.BlockSpec((WIN_BLOCK, GROUP_W), lambda i, t: (t + off, 0)),
            pl.BlockSpec((s, kw), lambda i, t: (0, 0)),
            pl.BlockSpec((s, kw), lambda i, t: (0, 0)),
        ],
        out_specs=pl.BlockSpec((1, WIN_BLOCK, GROUP_W), lambda i, t: (i, t, 0)),
        out_shape=jax.ShapeDtypeStruct((b, steps * WIN_BLOCK, GROUP_W), F32),
        scratch_shapes=[pltpu.VMEM((A_KV_HEADS, s, HEAD_DIM), BF16), pltpu.VMEM((A_KV_HEADS, s, HEAD_DIM), BF16)],
        compiler_params=_cparams(2),
        name="window_attention",
    )(sink, proj, proj, proj, cos_t, sin_t, cos_t, sin_t)


def _attn_b_kernel(q_ref, k_ref, v_ref, bias_ref, o_ref, kh_ref, vh_ref, *, n_ctx, n_lat, ctx_steps):
    t = pl.program_id(1)
    rows = n_lat // GRID_W
    nkeys = NB_ROWS * GRID_W

    @pl.when(t == 0)
    def _():
        k = k_ref[0]
        v = v_ref[0]
        for h in range(B_HEADS):
            kh_ref[h] = k[:, h * HEAD_DIM:(h + 1) * HEAD_DIM].astype(BF16)
            vh_ref[h] = v[:, h * HEAD_DIM:(h + 1) * HEAD_DIM].astype(BF16)

    q = q_ref[0] * (HEAD_DIM ** -0.5)

    def latent_block():
        r = t - ctx_steps
        r0 = jnp.clip(r - NB_ROWS // 2, 0, rows - NB_ROWS)
        kstart = pl.multiple_of(n_ctx + r0 * GRID_W, GRID_W)
        outs = []
        for h in range(B_HEADS):
            qh = q[:, h * HEAD_DIM:(h + 1) * HEAD_DIM].astype(BF16)
            s_loc = _dot_nt(qh, kh_ref[h, pl.ds(kstart, nkeys), :]) + bias_ref[0, h]
            s_ctx = _dot_nt(qh, kh_ref[h, 0:n_ctx, :])
            m = jnp.maximum(jnp.max(s_loc, axis=1, keepdims=True), jnp.max(s_ctx, axis=1, keepdims=True))
            p_loc = jnp.exp(s_loc - m)
            p_ctx = jnp.exp(s_ctx - m)
            den = jnp.sum(p_loc, axis=1, keepdims=True) + jnp.sum(p_ctx, axis=1, keepdims=True)
            o = (_dot(p_loc.astype(BF16), vh_ref[h, pl.ds(kstart, nkeys), :])
                 + _dot(p_ctx.astype(BF16), vh_ref[h, 0:n_ctx, :]))
            outs.append(o / den)
        o_ref[0] = jnp.concatenate(outs, axis=1)

    def ctx_block():
        outs = []
        for h in range(B_HEADS):
            qh = q[:, h * HEAD_DIM:(h + 1) * HEAD_DIM].astype(BF16)
            s_ctx = _dot_nt(qh, kh_ref[h, 0:n_ctx, :])
            m = jnp.max(s_ctx, axis=1, keepdims=True)
            p_ctx = jnp.exp(s_ctx - m)
            den = jnp.sum(p_ctx, axis=1, keepdims=True)
            outs.append(_dot(p_ctx.astype(BF16), vh_ref[h, 0:n_ctx, :]) / den)
        o_ref[0] = jnp.concatenate(outs, axis=1)

    if ctx_steps:
        pl.when(t < ctx_steps)(ctx_block)
        pl.when(t >= ctx_steps)(latent_block)
    else:
        latent_block()


def _attention_b(proj, bias, n_ctx, with_ctx):
    b, s, _ = proj.shape
    n_lat = s - n_ctx
    rows = n_lat // GRID_W
    ctx_blocks = n_ctx // GRID_W
    ctx_steps = ctx_blocks if with_ctx else 0
    off = ctx_blocks - ctx_steps
    steps = ctx_steps + rows

    def bias_index(i, t):
        r = jnp.maximum(t - ctx_steps, 0)
        r0 = jnp.clip(r - NB_ROWS // 2, 0, rows - NB_ROWS)
        return (r - r0, 0, 0, 0)

    kernel = functools.partial(_attn_b_kernel, n_ctx=n_ctx, n_lat=n_lat, ctx_steps=ctx_steps)
    return pl.pallas_call(
        kernel,
        grid=(b, steps),
        in_specs=[
            pl.BlockSpec((1, GRID_W, GROUP_W), lambda i, t: (i, t + off, 2)),
            pl.BlockSpec((1, s, GROUP_W), lambda i, t: (i, 0, 3)),
            pl.BlockSpec((1, s, GROUP_W), lambda i, t: (i, 0, 4)),
            pl.BlockSpec((1, B_HEADS, GRID_W, NB_ROWS * GRID_W), bias_index),
        ],
        out_specs=pl.BlockSpec((1, GRID_W, GROUP_W), lambda i, t: (i, t, 0)),
        out_shape=jax.ShapeDtypeStruct((b, steps * GRID_W, GROUP_W), F32),
        scratch_shapes=[pltpu.VMEM((B_HEADS, s, HEAD_DIM), BF16), pltpu.VMEM((B_HEADS, s, HEAD_DIM), BF16)],
        compiler_params=_cparams(2),
        name="neighbourhood_attention",
    )(proj, proj, proj, bias)


def _neighbourhood_bias(rpb):
    qc = np.arange(GRID_W)[:, None]
    kc = np.arange(GRID_W)[None, :]
    win0 = np.clip(qc - NB_COLS // 2, 0, GRID_W - NB_COLS)
    ok = (kc >= win0) & (kc < win0 + NB_COLS)
    d_col = np.clip(kc - qc, -(NB_COLS - 1), NB_COLS - 1) + (NB_COLS - 1)
    onehot = (d_col[None] == np.arange(2 * NB_COLS - 1)[:, None, None]).astype(np.float32)
    by_rel = jnp.stack([rpb[:, NB_ROWS - 1 - rel:2 * NB_ROWS - 1 - rel, :] for rel in range(NB_ROWS)], axis=0)
    tab = jnp.einsum('rhkc,cqj->rhqkj', by_rel.astype(F32), jnp.asarray(onehot), precision=lax.Precision.HIGHEST)
    tab = jnp.where(ok[None, None, :, None, :], tab, MASK_VALUE)
    return tab.reshape(NB_ROWS, B_HEADS, GRID_W, NB_ROWS * GRID_W)


def _conv_kernel(cx_ref, cb_ref, cc_ref, w_ref, o_ref, pad_ref, *, n_ctx, seq):
    edge = 8
    chunk = ROW_TILE
    pad_ref[0:edge, :] = jnp.zeros((edge, GROUP_W), F32)
    pad_ref[edge + seq:2 * edge + seq, :] = jnp.zeros((edge, GROUP_W), F32)
    for c in range(seq // chunk):
        sl = slice(c * chunk, (c + 1) * chunk)
        pad_ref[edge + c * chunk:edge + (c + 1) * chunk, :] = cc_ref[0, sl, :] * cx_ref[0, sl, :]
    w = w_ref[...]
    for c in range(seq // chunk):
        lo = edge + c * chunk
        row = c * chunk + lax.broadcasted_iota(jnp.int32, (chunk, 1), 0)
        prev = pad_ref[lo - 1:lo - 1 + chunk, :]
        cur = pad_ref[lo:lo + chunk, :]
        nxt = pad_ref[lo + 1:lo + 1 + chunk, :]
        prev = jnp.where((row == 0) | (row == n_ctx), 0.0, prev)
        nxt = jnp.where((row == n_ctx - 1) | (row == seq - 1), 0.0, nxt)
        conv = w[0:1] * prev + w[1:2] * cur + w[2:3] * nxt
        o_ref[0, c * chunk:(c + 1) * chunk, :] = cb_ref[0, c * chunk:(c + 1) * chunk, :] * conv


def _short_conv(proj, conv_w, n_ctx):
    b, s, _ = proj.shape
    kernel = functools.partial(_conv_kernel, n_ctx=n_ctx, seq=s)
    return pl.pallas_call(
        kernel,
        grid=(b,),
        in_specs=[
            pl.BlockSpec((1, s, GROUP_W), lambda i: (i, 0, 5)),
            pl.BlockSpec((1, s, GROUP_W), lambda i: (i, 0, 6)),
            pl.BlockSpec((1, s, GROUP_W), lambda i: (i, 0, 7)),
            pl.BlockSpec((CONV_W, GROUP_W), lambda i: (0, 0)),
        ],
        out_specs=pl.BlockSpec((1, s, GROUP_W), lambda i: (i, 0, 0)),
        out_shape=jax.ShapeDtypeStruct((b, s, GROUP_W), F32),
        scratch_shapes=[pltpu.VMEM((s + 16, GROUP_W), F32)],
        compiler_params=_cparams(1),
        name="gated_short_conv",
    )(proj, proj, proj, conv_w)


OFFSET_GROUP = 16


def _hgrn_chunk(d, base, q_ref, z_ref, i_ref, gate_consts, o_ref, st_ref, kst, fst, vst, sst, pbuf, ones_bd, bd_mask):
    ch = SCAN_CHUNK
    sgn = 1 if d == 0 else -1
    lb_floor, one_minus_lb, floor_excess = gate_consts
    rows = pl.ds(base, ch)
    z = z_ref[0, rows, :]
    q = q_ref[0, rows, :]
    v = i_ref[0, rows, :]
    sig = 1.0 / (1.0 + jnp.exp(-z))
    f = lb_floor + one_minus_lb * sig
    k = one_minus_lb * (1.0 - sig) - floor_excess
    kst[d, ch:2 * ch, :] = k
    fst[d, ch:2 * ch, :] = f
    vst[d, ch:2 * ch, :] = v

    def shifted(ref, o):
        return ref[d, ch - sgn * o:2 * ch - sgn * o, :]

    inc = f
    exc = f
    step = 1
    while step < ch:
        sst[d, ch:2 * ch, :] = inc
        inc = inc * shifted(sst, step)
        step *= 2
    step = 1
    while step < ch:
        sst[d, ch:2 * ch, :] = exc
        exc = exc * shifted(sst, -step)
        step *= 2
    sst[d, ch:2 * ch, :] = exc
    exc = shifted(sst, -1)
    total = inc[ch - 1:ch, :] if d == 0 else inc[0:1, :]

    decay = jnp.ones((ch, GROUP_W), F32)
    acc = jnp.zeros((ch, GROUP_W), F32)
    for g0 in range(0, ch, OFFSET_GROUP):
        lo, hi = (g0, ch) if d == 0 else (0, ch - g0)
        n = hi - lo
        if g0 > 0:
            decay = decay[OFFSET_GROUP:] if d == 0 else decay[:n]
        qs = q[lo:hi]

        def window(ref, o):
            return ref[d, ch - sgn * o + lo:ch - sgn * o + hi, :]

        for oo in range(OFFSET_GROUP):
            o = g0 + oo
            if o > 0:
                decay = decay * window(fst, o - 1)
            pbuf[d, oo * n:(oo + 1) * n, :] = (qs * window(kst, o) * decay).astype(BF16)
        head_sums = _dot(pbuf[d, 0:OFFSET_GROUP * n, :], ones_bd)
        part = jnp.zeros((n, GROUP_W), F32)
        for oo in range(OFFSET_GROUP):
            part = part + head_sums[oo * n:(oo + 1) * n, :] * window(vst, g0 + oo)
        pieces = [part]
        if lo:
            pieces.insert(0, jnp.zeros((lo, GROUP_W), F32))
        if ch - hi:
            pieces.append(jnp.zeros((ch - hi, GROUP_W), F32))
        acc = acc + jnp.concatenate(pieces, axis=0)

    st = st_ref[d]
    o_inter = _dot_nt((q * inc).astype(BF16), st.astype(BF16))
    upd = _dot_tn(v.astype(BF16), (k * exc).astype(BF16))
    st_ref[d] = st * total + upd * bd_mask
    o_ref[0, rows, :] = o_ref[0, rows, :] + acc + o_inter


def _hgrn_kernel(q_ref, zf_ref, zb_ref, i_ref, lb_ref, o_ref, st_ref, kst, fst, vst, sst, pbuf, *, n_ctx, seq):
    ch = SCAN_CHUNK
    n_chunks = seq // ch
    ctx_chunks = n_ctx // ch
    lb = lb_ref[...]
    lb_floor = jnp.maximum(lb, LB_FLOOR)
    gate_consts = (lb_floor, 1.0 - lb, lb_floor - lb)
    hi = lax.broadcasted_iota(jnp.int32, (GROUP_W, GROUP_W), 0) // HEAD_DIM
    hj = lax.broadcasted_iota(jnp.int32, (GROUP_W, GROUP_W), 1) // HEAD_DIM
    bd_mask = jnp.where(hi == hj, 1.0, 0.0).astype(F32)
    ones_bd = bd_mask.astype(BF16)

    o_ref[...] = jnp.zeros(o_ref.shape, F32)
    st_ref[...] = jnp.zeros(st_ref.shape, F32)
    kst[...] = jnp.zeros(kst.shape, F32)
    fst[...] = jnp.zeros(fst.shape, F32)
    vst[...] = jnp.zeros(vst.shape, F32)
    sst[...] = jnp.ones(sst.shape, F32)

    def body(s, carry):
        cf = s
        cb = jnp.where(s < ctx_chunks, ctx_chunks - 1 - s, n_chunks - 1 - (s - ctx_chunks))
        for d, c, z_ref in ((0, cf, zf_ref), (1, cb, zb_ref)):
            base = pl.multiple_of(c * ch, ch)
            _hgrn_chunk(d, base, q_ref, z_ref, i_ref, gate_consts, o_ref, st_ref, kst, fst, vst, sst, pbuf,
                        ones_bd, bd_mask)
        return carry

    lax.fori_loop(0, n_chunks, body, 0)


def _hgrn(proj, lower_bound, n_ctx):
    b, s, _ = proj.shape
    ch = SCAN_CHUNK
    kernel = functools.partial(_hgrn_kernel, n_ctx=n_ctx, seq=s)
    col = lambda c: pl.BlockSpec((1, s, GROUP_W), lambda i: (i, 0, c))
    return pl.pallas_call(
        kernel,
        grid=(b,),
        in_specs=[col(8), col(9), col(10), col(11), pl.BlockSpec((1, GROUP_W), lambda i: (0, 0))],
        out_specs=pl.BlockSpec((1, s, GROUP_W), lambda i: (i, 0, 0)),
        out_shape=jax.ShapeDtypeStruct((b, s, GROUP_W), F32),
        scratch_shapes=[
            pltpu.VMEM((2, GROUP_W, GROUP_W), F32),
            pltpu.VMEM((2, 3 * ch, GROUP_W), F32),
            pltpu.VMEM((2, 3 * ch, GROUP_W), F32),
            pltpu.VMEM((2, 3 * ch, GROUP_W), F32),
            pltpu.VMEM((2, 3 * ch, GROUP_W), F32),
            pltpu.VMEM((2, OFFSET_GROUP * ch, GROUP_W), BF16),
        ],
        compiler_params=_cparams(1),
        name="hgrn2_bidirectional",
    )(proj, proj, proj, proj, lower_bound.reshape(1, GROUP_W))


def _post_kernel(ya_ref, yb_ref, yc_ref, od_ref, dg_ref, x_ref, mod_ref, gn_ref, wo_ref, nf_ref, wq_ref,
                 k1_ref, k2_ref, xo_ref, h_ref, sc_ref):
    yd = od_ref[0] * _silu(dg_ref[0])
    gn = gn_ref[...]
    parts = (ya_ref[0], yb_ref[0], yc_ref[0], yd)
    normed = [_rms(p) * gn[:, i * GROUP_W:(i + 1) * GROUP_W] for i, p in enumerate(parts)]
    y = _dot(jnp.concatenate(normed, axis=1).astype(BF16), wo_ref[...])
    xn = x_ref[0] + mod_ref[0, 2:3, :] * y
    xo_ref[0] = xn
    h = _rms(xn) * nf_ref[...] * (1.0 + mod_ref[0, 4:5, :]) + mod_ref[0, 3:4, :]
    h_ref[0] = h
    qv = _dot(h.astype(BF16), wq_ref[...])
    half = PEER_NKEYS
    for g in range(2 * PEER_HEADS):
        keys = k1_ref[...] if g % 2 == 0 else k2_ref[...]
        sc_ref[g] = _dot_nt(keys, qv[:, g * half:(g + 1) * half].astype(BF16))


def _post_mixer(ya, yb, yc, od, proj, xc, mods, gnorm, wo_bf16, nffn, wq_bf16, k1_bf16, k2_bf16, n_ctx_tiles, tile0):
    b, s, d = xc.shape
    tm = ROW_TILE
    nt = s // tm - tile0
    s_out = nt * tm
    ctx_cond = mods.shape[0] - 1
    nq = wq_bf16.shape[1]
    att0 = tile0 - (s - ya.shape[1]) // tm
    att = lambda: pl.BlockSpec((1, tm, GROUP_W), lambda i, j: (i, j + att0, 0))
    grp = lambda: pl.BlockSpec((1, tm, GROUP_W), lambda i, j: (i, j + tile0, 0))
    const = lambda shape: pl.BlockSpec(shape, lambda i, j: (0,) * len(shape))
    return pl.pallas_call(
        _post_kernel,
        grid=(b, nt),
        in_specs=[
            att(), att(), grp(), grp(),
            pl.BlockSpec((1, tm, GROUP_W), lambda i, j: (i, j + tile0, 12)),
            pl.BlockSpec((1, tm, d), lambda i, j: (i, j + tile0, 0)),
            pl.BlockSpec((1, N_MOD, d), lambda i, j: (jnp.where(j + tile0 < n_ctx_tiles, ctx_cond, i), 0, 0)),
            const((1, d)), const((d, d)), const((1, d)), const((d, nq)),
            const((PEER_NKEYS, PEER_NKEYS)), const((PEER_NKEYS, PEER_NKEYS)),
        ],
        out_specs=[
            pl.BlockSpec((1, tm, d), lambda i, j: (i, j, 0)),
            pl.BlockSpec((1, tm, d), lambda i, j: (i, j, 0)),
            pl.BlockSpec((2 * PEER_HEADS, PEER_NKEYS, tm), lambda i, j: (0, 0, i * nt + j)),
        ],
        out_shape=[
            jax.ShapeDtypeStruct((b, s_out, d), F32),
            jax.ShapeDtypeStruct((b, s_out, d), F32),
            jax.ShapeDtypeStruct((2 * PEER_HEADS, PEER_NKEYS, b * s_out), F32),
        ],
        compiler_params=_cparams(2),
        name="post_mixer",
    )(ya, yb, yc, od, proj, xc, mods, gnorm.reshape(1, d), wo_bf16, nffn.reshape(1, d), wq_bf16, k1_bf16, k2_bf16)


def _top16(s, rows=None):
    if rows is None:
        rows = lax.broadcasted_iota(jnp.int32, s.shape, 0).astype(F32)
    vals, ids = [], []
    cur = s
    for _ in range(PEER_TOPK):
        m = jnp.max(cur, axis=0, keepdims=True)
        am = jnp.min(jnp.where(cur == m, rows, jnp.inf), axis=0, keepdims=True)
        vals.append(m)
        ids.append(am)
        cur = jnp.where(rows == am, -jnp.inf, cur)
    return jnp.concatenate(vals, axis=0), jnp.concatenate(ids, axis=0)


def _candidate_rows(a):
    return PEER_TOPK // (a + 1)


def _pick(table, sel):
    out = jnp.zeros(sel.shape, table.dtype)
    for a in range(PEER_TOPK):
        out = out + jnp.where(sel == float(a), table[a:a + 1, :], 0.0)
    return out


def _route_kernel(sc_ref, idx_ref, gate_ref):
    ids, gates = [], []
    for h in range(PEER_HEADS):
        v1, i1 = _top16(sc_ref[2 * h])
        v2, i2 = _top16(sc_ref[2 * h + 1])
        nb = [_candidate_rows(a) for a in range(PEER_TOPK)]
        cand = jnp.concatenate([v1[a:a + 1, :] + v2[0:nb[a], :] for a in range(PEER_TOPK)], axis=0)
        flat = jnp.concatenate([lax.broadcasted_iota(jnp.int32, (nb[a], v1.shape[1]), 0).astype(F32)
                                + float(a * PEER_TOPK) for a in range(PEER_TOPK)], axis=0)
        top_s, pos = _top16(cand, flat)
        a_sel = jnp.floor(pos * (1.0 / PEER_TOPK))
        b_sel = pos - a_sel * PEER_TOPK
        ids.append(_pick(i1, a_sel) * PEER_NKEYS + _pick(i2, b_sel))
        e = jnp.exp(top_s - jnp.max(top_s, axis=0, keepdims=True))
        gates.append(e / jnp.sum(e, axis=0, keepdims=True))
    idx_ref[...] = jnp.concatenate(ids, axis=0).T.astype(jnp.int32)
    gate_ref[...] = jnp.concatenate(gates, axis=0).T


def _routing(scores_t):
    g, nk, t = scores_t.shape
    tt = TOPK_TOKENS
    per_tok = PEER_HEADS * PEER_TOPK
    out_spec = pl.BlockSpec((tt, per_tok), lambda i: (i, 0))
    return pl.pallas_call(
        _route_kernel,
        grid=(t // tt,),
        in_specs=[pl.BlockSpec((g, nk, tt), lambda i: (0, 0, i))],
        out_specs=[out_spec, out_spec],
        out_shape=[jax.ShapeDtypeStruct((t, per_tok), jnp.int32),
                   jax.ShapeDtypeStruct((t, per_tok), F32)],
        compiler_params=_cparams(1),
        name="peer_routing",
    )(scores_t)


def _gelu_tanh(x):
    return 0.5 * x * (1.0 + jnp.tanh(math.sqrt(2.0 / math.pi) * (x + 0.044715 * (x * x * x))))


FOLD = (8, 128)


def _sublane_pair_sum(x, y, k):
    sub = lax.broadcasted_iota(jnp.int32, FOLD, 0)
    keep = (sub % (2 * k)) < k
    return jnp.where(keep, x, pltpu.roll(y, k, 0)) + jnp.where(keep, pltpu.roll(x, FOLD[0] - k, 0), y)


def _sublane_sums(p):
    z = [_sublane_pair_sum(p[j], p[j + 4], 4) for j in range(4)]
    w = [_sublane_pair_sum(z[0], z[2], 2), _sublane_pair_sum(z[1], z[3], 2)]
    return _sublane_pair_sum(w[0], w[1], 1)


LAND_SLOTS = PEER_TOKENS * PEER_HEADS * PEER_TOPK
COPY_GROUP = 64
REC = 3 * LAND_SLOTS


def _peer_kernel(rec_hbm, tbl_hbm, h_ref, gate_ref, x_ref, mod_ref, fn_ref, o_ref, rec_smem, tbl, act_rep,
                 tbl_sem, rec_sem, land_sem, *, n_steps, n_res, final):
    j = pl.program_id(0)
    tb = PEER_TOKENS
    per_tok = PEER_HEADS * PEER_TOPK
    sub = FOLD[0]

    def rec_copy(r, region):
        return pltpu.make_async_copy(rec_hbm.at[pl.ds(r * REC, REC)],
                                     rec_smem.at[pl.ds(region * REC, REC)], rec_sem.at[region])

    def issue_next(region, buf):
        n_groups = rec_smem[region * REC + 2 * LAND_SLOTS]

        def body(g, carry):
            for k in range(COPY_GROUP):
                e = rec_smem[region * REC + LAND_SLOTS + g * COPY_GROUP + k]
                pltpu.make_async_copy(tbl_hbm.at[e], tbl.at[n_res + buf * LAND_SLOTS + g * COPY_GROUP + k],
                                      land_sem.at[buf]).start(priority=k % 2)
            return carry
        lax.fori_loop(0, n_groups, body, 0)

    def wait_landing(region, buf):
        n_groups = rec_smem[region * REC + 2 * LAND_SLOTS + 1]

        def body(g, carry):
            grp = tbl.at[pl.ds(n_res + buf * LAND_SLOTS, COPY_GROUP)]
            pltpu.make_async_copy(grp, grp, land_sem.at[buf]).wait()
            return carry
        lax.fori_loop(0, n_groups, body, 0)

    g2 = mod_ref[0, 5]
    hi_mask = jnp.uint32(0xFFFF0000)

    def consume(half):
        for t in range(tb):
            tok = half * tb + t
            hf = h_ref[tok]
            rows = [rec_smem[half * REC + t * per_tok + e] for e in range(per_tok)]
            parts = []
            for g in range(per_tok // sub):
                prods = []
                for k in range(sub):
                    word = tbl[rows[g * sub + k]]
                    prods.append(pltpu.bitcast(word << 16, F32) * hf)
                parts.append(_sublane_sums(prods))
            s_row = jnp.sum(jnp.concatenate(parts, axis=0).T, axis=0, keepdims=True)
            act_row = _gelu_tanh(s_row) * gate_ref[tok:tok + 1, :]
            act_rep[tok] = jnp.broadcast_to(act_row, (FOLD[1], per_tok)).T
            accs = [jnp.zeros(FOLD, F32) for _ in range(4)]
            for e in range(per_tok):
                a = jnp.broadcast_to(act_rep[tok, e:e + 1, :], FOLD)
                v = pltpu.bitcast(tbl[rows[e]] & hi_mask, F32)
                accs[e % 4] = accs[e % 4] + a * v
            xn = x_ref[tok] + g2 * ((accs[0] + accs[1]) + (accs[2] + accs[3]))
            if final:
                ms = (jnp.sum(jnp.sum(xn * xn, axis=1, keepdims=True), axis=0, keepdims=True)
                      * (1.0 / (FOLD[0] * FOLD[1])))
                xn = xn * lax.rsqrt(ms + EPS) * fn_ref[...]
            o_ref[tok] = xn

    @pl.when(j == 0)
    def _():
        whole = pltpu.make_async_copy(tbl_hbm.at[pl.ds(0, n_res)], tbl.at[pl.ds(0, n_res)], tbl_sem)
        whole.start()
        first = rec_copy(0, 1)
        first.start()
        first.wait()
        issue_next(1, 0)
        rec_copy(1, 0).start()
        rec_copy(2, 1).start()
        whole.wait()

    more = j + 1 < n_steps
    rec_copy(2 * j + 1, 0).wait()
    issue_next(0, 1)
    wait_landing(0, 0)
    consume(0)

    @pl.when(more)
    def _():
        rec_copy(2 * j + 3, 0).start()

    rec_copy(2 * j + 2, 1).wait()
    issue_next(1, 0)
    wait_landing(1, 1)
    consume(1)

    @pl.when(more)
    def _():
        rec_copy(2 * j + 4, 1).start()


def _peer_records(idx, n_res, n_exp):
    blocks = idx.reshape(-1, LAND_SLOTS)
    nb = blocks.shape[0]
    away = blocks >= n_res
    rank = jnp.cumsum(away.astype(jnp.int32), axis=1) - 1
    n_groups = (jnp.sum(away.astype(jnp.int32), axis=1) + COPY_GROUP - 1) // COPY_GROUP
    buf = (jnp.arange(nb, dtype=jnp.int32) % 2)[:, None]
    rows = jnp.where(away, n_res + buf * LAND_SLOTS + rank, blocks)
    id_bits = (n_exp - 1).bit_length()
    pos_bits = (LAND_SLOTS - 1).bit_length()
    assert id_bits + pos_bits < 31
    pos = jnp.arange(LAND_SLOTS, dtype=jnp.int32)[None, :]
    packed = (jnp.where(away, 0, 1 << (id_bits + pos_bits)) | (pos << id_bits) | blocks).astype(jnp.int32)
    copy_list = lax.sort(packed, dimension=1) & ((1 << id_bits) - 1)
    zero_row = jnp.zeros((1, LAND_SLOTS), jnp.int32)
    counts = jnp.zeros((nb + 1, LAND_SLOTS), jnp.int32)
    counts = counts.at[:nb, 0].set(n_groups).at[1:, 1].set(n_groups)
    rec = jnp.concatenate([jnp.concatenate([zero_row, rows], axis=0),
                           jnp.concatenate([copy_list, zero_row], axis=0), counts], axis=1)
    return rec.reshape(-1)


def _peer_experts(h2, idx, gate, uv, x_mid, mods, final_gain, s_out, ctx_rows, final):
    t = h2.shape[0]
    tb = 2 * PEER_TOKENS
    per_tok = PEER_HEADS * PEER_TOPK
    n_steps = t // tb
    n_res = min(uv.shape[0], PEER_RESIDENT_BYTES // (4 * FOLD[0] * FOLD[1]))
    assert t % tb == 0 and s_out % tb == 0 and ctx_rows % tb == 0
    ctx_cond = mods.shape[0] - 1

    def mod_index(i):
        tok = i * tb
        return (jnp.where(tok % s_out < ctx_rows, ctx_cond, tok // s_out), 0, 0, 0)

    kernel = functools.partial(_peer_kernel, n_steps=n_steps, n_res=n_res, final=final)
    fold_spec = pl.BlockSpec((tb,) + FOLD, lambda i: (i, 0, 0))
    return pl.pallas_call(
        kernel,
        grid=(n_steps,),
        in_specs=[
            pl.BlockSpec(memory_space=pl.ANY),
            pl.BlockSpec(memory_space=pl.ANY),
            fold_spec,
            pl.BlockSpec((tb, per_tok), lambda i: (i, 0)),
            fold_spec,
            pl.BlockSpec((1, N_MOD) + FOLD, mod_index),
            pl.BlockSpec(FOLD, lambda i: (0, 0)),
        ],
        out_specs=fold_spec,
        out_shape=jax.ShapeDtypeStruct((t,) + FOLD, F32),
        scratch_shapes=[
            pltpu.SMEM((2 * REC,), jnp.int32),
            pltpu.VMEM((n_res + 2 * LAND_SLOTS,) + FOLD, jnp.uint32),
            pltpu.VMEM((2 * PEER_TOKENS, per_tok, FOLD[1]), F32),
            pltpu.SemaphoreType.DMA,
            pltpu.SemaphoreType.DMA((2,)),
            pltpu.SemaphoreType.DMA((2,)),
        ],
        compiler_params=_cparams(1, PEER_VMEM_LIMIT),
        name="peer_experts",
    )(_peer_records(idx, n_res, uv.shape[0]), uv, h2, gate, x_mid, mods.reshape(mods.shape[0], N_MOD, *FOLD),
      final_gain.reshape(FOLD))


def _pack_experts(u, v):
    n_exp = u.shape[0]
    ub = lax.bitcast_convert_type(u.astype(BF16), jnp.uint16).astype(jnp.uint32)
    vb = lax.bitcast_convert_type(v.astype(BF16), jnp.uint16).astype(jnp.uint32)
    return ((vb << 16) | ub).reshape(n_exp, *FOLD)


def _rope_tables(n_ctx, n_lat):
    t = jnp.arange(n_lat)
    inv = ROPE_THETA ** (-jnp.arange(0, AXIS_DIM, 2, dtype=F32) / AXIS_DIM)
    row = (t // GRID_W).astype(F32)[:, None] * inv
    col = (t % GRID_W).astype(F32)[:, None] * inv
    cos = jnp.concatenate([jnp.cos(row), jnp.cos(row), jnp.cos(col), jnp.cos(col)], axis=1)
    sin = jnp.concatenate([-jnp.sin(row), jnp.sin(row), -jnp.sin(col), jnp.sin(col)], axis=1)
    cos = jnp.concatenate([jnp.ones((n_ctx, HEAD_DIM), F32), cos], axis=0)
    sin = jnp.concatenate([jnp.zeros((n_ctx, HEAD_DIM), F32), sin], axis=0)
    return jnp.tile(cos, (1, A_HEADS)), jnp.tile(sin, (1, A_HEADS))


def kernel(x, c, ctx, c_ctx, w_mod, b_mod, norm_mix, norm_ffn, w_in, conv_w, attn_sink, na_rpb, lb_logits, group_norm, w_out, peer_wq, peer_k1, peer_k2, peer_u, peer_v, final_norm):
    b, n_lat, d = x.shape
    n_ctx = ctx.shape[1]
    depth = w_mod.shape[0]
    seq = n_ctx + n_lat
    assert n_ctx % ROW_TILE == 0 and n_lat % ROW_TILE == 0 and n_lat // GRID_W >= NB_ROWS
    assert n_lat >= WIN_BLOCK + 2 * WINDOW and w_in.shape[2] == IN_W
    n_ctx_tiles = n_ctx // ROW_TILE

    n_cond = b + 1
    pad = (-n_cond) % 8
    cvec = jnp.concatenate([c, c_ctx[None, :], jnp.zeros((pad, d), F32)], axis=0)
    mods = _modulation(cvec, w_mod, b_mod)[:, :n_cond].reshape(depth, n_cond, N_MOD, d)

    lb_soft = jax.nn.softmax(lb_logits.astype(F32), axis=0)
    lower_bounds = jnp.cumsum(lb_soft, axis=0) - lb_soft[0:1]
    cos_t, sin_t = _rope_tables(n_ctx, n_lat)

    xc = jnp.concatenate([ctx, x], axis=1)
    out = None
    for l in range(depth):
        last = l == depth - 1
        proj = _in_projection(xc, mods[l], norm_mix[l], w_in[l].astype(BF16), n_ctx_tiles)
        ya = _attention_a(proj, cos_t, sin_t, attn_sink[l], n_ctx, not last)
        yb = _attention_b(proj, _neighbourhood_bias(na_rpb[l]), n_ctx, not last)
        yc = _short_conv(proj, conv_w[l], n_ctx)
        od = _hgrn(proj, lower_bounds[l], n_ctx)
        tile0 = n_ctx_tiles if last else 0
        x_mid, h2, scores_t = _post_mixer(ya, yb, yc, od, proj, xc, mods[l], group_norm[l], w_out[l].astype(BF16),
                                          norm_ffn[l], peer_wq[l].astype(BF16), peer_k1[l].astype(BF16),
                                          peer_k2[l].astype(BF16), n_ctx_tiles, tile0)
        s_out = x_mid.shape[1]
        tokens = b * s_out
        idx, gate = _routing(scores_t)
        idx = idx.reshape(-1)
        uv = _pack_experts(peer_u[l], peer_v[l])
        res = _peer_experts(h2.reshape(tokens, *FOLD), idx, gate, uv, x_mid.reshape(tokens, *FOLD), mods[l],
                            final_norm, s_out, 0 if last else n_ctx, last)
        if last:
            out = res.reshape(b, s_out, d)
        else:
            xc = res.reshape(b, s_out, d)
    return out
```
